```python
import jax, jax.numpy as jnp
from jax import lax
import numpy as np

D_MODEL = 2048
BATCH = 2
SEQ = 4096
DEPTH = 1

CHUNK = 64
D_MIX = D_MODEL
D_ATTN = D_MIX // 2
D_POOL = D_MIX - D_ATTN
HEAD_DIM = 128
N_HEADS = D_ATTN // HEAD_DIM
POOL_WINDOWS = (2, 4, 8, 16)
N_POOL_GROUPS = len(POOL_WINDOWS)
POOL_GROUP_DIM = D_POOL // N_POOL_GROUPS
D_FF = ((8 * D_MODEL // 3 + 127) // 128) * 128
Q_BLOCK = 128
N_MOD = 9
D_IN_PROJ = 3 * D_ATTN + N_HEADS + D_POOL
EPS = 1e-6

kernel_name = "hybrid_fox_pool_macaron_block"


def _rmsnorm(x, g):
    xf = x.astype(jnp.float32)
    xf = xf * lax.rsqrt(jnp.mean(xf * xf, axis=-1, keepdims=True) + EPS)
    return xf.astype(x.dtype) * g


def _modulate(h, shift, scale):
    return h * (1.0 + scale[:, None, :]) + shift[:, None, :]


def _swiglu(h, w_in, w_out):
    a, b = jnp.split(h @ w_in, 2, axis=-1)
    return (jax.nn.silu(a) * b) @ w_out


def _forgetting_attention(q, k, v, log_f):
    S = q.shape[2]
    scale = HEAD_DIM ** -0.5
    F = jnp.cumsum(log_f, axis=-1)
    outs = []
    for i in range(S // Q_BLOCK):
        qs, qe = i * Q_BLOCK, (i + 1) * Q_BLOCK
        qb = q[:, :, qs:qe]
        kb = k[:, :, :qe]
        vb = v[:, :, :qe]
        logits = jnp.einsum('bhqd,bhkd->bhqk', qb, kb).astype(jnp.float32) * scale
        logits = logits + (F[:, :, qs:qe, None] - F[:, :, None, :qe])
        causal = (qs + jnp.arange(Q_BLOCK))[:, None] >= jnp.arange(qe)[None, :]
        logits = jnp.where(causal[None, None], logits, -jnp.inf)
        p = jax.nn.softmax(logits, axis=-1)
        outs.append(jnp.einsum('bhqk,bhkd->bhqd', p.astype(vb.dtype), vb))
    return jnp.concatenate(outs, axis=2)


def _multiscale_pool(u, pool_w, pool_scale):
    B, S, _ = u.shape
    ug = u.reshape(B, S, N_POOL_GROUPS, POOL_GROUP_DIM)
    pos = jnp.arange(S)
    pooled = []
    for g, w in enumerate(POOL_WINDOWS):
        xg = ug[:, :, g].astype(jnp.float32)
        cs0 = jnp.pad(jnp.cumsum(xg, axis=1), ((0, 0), (1, 0), (0, 0)))
        lag = jnp.pad(cs0, ((0, 0), (w - 1, 0), (0, 0)))[:, :S]
        count = jnp.minimum(pos + 1, w).astype(jnp.float32)[None, :, None]
        mean = (cs0[:, 1:] - lag) / count
        pooled.append((mean - xg).astype(u.dtype))
    p = jnp.stack(pooled, axis=2)
    p = jnp.einsum('bsgc,gcd->bsgd', p, pool_w)
    return p.reshape(B, S, D_POOL) * pool_scale


def _hybrid_mixer(h, w_in, b_forget, q_norm_g, k_norm_g, pool_w, pool_scale, w_out):
    B, S, _ = h.shape
    proj = h @ w_in
    q, k, v, f_logit, u = jnp.split(
        proj, [D_ATTN, 2 * D_ATTN, 3 * D_ATTN, 3 * D_ATTN + N_HEADS], axis=-1)

    def heads(t):
        return t.reshape(B, S, N_HEADS, HEAD_DIM).transpose(0, 2, 1, 3)

    q = _rmsnorm(heads(q), q_norm_g)
    k = _rmsnorm(heads(k), k_norm_g)
    v = heads(v)
    log_f = jax.nn.log_sigmoid((f_logit + b_forget).astype(jnp.float32)).transpose(0, 2, 1)
    attn = _forgetting_attention(q, k, v, log_f)
    attn = attn.transpose(0, 2, 1, 3).reshape(B, S, D_ATTN)
    pool = _multiscale_pool(u, pool_w, pool_scale)
    return jnp.concatenate([attn, pool], axis=-1) @ w_out


def _nrm(k, shape, scale):
    return jax.random.normal(k, shape, jnp.float32) * scale


def setup_inputs(seed: int = 0) -> dict:
    key = jax.random.key(seed)
    ks = jax.random.split(key, 20)
    L = DEPTH
    return {
        "x": _nrm(ks[0], (BATCH, SEQ, D_MODEL), 1.0),
        "c": _nrm(ks[1], (BATCH, D_MODEL), 1.0),
        "w_ada": _nrm(ks[2], (L, D_MODEL, N_MOD * D_MODEL), 0.5 * D_MODEL ** -0.5),
        "b_ada": _nrm(ks[3], (L, N_MOD * D_MODEL), 0.02),
        "ffn1_norm_g": 1.0 + _nrm(ks[4], (L, D_MODEL), 0.05),
        "ffn1_w_in": _nrm(ks[5], (L, D_MODEL, 2 * D_FF), D_MODEL ** -0.5),
        "ffn1_w_out": _nrm(ks[6], (L, D_FF, D_MODEL), D_FF ** -0.5),
        "mix_norm_g": 1.0 + _nrm(ks[7], (L, D_MODEL), 0.05),
        "w_in": _nrm(ks[8], (L, D_MODEL, D_IN_PROJ), D_MODEL ** -0.5),
        "b_forget": jax.random.uniform(ks[9], (L, N_HEADS), jnp.float32, 1.0, 4.0),
        "q_norm_g": 1.0 + _nrm(ks[10], (L, HEAD_DIM), 0.05),
        "k_norm_g": 1.0 + _nrm(ks[11], (L, HEAD_DIM), 0.05),
        "pool_w": _nrm(ks[12], (L, N_POOL_GROUPS, POOL_GROUP_DIM, POOL_GROUP_DIM), POOL_GROUP_DIM ** -0.5),
        "pool_scale": 1.0 + _nrm(ks[13], (L, D_POOL), 0.1),
        "w_out": _nrm(ks[14], (L, D_MIX, D_MODEL), D_MIX ** -0.5),
        "ffn2_norm_g": 1.0 + _nrm(ks[15], (L, D_MODEL), 0.05),
        "ffn2_w_in": _nrm(ks[16], (L, D_MODEL, 2 * D_FF), D_MODEL ** -0.5),
        "ffn2_w_out": _nrm(ks[17], (L, D_FF, D_MODEL), D_FF ** -0.5),
        "final_norm_g": 1.0 + _nrm(ks[18], (D_MODEL,), 0.05),
    }


def reference(x, c, w_ada, b_ada, ffn1_norm_g, ffn1_w_in, ffn1_w_out, mix_norm_g,
              w_in, b_forget, q_norm_g, k_norm_g, pool_w, pool_scale, w_out,
              ffn2_norm_g, ffn2_w_in, ffn2_w_out, final_norm_g):
    c_act = jax.nn.silu(c)
    for l in range(DEPTH):
        mod = c_act @ w_ada[l] + b_ada[l]
        sh1, sc1, g1, sh2, sc2, g2, sh3, sc3, g3 = jnp.split(mod, N_MOD, axis=-1)
        h = _modulate(_rmsnorm(x, ffn1_norm_g[l]), sh1, sc1)
        x = x + 0.5 * g1[:, None, :] * _swiglu(h, ffn1_w_in[l], ffn1_w_out[l])
        h = _modulate(_rmsnorm(x, mix_norm_g[l]), sh2, sc2)
        x = x + g2[:, None, :] * _hybrid_mixer(h, w_in[l], b_forget[l], q_norm_g[l], k_norm_g[l],
                                              pool_w[l], pool_scale[l], w_out[l])
        h = _modulate(_rmsnorm(x, ffn2_norm_g[l]), sh3, sc3)
        x = x + 0.5 * g3[:, None, :] * _swiglu(h, ffn2_w_in[l], ffn2_w_out[l])
    return _rmsnorm(x, final_norm_g)
```

```python
import functools

import jax
import jax.numpy as jnp
from jax import lax
from jax.experimental import pallas as pl
from jax.experimental.pallas import tpu as pltpu

F32 = jnp.float32
BF16 = jnp.bfloat16

EPS = 1e-6
HEAD_DIM = 128
POOL_WINDOWS = (2, 4, 8, 16)
LANES = 128
MXU_DIM = 256
MIB = 1024 * 1024


def _cparams(dims, vmem_mib):
    return pltpu.CompilerParams(dimension_semantics=dims,
                                vmem_limit_bytes=vmem_mib * MIB)


def _norm_mod(x, g, shift, scale):
    ms = jnp.mean(x * x, axis=-1, keepdims=True)
    xn = x * lax.rsqrt(ms + EPS) * g
    return xn * (1.0 + scale) + shift


def _ada_kernel(c_ref, w_ref, b_ref, o_ref):
    c = c_ref[...]
    ca = c * (1.0 / (1.0 + jnp.exp(-c)))
    o_ref[...] = jnp.dot(ca.astype(BF16), w_ref[...].astype(BF16),
                         preferred_element_type=F32) + b_ref[...]


def _ada_call(c_pad, w, b, tn=1024):
    m, d = c_pad.shape
    n = w.shape[1]
    return pl.pallas_call(
        _ada_kernel,
        grid=(n // tn,),
        in_specs=[pl.BlockSpec((m, d), lambda j: (0, 0)),
                  pl.BlockSpec((d, tn), lambda j: (0, j)),
                  pl.BlockSpec((1, tn), lambda j: (0, j))],
        out_specs=pl.BlockSpec((m, tn), lambda j: (0, j)),
        out_shape=jax.ShapeDtypeStruct((m, n), F32),
        compiler_params=_cparams(("arbitrary",), 40),
    )(c_pad, w, b)


def _ffn_kernel(x_ref, g_ref, sh_ref, sc_ref, gate_ref, win_ref, wout_ref,
                *rest, nj, tf, final_norm):
    if final_norm:
        fg_ref, o_ref, h_scr = rest
    else:
        o_ref, h_scr = rest
    j = pl.program_id(1)

    @pl.when(j == 0)
    def _():
        h = _norm_mod(x_ref[...], g_ref[...], sh_ref[0], sc_ref[0])
        h_scr[...] = h.astype(BF16)
        o_ref[...] = jnp.zeros_like(o_ref)

    hw = jnp.dot(h_scr[...], win_ref[...], preferred_element_type=F32)
    a = hw[:, :tf]
    b = hw[:, tf:]
    act = (a * (1.0 / (1.0 + jnp.exp(-a))) * b).astype(BF16)
    o_ref[...] += jnp.dot(act, wout_ref[...], preferred_element_type=F32)

    @pl.when(j == nj - 1)
    def _():
        y = x_ref[...] + 0.5 * gate_ref[0] * o_ref[...]
        if final_norm:
            ms = jnp.mean(y * y, axis=-1, keepdims=True)
            y = y * lax.rsqrt(ms + EPS) * fg_ref[...]
        o_ref[...] = y


def _ffn_call(x, g, shift, scale, gate, w_in_p, w_out_p, final_g, *, seq, tm=512,
              tf=MXU_DIM):
    t, d = x.shape
    nj = w_out_p.shape[0] // tf
    tiles_per_seq = seq // tm
    mod_spec = pl.BlockSpec((1, 1, d), lambda i, j: (i // tiles_per_seq, 0, 0))
    vec_spec = pl.BlockSpec((1, d), lambda i, j: (0, 0))
    in_specs = [pl.BlockSpec((tm, d), lambda i, j: (i, 0)),
                vec_spec, mod_spec, mod_spec, mod_spec,
                pl.BlockSpec((d, 2 * tf), lambda i, j: (0, j)),
                pl.BlockSpec((tf, d), lambda i, j: (j, 0))]
    args = [x, g, shift, scale, gate, w_in_p, w_out_p]
    if final_g is not None:
        in_specs.append(vec_spec)
        args.append(final_g)
    return pl.pallas_call(
        functools.partial(_ffn_kernel, nj=nj, tf=tf, final_norm=final_g is not None),
        grid=(t // tm, nj),
        in_specs=in_specs,
        out_specs=pl.BlockSpec((tm, d), lambda i, j: (i, 0)),
        out_shape=jax.ShapeDtypeStruct((t, d), F32),
        scratch_shapes=[pltpu.VMEM((tm, d), BF16)],
        compiler_params=_cparams(("parallel", "arbitrary"), 48),
    )(*args)


def _pack_ffn_weights(w_in, w_out, tf=MXU_DIM):
    d, two_ff = w_in.shape
    dff = two_ff // 2
    nj = -(-dff // tf)
    pad = nj * tf - dff
    a = jnp.pad(w_in[:, :dff], ((0, 0), (0, pad))).reshape(d, nj, tf)
    b = jnp.pad(w_in[:, dff:], ((0, 0), (0, pad))).reshape(d, nj, tf)
    w_in_p = jnp.concatenate([a, b], axis=2).reshape(d, nj * 2 * tf).astype(BF16)
    w_out_p = jnp.pad(w_out, ((0, pad), (0, 0))).astype(BF16)
    return w_in_p, w_out_p


def _inproj_kernel(x_ref, g_ref, sh_ref, sc_ref, w_ref, wf_ref, qg_ref, kg_ref,
                   qkv_ref, u_ref, f_ref, h_scr, *, tn, n_qk, n_qkv):
    j = pl.program_id(1)

    @pl.when(j == 0)
    def _():
        h = _norm_mod(x_ref[...], g_ref[...], sh_ref[0], sc_ref[0])
        h_scr[...] = h.astype(BF16)
        f_ref[...] = jnp.dot(h_scr[...], wf_ref[...], preferred_element_type=F32)

    acc = jnp.dot(h_scr[...], w_ref[...], preferred_element_type=F32)

    @pl.when(j < n_qk)
    def _():
        gain = jnp.where(j < n_qk // 2, qg_ref[...], kg_ref[...])
        for hh in range(tn // HEAD_DIM):
            sl = slice(hh * HEAD_DIM, (hh + 1) * HEAD_DIM)
            xh = acc[:, sl]
            ms = jnp.mean(xh * xh, axis=-1, keepdims=True)
            qkv_ref[:, sl] = (xh * lax.rsqrt(ms + EPS) * gain).astype(BF16)

    @pl.when(jnp.logical_and(j >= n_qk, j < n_qkv))
    def _():
        qkv_ref[...] = acc.astype(BF16)

    @pl.when(j >= n_qkv)
    def _():
        u_ref[...] = acc


def _inproj_call(x, g, shift, scale, w_main, w_f, q_gain, k_gain, *, seq, d_attn,
                 tm=512, tn=512):
    t, d = x.shape
    n_main = w_main.shape[1]
    nj = n_main // tn
    n_qk = 2 * d_attn // tn
    n_qkv = 3 * d_attn // tn
    d_pool = n_main - 3 * d_attn
    tiles_per_seq = seq // tm
    mod_spec = pl.BlockSpec((1, 1, d), lambda i, j: (i // tiles_per_seq, 0, 0))
    return pl.pallas_call(
        functools.partial(_inproj_kernel, tn=tn, n_qk=n_qk, n_qkv=n_qkv),
        grid=(t // tm, nj),
        in_specs=[pl.BlockSpec((tm, d), lambda i, j: (i, 0)),
                  pl.BlockSpec((1, d), lambda i, j: (0, 0)),
                  mod_spec, mod_spec,
                  pl.BlockSpec((d, tn), lambda i, j: (0, j)),
                  pl.BlockSpec((d, LANES), lambda i, j: (0, 0)),
                  pl.BlockSpec((1, HEAD_DIM), lambda i, j: (0, 0)),
                  pl.BlockSpec((1, HEAD_DIM), lambda i, j: (0, 0))],
        out_specs=[
            pl.BlockSpec((tm, tn), lambda i, j: (i, jnp.minimum(j, n_qkv - 1))),
            pl.BlockSpec((tm, tn), lambda i, j: (i, jnp.maximum(j - n_qkv, 0))),
            pl.BlockSpec((tm, LANES), lambda i, j: (i, 0))],
        out_shape=[jax.ShapeDtypeStruct((t, 3 * d_attn), BF16),
                   jax.ShapeDtypeStruct((t, d_pool), F32),
                   jax.ShapeDtypeStruct((t, LANES), F32)],
        scratch_shapes=[pltpu.VMEM((tm, d), BF16)],
        compiler_params=_cparams(("parallel", "arbitrary"), 40),
    )(x, g, shift, scale, w_main, w_f, q_gain, k_gain)


def _forget_kernel(f_ref, b_ref, o_ref, pad_scr, *, seq):
    z = f_ref[0] + b_ref[...]
    x = jnp.minimum(z, 0.0) - jnp.log1p(jnp.exp(-jnp.abs(z)))
    pad_scr[pl.ds(0, seq), :] = jnp.zeros((seq, LANES), F32)
    d = 1
    while d < seq:
        pad_scr[pl.ds(seq, seq), :] = x
        x = x + pad_scr[pl.ds(seq - d, seq), :]
        d *= 2
    o_ref[0] = x


def _forget_call(f_logit, b_pad):
    nb, seq, _ = f_logit.shape
    return pl.pallas_call(
        functools.partial(_forget_kernel, seq=seq),
        grid=(nb,),
        in_specs=[pl.BlockSpec((1, seq, LANES), lambda b: (b, 0, 0)),
                  pl.BlockSpec((1, LANES), lambda b: (0, 0))],
        out_specs=pl.BlockSpec((1, seq, LANES), lambda b: (b, 0, 0)),
        out_shape=jax.ShapeDtypeStruct((nb, seq, LANES), F32),
        scratch_shapes=[pltpu.VMEM((2 * seq, LANES), F32)],
        compiler_params=_cparams(("parallel",), 40),
    )(f_logit, b_pad)


POOL_HALO = 16
POOL_ROWS = 512


def _pool_kernel(u_ref, w_ref, s_ref, o_ref, pad_scr, *, seq):
    g = pl.program_id(1)
    cg = u_ref.shape[2]
    pad_scr[pl.ds(0, POOL_HALO), :] = jnp.zeros((POOL_HALO, cg), F32)
    pad_scr[pl.ds(POOL_HALO, seq), :] = u_ref[0]
    w = w_ref[0].astype(BF16)
    scale = s_ref[...]
    for gi, win in enumerate(POOL_WINDOWS):
        @pl.when(g == gi)
        def _(win=win):
            for r0 in range(0, seq, POOL_ROWS):
                tok = pad_scr[pl.ds(POOL_HALO + r0, POOL_ROWS), :]
                tot = tok
                for dd in range(1, win):
                    tot = tot + pad_scr[pl.ds(POOL_HALO + r0 - dd, POOL_ROWS), :]
                pos = r0 + lax.broadcasted_iota(jnp.int32, (POOL_ROWS, cg), 0)
                cnt = jnp.minimum(pos + 1, win).astype(F32)
                p = tot / cnt - tok
                y = jnp.dot(p.astype(BF16), w, preferred_element_type=F32) * scale
                o_ref[0, pl.ds(r0, POOL_ROWS), :] = y.astype(BF16)


def _pool_call(u, pool_w, pool_scale):
    nb, seq, d_pool = u.shape
    ng, cg, _ = pool_w.shape
    return pl.pallas_call(
        functools.partial(_pool_kernel, seq=seq),
        grid=(nb, ng),
        in_specs=[pl.BlockSpec((1, seq, cg), lambda b, g: (b, 0, g)),
                  pl.BlockSpec((1, cg, cg), lambda b, g: (g, 0, 0)),
                  pl.BlockSpec((1, cg), lambda b, g: (0, g))],
        out_specs=pl.BlockSpec((1, seq, cg), lambda b, g: (b, 0, g)),
        out_shape=jax.ShapeDtypeStruct((nb, seq, d_pool), BF16),
        scratch_shapes=[pltpu.VMEM((POOL_HALO + seq, cg), F32)],
        compiler_params=_cparams(("parallel", "arbitrary"), 40),
    )(u, pool_w, pool_scale)


NEG_BIG = -1e30


def _attn_kernel(q_ref, k_ref, v_ref, fc_ref, fr_ref, o_ref, m_scr, l_scr, acc_scr,
                 *, tq):
    qi = pl.program_id(2)
    q = q_ref[...]
    fcol = fc_ref[0, 0]
    m_scr[...] = jnp.full_like(m_scr, NEG_BIG)
    l_scr[...] = jnp.zeros_like(l_scr)
    acc_scr[...] = jnp.zeros_like(acc_scr)

    def step(kj, masked):
        start = pl.multiple_of(kj * tq, tq)
        k = k_ref[pl.ds(start, tq), :]
        v = v_ref[pl.ds(start, tq), :]
        frow = fr_ref[0, 0, pl.ds(kj, 1), :]
        s = lax.dot_general(q, k, (((1,), (1,)), ((), ())),
                            preferred_element_type=F32)
        s = s + (fcol - frow)
        if masked:
            row = lax.broadcasted_iota(jnp.int32, (tq, tq), 0)
            col = lax.broadcasted_iota(jnp.int32, (tq, tq), 1)
            s = jnp.where(row >= col, s, NEG_BIG)
        m_prev = m_scr[...]
        m_new = jnp.maximum(m_prev, jnp.max(s, axis=-1, keepdims=True))
        alpha = jnp.exp(m_prev - m_new)
        p = jnp.exp(s - m_new)
        if masked:
            p = jnp.where(row >= col, p, 0.0)
        l_scr[...] = alpha * l_scr[...] + jnp.sum(p, axis=-1, keepdims=True)
        acc_scr[...] = alpha * acc_scr[...] + jnp.dot(
            p.astype(BF16), v, preferred_element_type=F32)
        m_scr[...] = m_new

    def body(kj, carry):
        step(kj, False)
        return carry

    lax.fori_loop(0, qi, body, 0)
    step(qi, True)
    o_ref[...] = (acc_scr[...] / l_scr[...]).astype(BF16)


def _attn_call(qkv, fcol, frow, *, nb, seq, n_heads, tq=512):
    t = qkv.shape[0]
    nq = seq // tq
    d_attn = n_heads * HEAD_DIM
    return pl.pallas_call(
        functools.partial(_attn_kernel, tq=tq),
        grid=(nb, n_heads, nq),
        in_specs=[
            pl.BlockSpec((tq, HEAD_DIM), lambda b, h, i: (b * nq + i, h)),
            pl.BlockSpec((seq, HEAD_DIM), lambda b, h, i: (b, n_heads + h)),
            pl.BlockSpec((seq, HEAD_DIM), lambda b, h, i: (b, 2 * n_heads + h)),
            pl.BlockSpec((1, 1, tq, 1), lambda b, h, i: (b, h, i, 0)),
            pl.BlockSpec((1, 1, nq, tq), lambda b, h, i: (b, h, 0, 0))],
        out_specs=pl.BlockSpec((tq, HEAD_DIM), lambda b, h, i: (b * nq + i, h)),
        out_shape=jax.ShapeDtypeStruct((t, d_attn), BF16),
        scratch_shapes=[pltpu.VMEM((tq, 1), F32), pltpu.VMEM((tq, 1), F32),
                        pltpu.VMEM((tq, HEAD_DIM), F32)],
        compiler_params=_cparams(("parallel", "parallel", "arbitrary"), 40),
    )(qkv, qkv, qkv, fcol, frow)


def _outproj_kernel(x_ref, attn_ref, pool_ref, wa_ref, wp_ref, gate_ref, o_ref):
    y = jnp.dot(attn_ref[...], wa_ref[...], preferred_element_type=F32)
    y = y + jnp.dot(pool_ref[...], wp_ref[...], preferred_element_type=F32)
    o_ref[...] = x_ref[...] + gate_ref[0] * y


def _outproj_call(x, attn, pool, w_attn, w_pool, gate, *, seq, tm=512):
    t, d = x.shape
    da = attn.shape[1]
    dp = pool.shape[1]
    tiles_per_seq = seq // tm
    return pl.pallas_call(
        _outproj_kernel,
        grid=(t // tm,),
        in_specs=[pl.BlockSpec((tm, d), lambda i: (i, 0)),
                  pl.BlockSpec((tm, da), lambda i: (i, 0)),
                  pl.BlockSpec((tm, dp), lambda i: (i, 0)),
                  pl.BlockSpec((da, d), lambda i: (0, 0)),
                  pl.BlockSpec((dp, d), lambda i: (0, 0)),
                  pl.BlockSpec((1, 1, d), lambda i: (i // tiles_per_seq, 0, 0))],
        out_specs=pl.BlockSpec((tm, d), lambda i: (i, 0)),
        out_shape=jax.ShapeDtypeStruct((t, d), F32),
        compiler_params=_cparams(("parallel",), 48),
    )(x, attn, pool, w_attn, w_pool, gate)


def kernel(x, c, w_ada, b_ada, ffn1_norm_g, ffn1_w_in, ffn1_w_out, mix_norm_g, w_in,
           b_forget, q_norm_g, k_norm_g, pool_w, pool_scale, w_out, ffn2_norm_g,
           ffn2_w_in, ffn2_w_out, final_norm_g):
    nb, seq, d = x.shape
    t = nb * seq
    n_heads = b_forget.shape[1]
    d_attn = n_heads * HEAD_DIM
    d_pool = pool_scale.shape[1]
    depth = w_ada.shape[0]
    xf = x.reshape(t, d)

    c_pad = jnp.pad(c, ((0, 8 - nb), (0, 0)))
    for l in range(depth):
        mod = _ada_call(c_pad, w_ada[l], b_ada[l].reshape(1, -1))[:nb]
        sh1, sc1, g1, sh2, sc2, g2, sh3, sc3, g3 = [
            mod[:, i * d:(i + 1) * d].reshape(nb, 1, d) for i in range(9)]

        w1i, w1o = _pack_ffn_weights(ffn1_w_in[l], ffn1_w_out[l])
        xf = _ffn_call(xf, ffn1_norm_g[l].reshape(1, d), sh1, sc1, g1, w1i, w1o, None,
                       seq=seq)

        wl = w_in[l]
        f0 = 3 * d_attn
        w_main = jnp.concatenate([wl[:, :f0], wl[:, f0 + n_heads:]], axis=1).astype(BF16)
        w_f = jnp.pad(wl[:, f0:f0 + n_heads], ((0, 0), (0, LANES - n_heads))).astype(BF16)
        q_gain = (q_norm_g[l] * (HEAD_DIM ** -0.5)).reshape(1, HEAD_DIM)
        k_gain = k_norm_g[l].reshape(1, HEAD_DIM)
        qkv, u, f_logit = _inproj_call(xf, mix_norm_g[l].reshape(1, d), sh2, sc2, w_main,
                                       w_f, q_gain, k_gain, seq=seq, d_attn=d_attn)
        b_pad = jnp.pad(b_forget[l], (0, LANES - n_heads)).reshape(1, LANES)
        fcum = _forget_call(f_logit.reshape(nb, seq, LANES), b_pad)
        fh = fcum[:, :, :n_heads].transpose(0, 2, 1)
        tq = 512
        attn = _attn_call(qkv, fh.reshape(nb, n_heads, seq, 1),
                          fh.reshape(nb, n_heads, seq // tq, tq),
                          nb=nb, seq=seq, n_heads=n_heads, tq=tq)
        pool = _pool_call(u.reshape(nb, seq, d_pool), pool_w[l], pool_scale[l].reshape(1, -1))
        wo = w_out[l].astype(BF16)
        xf = _outproj_call(xf, attn, pool.reshape(t, d_pool), wo[:d_attn], wo[d_attn:], g2,
                           seq=seq)

        w2i, w2o = _pack_ffn_weights(ffn2_w_in[l], ffn2_w_out[l])
        last = l == depth - 1
        xf = _ffn_call(xf, ffn2_norm_g[l].reshape(1, d), sh3, sc3, g3, w2i, w2o,
                       final_norm_g.reshape(1, d) if last else None, seq=seq)
    return xf.reshape(nb, seq, d)
```

```python
import functools

import jax
import jax.numpy as jnp
from jax import lax
from jax.experimental import pallas as pl
from jax.experimental.pallas import tpu as pltpu

F32 = jnp.float32
BF16 = jnp.bfloat16

EPS = 1e-6
HEAD_DIM = 128
POOL_WINDOWS = (2, 4, 8, 16)
LANES = 128
MXU_DIM = 256
MIB = 1024 * 1024


def _cparams(dims, vmem_mib):
    return pltpu.CompilerParams(dimension_semantics=dims,
                                vmem_limit_bytes=vmem_mib * MIB)


def _norm_mod(x, g, shift, scale):
    ms = jnp.mean(x * x, axis=-1, keepdims=True)
    xn = x * lax.rsqrt(ms + EPS) * g
    return xn * (1.0 + scale) + shift


def _ada_kernel(c_ref, w_ref, b_ref, o_ref):
    c = c_ref[...]
    ca = c * (1.0 / (1.0 + jnp.exp(-c)))
    o_ref[...] = jnp.dot(ca.astype(BF16), w_ref[...].astype(BF16),
                         preferred_element_type=F32) + b_ref[...]


def _ada_call(c_pad, w, b, tn=1024):
    m, d = c_pad.shape
    n = w.shape[1]
    return pl.pallas_call(
        _ada_kernel,
        grid=(n // tn,),
        in_specs=[pl.BlockSpec((m, d), lambda j: (0, 0)),
                  pl.BlockSpec((d, tn), lambda j: (0, j)),
                  pl.BlockSpec((1, tn), lambda j: (0, j))],
        out_specs=pl.BlockSpec((m, tn), lambda j: (0, j)),
        out_shape=jax.ShapeDtypeStruct((m, n), F32),
        compiler_params=_cparams(("arbitrary",), 40),
        name="ada_mod",
    )(c_pad, w, b)


def _ffn_kernel(x_ref, g_ref, sh_ref, sc_ref, gate_ref, win_ref, wout_ref,
                *rest, nj, tf, final_norm):
    if final_norm:
        fg_ref, o_ref, h_scr = rest
    else:
        o_ref, h_scr = rest
    j = pl.program_id(1)

    @pl.when(j == 0)
    def _():
        h = _norm_mod(x_ref[...], g_ref[...], sh_ref[0], sc_ref[0])
        h_scr[...] = h.astype(BF16)
        o_ref[...] = jnp.zeros_like(o_ref)

    hw = jnp.dot(h_scr[...], win_ref[...], preferred_element_type=F32)
    a = hw[:, :tf]
    b = hw[:, tf:]
    act = (a * (1.0 / (1.0 + jnp.exp(-a))) * b).astype(BF16)
    o_ref[...] += jnp.dot(act, wout_ref[...], preferred_element_type=F32)

    @pl.when(j == nj - 1)
    def _():
        y = x_ref[...] + 0.5 * gate_ref[0] * o_ref[...]
        if final_norm:
            ms = jnp.mean(y * y, axis=-1, keepdims=True)
            y = y * lax.rsqrt(ms + EPS) * fg_ref[...]
        o_ref[...] = y


def _ffn_call(x, g, shift, scale, gate, w_in_p, w_out_p, final_g, *, seq, tm=512,
              tf=MXU_DIM):
    t, d = x.shape
    nj = w_out_p.shape[0] // tf
    tiles_per_seq = seq // tm
    mod_spec = pl.BlockSpec((1, 1, d), lambda i, j: (i // tiles_per_seq, 0, 0))
    vec_spec = pl.BlockSpec((1, d), lambda i, j: (0, 0))
    in_specs = [pl.BlockSpec((tm, d), lambda i, j: (i, 0)),
                vec_spec, mod_spec, mod_spec, mod_spec,
                pl.BlockSpec((d, 2 * tf), lambda i, j: (0, j)),
                pl.BlockSpec((tf, d), lambda i, j: (j, 0))]
    args = [x, g, shift, scale, gate, w_in_p, w_out_p]
    if final_g is not None:
        in_specs.append(vec_spec)
        args.append(final_g)
    return pl.pallas_call(
        functools.partial(_ffn_kernel, nj=nj, tf=tf, final_norm=final_g is not None),
        grid=(t // tm, nj),
        in_specs=in_specs,
        out_specs=pl.BlockSpec((tm, d), lambda i, j: (i, 0)),
        out_shape=jax.ShapeDtypeStruct((t, d), F32),
        scratch_shapes=[pltpu.VMEM((tm, d), BF16)],
        compiler_params=_cparams(("parallel", "arbitrary"), 48),
        name="ffn_final" if final_g is not None else "ffn",
    )(*args)


def _pack_in_kernel(a0_ref, a1_ref, b0_ref, b1_ref, o_ref, *, n_full):
    keep = pl.program_id(0) < n_full
    o_ref[:, 0 * LANES:1 * LANES] = a0_ref[...].astype(BF16)
    o_ref[:, 1 * LANES:2 * LANES] = jnp.where(keep, a1_ref[...], 0.0).astype(BF16)
    o_ref[:, 2 * LANES:3 * LANES] = b0_ref[...].astype(BF16)
    o_ref[:, 3 * LANES:4 * LANES] = jnp.where(keep, b1_ref[...], 0.0).astype(BF16)


def _pack_out_kernel(r0_ref, r1_ref, o_ref, *, n_full):
    keep = pl.program_id(0) < n_full
    o_ref[0 * LANES:1 * LANES, :] = r0_ref[...].astype(BF16)
    o_ref[1 * LANES:2 * LANES, :] = jnp.where(keep, r1_ref[...], 0.0).astype(BF16)


def _pack_ffn_weights(w_in, w_out):
    d, two_ff = w_in.shape
    dff = two_ff // 2
    nblk = dff // LANES
    nj = -(-nblk // 2)
    n_full = nblk // 2
    second = lambda j: jnp.minimum(2 * j + 1, nblk - 1)
    col = lambda f: pl.BlockSpec((d, LANES), lambda j: (0, f(j)))
    w_in_p = pl.pallas_call(
        functools.partial(_pack_in_kernel, n_full=n_full),
        grid=(nj,),
        in_specs=[col(lambda j: 2 * j), col(second),
                  col(lambda j: nblk + 2 * j), col(lambda j: nblk + second(j))],
        out_specs=pl.BlockSpec((d, 4 * LANES), lambda j: (0, j)),
        out_shape=jax.ShapeDtypeStruct((d, nj * 4 * LANES), BF16),
        compiler_params=_cparams(("parallel",), 32),
        name="pack_w_in",
    )(w_in, w_in, w_in, w_in)
    row = lambda f: pl.BlockSpec((LANES, d), lambda j: (f(j), 0))
    w_out_p = pl.pallas_call(
        functools.partial(_pack_out_kernel, n_full=n_full),
        grid=(nj,),
        in_specs=[row(lambda j: 2 * j), row(second)],
        out_specs=pl.BlockSpec((2 * LANES, d), lambda j: (j, 0)),
        out_shape=jax.ShapeDtypeStruct((nj * 2 * LANES, d), BF16),
        compiler_params=_cparams(("parallel",), 32),
        name="pack_w_out",
    )(w_out, w_out)
    return w_in_p, w_out_p


def _inproj_kernel(x_ref, g_ref, sh_ref, sc_ref, w_ref, wf_ref, qg_ref, kg_ref,
                   qkv_ref, u_ref, f_ref, h_scr, *, tn, n_qk, n_qkv):
    j = pl.program_id(1)

    @pl.when(j == 0)
    def _():
        h = _norm_mod(x_ref[...], g_ref[...], sh_ref[0], sc_ref[0])
        h_scr[...] = h.astype(BF16)
        f_ref[...] = jnp.dot(h_scr[...], wf_ref[...], preferred_element_type=F32)

    acc = jnp.dot(h_scr[...], w_ref[...], preferred_element_type=F32)

    @pl.when(j < n_qk)
    def _():
        gain = jnp.where(j < n_qk // 2, qg_ref[...], kg_ref[...])
        for hh in range(tn // HEAD_DIM):
            sl = slice(hh * HEAD_DIM, (hh + 1) * HEAD_DIM)
            xh = acc[:, sl]
            ms = jnp.mean(xh * xh, axis=-1, keepdims=True)
            qkv_ref[:, sl] = (xh * lax.rsqrt(ms + EPS) * gain).astype(BF16)

    @pl.when(jnp.logical_and(j >= n_qk, j < n_qkv))
    def _():
        qkv_ref[...] = acc.astype(BF16)

    @pl.when(j >= n_qkv)
    def _():
        u_ref[...] = acc


def _inproj_call(x, g, shift, scale, w_main, w_f, q_gain, k_gain, *, seq, d_attn,
                 tm=512, tn=512):
    t, d = x.shape
    n_main = w_main.shape[1]
    nj = n_main // tn
    n_qk = 2 * d_attn // tn
    n_qkv = 3 * d_attn // tn
    d_pool = n_main - 3 * d_attn
    tiles_per_seq = seq // tm
    mod_spec = pl.BlockSpec((1, 1, d), lambda i, j: (i // tiles_per_seq, 0, 0))
    return pl.pallas_call(
        functools.partial(_inproj_kernel, tn=tn, n_qk=n_qk, n_qkv=n_qkv),
        grid=(t // tm, nj),
        in_specs=[pl.BlockSpec((tm, d), lambda i, j: (i, 0)),
                  pl.BlockSpec((1, d), lambda i, j: (0, 0)),
                  mod_spec, mod_spec,
                  pl.BlockSpec((d, tn), lambda i, j: (0, j)),
                  pl.BlockSpec((d, LANES), lambda i, j: (0, 0)),
                  pl.BlockSpec((1, HEAD_DIM), lambda i, j: (0, 0)),
                  pl.BlockSpec((1, HEAD_DIM), lambda i, j: (0, 0))],
        out_specs=[
            pl.BlockSpec((tm, tn), lambda i, j: (i, jnp.minimum(j, n_qkv - 1))),
            pl.BlockSpec((tm, tn), lambda i, j: (i, jnp.maximum(j - n_qkv, 0))),
            pl.BlockSpec((tm, LANES), lambda i, j: (i, 0))],
        out_shape=[jax.ShapeDtypeStruct((t, 3 * d_attn), BF16),
                   jax.ShapeDtypeStruct((t, d_pool), F32),
                   jax.ShapeDtypeStruct((t, LANES), F32)],
        scratch_shapes=[pltpu.VMEM((tm, d), BF16)],
        compiler_params=_cparams(("parallel", "arbitrary"), 40),
        name="mixer_inproj",
    )(x, g, shift, scale, w_main, w_f, q_gain, k_gain)


def _forget_kernel(f_ref, b_ref, o_ref, pad_scr, *, seq):
    z = f_ref[0] + b_ref[...]
    x = jnp.minimum(z, 0.0) - jnp.log1p(jnp.exp(-jnp.abs(z)))
    pad_scr[pl.ds(0, seq), :] = jnp.zeros((seq, LANES), F32)
    d = 1
    while d < seq:
        pad_scr[pl.ds(seq, seq), :] = x
        x = x + pad_scr[pl.ds(seq - d, seq), :]
        d *= 2
    o_ref[0] = x


def _forget_call(f_logit, b_pad):
    nb, seq, _ = f_logit.shape
    return pl.pallas_call(
        functools.partial(_forget_kernel, seq=seq),
        grid=(nb,),
        in_specs=[pl.BlockSpec((1, seq, LANES), lambda b: (b, 0, 0)),
                  pl.BlockSpec((1, LANES), lambda b: (0, 0))],
        out_specs=pl.BlockSpec((1, seq, LANES), lambda b: (b, 0, 0)),
        out_shape=jax.ShapeDtypeStruct((nb, seq, LANES), F32),
        scratch_shapes=[pltpu.VMEM((2 * seq, LANES), F32)],
        compiler_params=_cparams(("parallel",), 40),
        name="forget_cumsum",
    )(f_logit, b_pad)


POOL_HALO = 16
POOL_ROWS = 512


def _pool_kernel(u_ref, w_ref, s_ref, o_ref, pad_scr, *, seq):
    g = pl.program_id(1)
    cg = u_ref.shape[2]
    pad_scr[pl.ds(0, POOL_HALO), :] = jnp.zeros((POOL_HALO, cg), F32)
    pad_scr[pl.ds(POOL_HALO, seq), :] = u_ref[0]
    w = w_ref[0].astype(BF16)
    scale = s_ref[...]
    for gi, win in enumerate(POOL_WINDOWS):
        @pl.when(g == gi)
        def _(win=win):
            for r0 in range(0, seq, POOL_ROWS):
                tok = pad_scr[pl.ds(POOL_HALO + r0, POOL_ROWS), :]
                tot = tok
                for dd in range(1, win):
                    tot = tot + pad_scr[pl.ds(POOL_HALO + r0 - dd, POOL_ROWS), :]
                pos = r0 + lax.broadcasted_iota(jnp.int32, (POOL_ROWS, cg), 0)
                cnt = jnp.minimum(pos + 1, win).astype(F32)
                p = tot / cnt - tok
                y = jnp.dot(p.astype(BF16), w, preferred_element_type=F32) * scale
                o_ref[0, pl.ds(r0, POOL_ROWS), :] = y.astype(BF16)


def _pool_call(u, pool_w, pool_scale):
    nb, seq, d_pool = u.shape
    ng, cg, _ = pool_w.shape
    return pl.pallas_call(
        functools.partial(_pool_kernel, seq=seq),
        grid=(nb, ng),
        in_specs=[pl.BlockSpec((1, seq, cg), lambda b, g: (b, 0, g)),
                  pl.BlockSpec((1, cg, cg), lambda b, g: (g, 0, 0)),
                  pl.BlockSpec((1, cg), lambda b, g: (0, g))],
        out_specs=pl.BlockSpec((1, seq, cg), lambda b, g: (b, 0, g)),
        out_shape=jax.ShapeDtypeStruct((nb, seq, d_pool), BF16),
        scratch_shapes=[pltpu.VMEM((POOL_HALO + seq, cg), F32)],
        compiler_params=_cparams(("parallel", "arbitrary"), 40),
        name="ms_pool",
    )(u, pool_w, pool_scale)


NEG_BIG = -1e30


def _lane_tile_reduce(x, op):
    out = x[:, :LANES]
    for t in range(1, x.shape[1] // LANES):
        out = op(out, x[:, t * LANES:(t + 1) * LANES])
    return out


def _attn_kernel(q_ref, k_ref, v_ref, fc_ref, fr_ref, o_ref, s_scr, p_scr, *, tq):
    seq = q_ref.shape[0]
    for i in range(seq // tq):
        slot = i % 2
        q = q_ref[pl.ds(i * tq, tq), :]
        fcol = fc_ref[0, 0, pl.ds(i * tq, tq), :]
        n_chunks = i + 1
        m_tile = None
        for c in range(n_chunks):
            k = k_ref[pl.ds(c * tq, tq), :]
            frow = fr_ref[0, 0, :, pl.ds(c * tq, tq)]
            s = lax.dot_general(q, k, (((1,), (1,)), ((), ())),
                                preferred_element_type=F32)
            s = s + (fcol - frow)
            if c == i:
                row = lax.broadcasted_iota(jnp.int32, (tq, tq), 0)
                col = lax.broadcasted_iota(jnp.int32, (tq, tq), 1)
                s = jnp.where(row >= col, s, NEG_BIG)
            s_scr[slot, :, pl.ds(c * tq, tq)] = s
            cm = _lane_tile_reduce(s, jnp.maximum)
            m_tile = cm if m_tile is None else jnp.maximum(m_tile, cm)
        m_row = jnp.max(m_tile, axis=-1, keepdims=True)
        l_tile = None
        for c in range(n_chunks):
            p = jnp.exp(s_scr[slot, :, pl.ds(c * tq, tq)] - m_row)
            p_scr[slot, :, pl.ds(c * tq, tq)] = p.astype(BF16)
            cl = _lane_tile_reduce(p, jnp.add)
            l_tile = cl if l_tile is None else l_tile + cl
        l_row = jnp.sum(l_tile, axis=-1, keepdims=True)
        kend = n_chunks * tq
        acc = jnp.dot(p_scr[slot, :, pl.ds(0, kend)], v_ref[pl.ds(0, kend), :],
                      preferred_element_type=F32)
        o_ref[pl.ds(i * tq, tq), :] = (acc / l_row).astype(BF16)


def _attn_call(qkv, fcol, frow, *, nb, seq, n_heads, tq=512):
    t = qkv.shape[0]
    d_attn = n_heads * HEAD_DIM
    head_spec = lambda off: pl.BlockSpec((seq, HEAD_DIM), lambda b, h: (b, off + h))
    return pl.pallas_call(
        functools.partial(_attn_kernel, tq=tq),
        grid=(nb, n_heads),
        in_specs=[head_spec(0), head_spec(n_heads), head_spec(2 * n_heads),
                  pl.BlockSpec((1, 1, seq, 1), lambda b, h: (b, h, 0, 0)),
                  pl.BlockSpec((1, 1, 1, seq), lambda b, h: (b, h, 0, 0))],
        out_specs=head_spec(0),
        out_shape=jax.ShapeDtypeStruct((t, d_attn), BF16),
        scratch_shapes=[pltpu.VMEM((2, tq, seq), F32), pltpu.VMEM((2, tq, seq), BF16)],
        compiler_params=_cparams(("parallel", "parallel"), 56),
        name="fox_attention",
    )(qkv, qkv, qkv, fcol, frow)


def _outproj_kernel(x_ref, attn_ref, pool_ref, wa_ref, wp_ref, gate_ref, o_ref):
    y = jnp.dot(attn_ref[...], wa_ref[...], preferred_element_type=F32)
    y = y + jnp.dot(pool_ref[...], wp_ref[...], preferred_element_type=F32)
    o_ref[...] = x_ref[...] + gate_ref[0] * y


def _outproj_call(x, attn, pool, w_attn, w_pool, gate, *, seq, tm=512):
    t, d = x.shape
    da = attn.shape[1]
    dp = pool.shape[1]
    tiles_per_seq = seq // tm
    return pl.pallas_call(
        _outproj_kernel,
        grid=(t // tm,),
        in_specs=[pl.BlockSpec((tm, d), lambda i: (i, 0)),
                  pl.BlockSpec((tm, da), lambda i: (i, 0)),
                  pl.BlockSpec((tm, dp), lambda i: (i, 0)),
                  pl.BlockSpec((da, d), lambda i: (0, 0)),
                  pl.BlockSpec((dp, d), lambda i: (0, 0)),
                  pl.BlockSpec((1, 1, d), lambda i: (i // tiles_per_seq, 0, 0))],
        out_specs=pl.BlockSpec((tm, d), lambda i: (i, 0)),
        out_shape=jax.ShapeDtypeStruct((t, d), F32),
        compiler_params=_cparams(("parallel",), 48),
        name="mixer_outproj",
    )(x, attn, pool, w_attn, w_pool, gate)


def kernel(x, c, w_ada, b_ada, ffn1_norm_g, ffn1_w_in, ffn1_w_out, mix_norm_g, w_in,
           b_forget, q_norm_g, k_norm_g, pool_w, pool_scale, w_out, ffn2_norm_g,
           ffn2_w_in, ffn2_w_out, final_norm_g):
    nb, seq, d = x.shape
    t = nb * seq
    n_heads = b_forget.shape[1]
    d_attn = n_heads * HEAD_DIM
    d_pool = pool_scale.shape[1]
    depth = w_ada.shape[0]
    xf = x.reshape(t, d)

    c_pad = jnp.pad(c, ((0, 8 - nb), (0, 0)))
    for l in range(depth):
        mod = _ada_call(c_pad, w_ada[l], b_ada[l].reshape(1, -1))[:nb]
        sh1, sc1, g1, sh2, sc2, g2, sh3, sc3, g3 = [
            mod[:, i * d:(i + 1) * d].reshape(nb, 1, d) for i in range(9)]

        w1i, w1o = _pack_ffn_weights(ffn1_w_in[l], ffn1_w_out[l])
        xf = _ffn_call(xf, ffn1_norm_g[l].reshape(1, d), sh1, sc1, g1, w1i, w1o, None,
                       seq=seq)

        wl = w_in[l]
        f0 = 3 * d_attn
        w_main = jnp.concatenate([wl[:, :f0], wl[:, f0 + n_heads:]], axis=1).astype(BF16)
        w_f = jnp.pad(wl[:, f0:f0 + n_heads], ((0, 0), (0, LANES - n_heads))).astype(BF16)
        q_gain = (q_norm_g[l] * (HEAD_DIM ** -0.5)).reshape(1, HEAD_DIM)
        k_gain = k_norm_g[l].reshape(1, HEAD_DIM)
        qkv, u, f_logit = _inproj_call(xf, mix_norm_g[l].reshape(1, d), sh2, sc2, w_main,
                                       w_f, q_gain, k_gain, seq=seq, d_attn=d_attn)
        b_pad = jnp.pad(b_forget[l], (0, LANES - n_heads)).reshape(1, LANES)
        fcum = _forget_call(f_logit.reshape(nb, seq, LANES), b_pad)
        fh = fcum[:, :, :n_heads].transpose(0, 2, 1)
        attn = _attn_call(qkv, fh.reshape(nb, n_heads, seq, 1),
                          fh.reshape(nb, n_heads, 1, seq),
                          nb=nb, seq=seq, n_heads=n_heads)
        pool = _pool_call(u.reshape(nb, seq, d_pool), pool_w[l], pool_scale[l].reshape(1, -1))
        wo = w_out[l].astype(BF16)
        xf = _outproj_call(xf, attn, pool.reshape(t, d_pool), wo[:d_attn], wo[d_attn:], g2,
                           seq=seq)

        w2i, w2o = _pack_ffn_weights(ffn2_w_in[l], ffn2_w_out[l])
        last = l == depth - 1
        xf = _ffn_call(xf, ffn2_norm_g[l].reshape(1, d), sh3, sc3, g3, w2i, w2o,
                       final_norm_g.reshape(1, d) if last else None, seq=seq)
    return xf.reshape(nb, seq, d)
```

```python
import functools

import jax
import jax.numpy as jnp
from jax import lax
from jax.experimental import pallas as pl
from jax.experimental.pallas import tpu as pltpu

F32 = jnp.float32
BF16 = jnp.bfloat16

EPS = 1e-6
HEAD_DIM = 128
POOL_WINDOWS = (2, 4, 8, 16)
LANES = 128
MXU_DIM = 256
MIB = 1024 * 1024
LOG2E = 1.4426950408889634


def _cparams(dims, vmem_mib):
    return pltpu.CompilerParams(dimension_semantics=dims,
                                vmem_limit_bytes=vmem_mib * MIB)


def _norm_mod(x, g, shift, scale):
    ms = jnp.mean(x * x, axis=-1, keepdims=True)
    xn = x * lax.rsqrt(ms + EPS) * g
    return xn * (1.0 + scale) + shift


def _ada_kernel(c_ref, w_ref, b_ref, o_ref):
    c = c_ref[...]
    ca = c * (1.0 / (1.0 + jnp.exp(-c)))
    o_ref[...] = jnp.dot(ca.astype(BF16), w_ref[...].astype(BF16),
                         preferred_element_type=F32) + b_ref[...]


def _ada_call(c_pad, w, b, tn=1024):
    m, d = c_pad.shape
    n = w.shape[1]
    return pl.pallas_call(
        _ada_kernel,
        grid=(n // tn,),
        in_specs=[pl.BlockSpec((m, d), lambda j: (0, 0)),
                  pl.BlockSpec((d, tn), lambda j: (0, j)),
                  pl.BlockSpec((1, tn), lambda j: (0, j))],
        out_specs=pl.BlockSpec((m, tn), lambda j: (0, j)),
        out_shape=jax.ShapeDtypeStruct((m, n), F32),
        compiler_params=_cparams(("arbitrary",), 40),
        name="ada_mod",
    )(c_pad, w, b)


def _ffn_kernel(x_ref, g_ref, sh_ref, sc_ref, gate_ref, win_ref, wout_ref,
                *rest, nj, tf, final_norm):
    if final_norm:
        fg_ref, o_ref, h_scr = rest
    else:
        o_ref, h_scr = rest
    j = pl.program_id(1)

    @pl.when(j == 0)
    def _():
        h = _norm_mod(x_ref[...], g_ref[...], sh_ref[0], sc_ref[0])
        h_scr[...] = h.astype(BF16)
        o_ref[...] = jnp.zeros_like(o_ref)

    hw = jnp.dot(h_scr[...], win_ref[...], preferred_element_type=F32)
    cps = win_ref.shape[1] // (2 * tf)
    a = jnp.concatenate([hw[:, 2 * c * tf:(2 * c + 1) * tf] for c in range(cps)], axis=1)
    b = jnp.concatenate([hw[:, (2 * c + 1) * tf:(2 * c + 2) * tf] for c in range(cps)],
                        axis=1)
    act = (a * (1.0 / (1.0 + jnp.exp(-a))) * b).astype(BF16)
    o_ref[...] += jnp.dot(act, wout_ref[...], preferred_element_type=F32)

    @pl.when(j == nj - 1)
    def _():
        y = x_ref[...] + 0.5 * gate_ref[0] * o_ref[...]
        if final_norm:
            ms = jnp.mean(y * y, axis=-1, keepdims=True)
            y = y * lax.rsqrt(ms + EPS) * fg_ref[...]
        o_ref[...] = y


def _ffn_call(x, g, shift, scale, gate, w_in_p, w_out_p, final_g, *, seq, tm=512,
              tf=MXU_DIM, cps=2):
    t, d = x.shape
    nj = w_out_p.shape[0] // (cps * tf)
    tiles_per_seq = seq // tm
    mod_spec = pl.BlockSpec((1, 1, d), lambda i, j: (i // tiles_per_seq, 0, 0))
    vec_spec = pl.BlockSpec((1, d), lambda i, j: (0, 0))
    in_specs = [pl.BlockSpec((tm, d), lambda i, j: (i, 0)),
                vec_spec, mod_spec, mod_spec, mod_spec,
                pl.BlockSpec((d, cps * 2 * tf), lambda i, j: (0, j)),
                pl.BlockSpec((cps * tf, d), lambda i, j: (j, 0))]
    args = [x, g, shift, scale, gate, w_in_p, w_out_p]
    if final_g is not None:
        in_specs.append(vec_spec)
        args.append(final_g)
    return pl.pallas_call(
        functools.partial(_ffn_kernel, nj=nj, tf=tf, final_norm=final_g is not None),
        grid=(t // tm, nj),
        in_specs=in_specs,
        out_specs=pl.BlockSpec((tm, d), lambda i, j: (i, 0)),
        out_shape=jax.ShapeDtypeStruct((t, d), F32),
        scratch_shapes=[pltpu.VMEM((tm, d), BF16)],
        compiler_params=_cparams(("parallel", "arbitrary"), 48),
        name="ffn_final" if final_g is not None else "ffn",
    )(*args)


def _pack_in_kernel(a0_ref, a1_ref, b0_ref, b1_ref, o_ref, *, n_full):
    keep = pl.program_id(0) < n_full
    o_ref[:, 0 * LANES:1 * LANES] = a0_ref[...].astype(BF16)
    o_ref[:, 1 * LANES:2 * LANES] = jnp.where(keep, a1_ref[...], 0.0).astype(BF16)
    o_ref[:, 2 * LANES:3 * LANES] = b0_ref[...].astype(BF16)
    o_ref[:, 3 * LANES:4 * LANES] = jnp.where(keep, b1_ref[...], 0.0).astype(BF16)


def _pack_out_kernel(r0_ref, r1_ref, o_ref, *, n_full):
    keep = pl.program_id(0) < n_full
    o_ref[0 * LANES:1 * LANES, :] = r0_ref[...].astype(BF16)
    o_ref[1 * LANES:2 * LANES, :] = jnp.where(keep, r1_ref[...], 0.0).astype(BF16)


def _pack_ffn_weights(w_in, w_out):
    d, two_ff = w_in.shape
    dff = two_ff // 2
    nblk = dff // LANES
    nj = -(-nblk // 2)
    n_full = nblk // 2
    second = lambda j: jnp.minimum(2 * j + 1, nblk - 1)
    col = lambda f: pl.BlockSpec((d, LANES), lambda j: (0, f(j)))
    w_in_p = pl.pallas_call(
        functools.partial(_pack_in_kernel, n_full=n_full),
        grid=(nj,),
        in_specs=[col(lambda j: 2 * j), col(second),
                  col(lambda j: nblk + 2 * j), col(lambda j: nblk + second(j))],
        out_specs=pl.BlockSpec((d, 4 * LANES), lambda j: (0, j)),
        out_shape=jax.ShapeDtypeStruct((d, nj * 4 * LANES), BF16),
        compiler_params=_cparams(("parallel",), 32),
        name="pack_w_in",
    )(w_in, w_in, w_in, w_in)
    row = lambda f: pl.BlockSpec((LANES, d), lambda j: (f(j), 0))
    w_out_p = pl.pallas_call(
        functools.partial(_pack_out_kernel, n_full=n_full),
        grid=(nj,),
        in_specs=[row(lambda j: 2 * j), row(second)],
        out_specs=pl.BlockSpec((2 * LANES, d), lambda j: (j, 0)),
        out_shape=jax.ShapeDtypeStruct((nj * 2 * LANES, d), BF16),
        compiler_params=_cparams(("parallel",), 32),
        name="pack_w_out",
    )(w_out, w_out)
    return w_in_p, w_out_p


def _inproj_kernel(x_ref, g_ref, sh_ref, sc_ref, w_ref, wf_ref, qg_ref, kg_ref,
                   qkv_ref, u_ref, f_ref, h_scr, *, tn, n_qk, n_qkv):
    j = pl.program_id(1)

    @pl.when(j == 0)
    def _():
        h = _norm_mod(x_ref[...], g_ref[...], sh_ref[0], sc_ref[0])
        h_scr[...] = h.astype(BF16)
        f_ref[...] = jnp.dot(h_scr[...], wf_ref[...], preferred_element_type=F32)

    acc = jnp.dot(h_scr[...], w_ref[...], preferred_element_type=F32)

    @pl.when(j < n_qk)
    def _():
        gain = jnp.where(j < n_qk // 2, qg_ref[...], kg_ref[...])
        for hh in range(tn // HEAD_DIM):
            sl = slice(hh * HEAD_DIM, (hh + 1) * HEAD_DIM)
            xh = acc[:, sl]
            ms = jnp.mean(xh * xh, axis=-1, keepdims=True)
            qkv_ref[:, sl] = (xh * lax.rsqrt(ms + EPS) * gain).astype(BF16)

    @pl.when(jnp.logical_and(j >= n_qk, j < n_qkv))
    def _():
        qkv_ref[...] = acc.astype(BF16)

    @pl.when(j >= n_qkv)
    def _():
        u_ref[...] = acc


def _inproj_call(x, g, shift, scale, w_main, w_f, q_gain, k_gain, *, seq, d_attn,
                 tm=1024, tn=512):
    t, d = x.shape
    n_main = w_main.shape[1]
    nj = n_main // tn
    n_qk = 2 * d_attn // tn
    n_qkv = 3 * d_attn // tn
    d_pool = n_main - 3 * d_attn
    tiles_per_seq = seq // tm
    mod_spec = pl.BlockSpec((1, 1, d), lambda i, j: (i // tiles_per_seq, 0, 0))
    return pl.pallas_call(
        functools.partial(_inproj_kernel, tn=tn, n_qk=n_qk, n_qkv=n_qkv),
        grid=(t // tm, nj),
        in_specs=[pl.BlockSpec((tm, d), lambda i, j: (i, 0)),
                  pl.BlockSpec((1, d), lambda i, j: (0, 0)),
                  mod_spec, mod_spec,
                  pl.BlockSpec((d, tn), lambda i, j: (0, j)),
                  pl.BlockSpec((d, LANES), lambda i, j: (0, 0)),
                  pl.BlockSpec((1, HEAD_DIM), lambda i, j: (0, 0)),
                  pl.BlockSpec((1, HEAD_DIM), lambda i, j: (0, 0))],
        out_specs=[
            pl.BlockSpec((tm, tn), lambda i, j: (i, jnp.minimum(j, n_qkv - 1))),
            pl.BlockSpec((tm, tn), lambda i, j: (i, jnp.maximum(j - n_qkv, 0))),
            pl.BlockSpec((tm, LANES), lambda i, j: (i, 0))],
        out_shape=[jax.ShapeDtypeStruct((t, 3 * d_attn), BF16),
                   jax.ShapeDtypeStruct((t, d_pool), F32),
                   jax.ShapeDtypeStruct((t, LANES), F32)],
        scratch_shapes=[pltpu.VMEM((tm, d), BF16)],
        compiler_params=_cparams(("parallel", "arbitrary"), 52),
        name="mixer_inproj",
    )(x, g, shift, scale, w_main, w_f, q_gain, k_gain)


def _forget_kernel(f_ref, b_ref, o_ref, pad_scr, *, seq):
    z = f_ref[0] + b_ref[...]
    x = jnp.minimum(z, 0.0) - jnp.log1p(jnp.exp(-jnp.abs(z)))
    pad_scr[pl.ds(0, seq), :] = jnp.zeros((seq, LANES), F32)
    d = 1
    while d < seq:
        pad_scr[pl.ds(seq, seq), :] = x
        x = x + pad_scr[pl.ds(seq - d, seq), :]
        d *= 2
    o_ref[0] = x * LOG2E


def _forget_call(f_logit, b_pad):
    nb, seq, _ = f_logit.shape
    return pl.pallas_call(
        functools.partial(_forget_kernel, seq=seq),
        grid=(nb,),
        in_specs=[pl.BlockSpec((1, seq, LANES), lambda b: (b, 0, 0)),
                  pl.BlockSpec((1, LANES), lambda b: (0, 0))],
        out_specs=pl.BlockSpec((1, seq, LANES), lambda b: (b, 0, 0)),
        out_shape=jax.ShapeDtypeStruct((nb, seq, LANES), F32),
        scratch_shapes=[pltpu.VMEM((2 * seq, LANES), F32)],
        compiler_params=_cparams(("parallel",), 40),
        name="forget_cumsum",
    )(f_logit, b_pad)


POOL_HALO = 16
POOL_ROWS = 512


def _pool_kernel(u_ref, w_ref, s_ref, o_ref, pad_scr, *, seq):
    g = pl.program_id(1)
    cg = u_ref.shape[2]
    pad_scr[pl.ds(0, POOL_HALO), :] = jnp.zeros((POOL_HALO, cg), F32)
    pad_scr[pl.ds(POOL_HALO, seq), :] = u_ref[0]
    w = w_ref[0].astype(BF16)
    scale = s_ref[...]
    for gi, win in enumerate(POOL_WINDOWS):
        @pl.when(g == gi)
        def _(win=win):
            for r0 in range(0, seq, POOL_ROWS):
                tok = pad_scr[pl.ds(POOL_HALO + r0, POOL_ROWS), :]
                tot = tok
                for dd in range(1, win):
                    tot = tot + pad_scr[pl.ds(POOL_HALO + r0 - dd, POOL_ROWS), :]
                pos = r0 + lax.broadcasted_iota(jnp.int32, (POOL_ROWS, cg), 0)
                cnt = jnp.minimum(pos + 1, win).astype(F32)
                p = tot / cnt - tok
                y = jnp.dot(p.astype(BF16), w, preferred_element_type=F32) * scale
                o_ref[0, pl.ds(r0, POOL_ROWS), :] = y.astype(BF16)


def _pool_call(u, pool_w, pool_scale):
    nb, seq, d_pool = u.shape
    ng, cg, _ = pool_w.shape
    return pl.pallas_call(
        functools.partial(_pool_kernel, seq=seq),
        grid=(nb, ng),
        in_specs=[pl.BlockSpec((1, seq, cg), lambda b, g: (b, 0, g)),
                  pl.BlockSpec((1, cg, cg), lambda b, g: (g, 0, 0)),
                  pl.BlockSpec((1, cg), lambda b, g: (0, g))],
        out_specs=pl.BlockSpec((1, seq, cg), lambda b, g: (b, 0, g)),
        out_shape=jax.ShapeDtypeStruct((nb, seq, d_pool), BF16),
        scratch_shapes=[pltpu.VMEM((POOL_HALO + seq, cg), F32)],
        compiler_params=_cparams(("parallel", "arbitrary"), 40),
        name="ms_pool",
    )(u, pool_w, pool_scale)


NEG_BIG = -1e30


def _lane_tile_reduce(x, op):
    out = x[:, :LANES]
    for t in range(1, x.shape[1] // LANES):
        out = op(out, x[:, t * LANES:(t + 1) * LANES])
    return out


def _attn_kernel(q_ref, k_ref, v_ref, fc_ref, fr_ref, o_ref, s_scr, p_scr, *, tq):
    seq = q_ref.shape[0]
    for i in range(seq // tq):
        slot = i % 2
        q = q_ref[pl.ds(i * tq, tq), :]
        fcol = fc_ref[0, 0, pl.ds(i * tq, tq), :]
        n_chunks = i + 1
        m_tile = None
        for c in range(n_chunks):
            k = k_ref[pl.ds(c * tq, tq), :]
            frow = fr_ref[0, 0, :, pl.ds(c * tq, tq)]
            s = lax.dot_general(q, k, (((1,), (1,)), ((), ())),
                                preferred_element_type=F32)
            s = s + (fcol - frow)
            if c == i:
                row = lax.broadcasted_iota(jnp.int32, (tq, tq), 0)
                col = lax.broadcasted_iota(jnp.int32, (tq, tq), 1)
                s = jnp.where(row >= col, s, NEG_BIG)
            s_scr[slot, :, pl.ds(c * tq, tq)] = s
            cm = _lane_tile_reduce(s, jnp.maximum)
            m_tile = cm if m_tile is None else jnp.maximum(m_tile, cm)
        m_row = jnp.max(m_tile, axis=-1, keepdims=True)
        l_tile = None
        for c in range(n_chunks):
            p = jnp.exp2(s_scr[slot, :, pl.ds(c * tq, tq)] - m_row)
            p_scr[slot, :, pl.ds(c * tq, tq)] = p.astype(BF16)
            cl = _lane_tile_reduce(p, jnp.add)
            l_tile = cl if l_tile is None else l_tile + cl
        l_row = jnp.sum(l_tile, axis=-1, keepdims=True)
        kend = n_chunks * tq
        acc = jnp.dot(p_scr[slot, :, pl.ds(0, kend)], v_ref[pl.ds(0, kend), :],
                      preferred_element_type=F32)
        o_ref[pl.ds(i * tq, tq), :] = (acc / l_row).astype(BF16)


def _attn_call(qkv, fcol, frow, *, nb, seq, n_heads, tq=512):
    t = qkv.shape[0]
    d_attn = n_heads * HEAD_DIM
    head_spec = lambda off: pl.BlockSpec((seq, HEAD_DIM), lambda b, h: (b, off + h))
    return pl.pallas_call(
        functools.partial(_attn_kernel, tq=tq),
        grid=(nb, n_heads),
        in_specs=[head_spec(0), head_spec(n_heads), head_spec(2 * n_heads),
                  pl.BlockSpec((1, 1, seq, 1), lambda b, h: (b, h, 0, 0)),
                  pl.BlockSpec((1, 1, 1, seq), lambda b, h: (b, h, 0, 0))],
        out_specs=head_spec(0),
        out_shape=jax.ShapeDtypeStruct((t, d_attn), BF16),
        scratch_shapes=[pltpu.VMEM((2, tq, seq), F32), pltpu.VMEM((2, tq, seq), BF16)],
        compiler_params=_cparams(("parallel", "parallel"), 56),
        name="fox_attention",
    )(qkv, qkv, qkv, fcol, frow)


def _outproj_kernel(x_ref, attn_ref, pool_ref, wa_ref, wp_ref, gate_ref, o_ref):
    y = jnp.dot(attn_ref[...], wa_ref[...], preferred_element_type=F32)
    y = y + jnp.dot(pool_ref[...], wp_ref[...], preferred_element_type=F32)
    o_ref[...] = x_ref[...] + gate_ref[0] * y


def _outproj_call(x, attn, pool, w_attn, w_pool, gate, *, seq, tm=512):
    t, d = x.shape
    da = attn.shape[1]
    dp = pool.shape[1]
    tiles_per_seq = seq // tm
    return pl.pallas_call(
        _outproj_kernel,
        grid=(t // tm,),
        in_specs=[pl.BlockSpec((tm, d), lambda i: (i, 0)),
                  pl.BlockSpec((tm, da), lambda i: (i, 0)),
                  pl.BlockSpec((tm, dp), lambda i: (i, 0)),
                  pl.BlockSpec((da, d), lambda i: (0, 0)),
                  pl.BlockSpec((dp, d), lambda i: (0, 0)),
                  pl.BlockSpec((1, 1, d), lambda i: (i // tiles_per_seq, 0, 0))],
        out_specs=pl.BlockSpec((tm, d), lambda i: (i, 0)),
        out_shape=jax.ShapeDtypeStruct((t, d), F32),
        compiler_params=_cparams(("parallel",), 48),
        name="mixer_outproj",
    )(x, attn, pool, w_attn, w_pool, gate)


def kernel(x, c, w_ada, b_ada, ffn1_norm_g, ffn1_w_in, ffn1_w_out, mix_norm_g, w_in,
           b_forget, q_norm_g, k_norm_g, pool_w, pool_scale, w_out, ffn2_norm_g,
           ffn2_w_in, ffn2_w_out, final_norm_g):
    nb, seq, d = x.shape
    t = nb * seq
    n_heads = b_forget.shape[1]
    d_attn = n_heads * HEAD_DIM
    d_pool = pool_scale.shape[1]
    depth = w_ada.shape[0]
    xf = x.reshape(t, d)

    c_pad = jnp.pad(c, ((0, 8 - nb), (0, 0)))
    for l in range(depth):
        mod = _ada_call(c_pad, w_ada[l], b_ada[l].reshape(1, -1))[:nb]
        sh1, sc1, g1, sh2, sc2, g2, sh3, sc3, g3 = [
            mod[:, i * d:(i + 1) * d].reshape(nb, 1, d) for i in range(9)]

        w1i, w1o = _pack_ffn_weights(ffn1_w_in[l], ffn1_w_out[l])
        xf = _ffn_call(xf, ffn1_norm_g[l].reshape(1, d), sh1, sc1, g1, w1i, w1o, None,
                       seq=seq)

        wl = w_in[l]
        f0 = 3 * d_attn
        w_main = jnp.concatenate([wl[:, :f0], wl[:, f0 + n_heads:]], axis=1).astype(BF16)
        w_f = jnp.pad(wl[:, f0:f0 + n_heads], ((0, 0), (0, LANES - n_heads))).astype(BF16)
        q_gain = (q_norm_g[l] * (HEAD_DIM ** -0.5 * LOG2E)).reshape(1, HEAD_DIM)
        k_gain = k_norm_g[l].reshape(1, HEAD_DIM)
        qkv, u, f_logit = _inproj_call(xf, mix_norm_g[l].reshape(1, d), sh2, sc2, w_main,
                                       w_f, q_gain, k_gain, seq=seq, d_attn=d_attn)
        b_pad = jnp.pad(b_forget[l], (0, LANES - n_heads)).reshape(1, LANES)
        fcum = _forget_call(f_logit.reshape(nb, seq, LANES), b_pad)
        fh = fcum[:, :, :n_heads].transpose(0, 2, 1)
        attn = _attn_call(qkv, fh.reshape(nb, n_heads, seq, 1),
                          fh.reshape(nb, n_heads, 1, seq),
                          nb=nb, seq=seq, n_heads=n_heads)
        pool = _pool_call(u.reshape(nb, seq, d_pool), pool_w[l], pool_scale[l].reshape(1, -1))
        wo = w_out[l].astype(BF16)
        xf = _outproj_call(xf, attn, pool.reshape(t, d_pool), wo[:d_attn], wo[d_attn:], g2,
                           seq=seq)

        w2i, w2o = _pack_ffn_weights(ffn2_w_in[l], ffn2_w_out[l])
        last = l == depth - 1
        xf = _ffn_call(xf, ffn2_norm_g[l].reshape(1, d), sh3, sc3, g3, w2i, w2o,
                       final_norm_g.reshape(1, d) if last else None, seq=seq)
    return xf.reshape(nb, seq, d)
```

```python
import functools

import jax
import jax.numpy as jnp
from jax import lax
from jax.experimental import pallas as pl
from jax.experimental.pallas import tpu as pltpu

F32 = jnp.float32
BF16 = jnp.bfloat16

EPS = 1e-6
HEAD_DIM = 128
POOL_WINDOWS = (2, 4, 8, 16)
LANES = 128
MXU_DIM = 256
MIB = 1024 * 1024
LOG2E = 1.4426950408889634


def _cparams(dims, vmem_mib):
    return pltpu.CompilerParams(dimension_semantics=dims,
                                vmem_limit_bytes=vmem_mib * MIB)


def _norm_mod(x, g, shift, scale):
    ms = jnp.mean(x * x, axis=-1, keepdims=True)
    xn = x * lax.rsqrt(ms + EPS) * g
    return xn * (1.0 + scale) + shift


def _ada_kernel(c_ref, w_ref, b_ref, o_ref):
    c = c_ref[...]
    ca = c * (1.0 / (1.0 + jnp.exp(-c)))
    o_ref[...] = jnp.dot(ca.astype(BF16), w_ref[...].astype(BF16),
                         preferred_element_type=F32) + b_ref[...]


def _ada_call(c_pad, w, b, tn=1024):
    m, d = c_pad.shape
    n = w.shape[1]
    return pl.pallas_call(
        _ada_kernel,
        grid=(n // tn,),
        in_specs=[pl.BlockSpec((m, d), lambda j: (0, 0)),
                  pl.BlockSpec((d, tn), lambda j: (0, j)),
                  pl.BlockSpec((1, tn), lambda j: (0, j))],
        out_specs=pl.BlockSpec((m, tn), lambda j: (0, j)),
        out_shape=jax.ShapeDtypeStruct((m, n), F32),
        compiler_params=_cparams(("arbitrary",), 40),
        name="ada_mod",
    )(c_pad, w, b)


def _ffn_kernel(x_ref, g_ref, sh_ref, sc_ref, gate_ref, a0_ref, a1_ref, b0_ref, b1_ref,
                r0_ref, r1_ref, *rest, nj, n_full, final_norm):
    if final_norm:
        fg_ref, o_ref, h_scr, win_scr, wout_scr = rest
    else:
        o_ref, h_scr, win_scr, wout_scr = rest
    j = pl.program_id(1)

    @pl.when(j == 0)
    def _():
        h = _norm_mod(x_ref[...], g_ref[...], sh_ref[0], sc_ref[0])
        h_scr[...] = h.astype(BF16)
        o_ref[...] = jnp.zeros_like(o_ref)

    keep = j < n_full
    win_scr[:, 0 * LANES:1 * LANES] = a0_ref[...].astype(BF16)
    win_scr[:, 1 * LANES:2 * LANES] = jnp.where(keep, a1_ref[...], 0.0).astype(BF16)
    win_scr[:, 2 * LANES:3 * LANES] = b0_ref[...].astype(BF16)
    win_scr[:, 3 * LANES:4 * LANES] = jnp.where(keep, b1_ref[...], 0.0).astype(BF16)
    wout_scr[0 * LANES:1 * LANES, :] = r0_ref[...].astype(BF16)
    wout_scr[1 * LANES:2 * LANES, :] = r1_ref[...].astype(BF16)

    hw = jnp.dot(h_scr[...], win_scr[...], preferred_element_type=F32)
    a = hw[:, :2 * LANES]
    b = hw[:, 2 * LANES:]
    act = (a * (1.0 / (1.0 + jnp.exp(-a))) * b).astype(BF16)
    o_ref[...] += jnp.dot(act, wout_scr[...], preferred_element_type=F32)

    @pl.when(j == nj - 1)
    def _():
        y = x_ref[...] + 0.5 * gate_ref[0] * o_ref[...]
        if final_norm:
            ms = jnp.mean(y * y, axis=-1, keepdims=True)
            y = y * lax.rsqrt(ms + EPS) * fg_ref[...]
        o_ref[...] = y


def _ffn_call(x, g, shift, scale, gate, w_in, w_out, final_g, *, seq, tm=1024):
    t, d = x.shape
    dff = w_out.shape[0]
    nblk = dff // LANES
    nj = -(-nblk // 2)
    n_full = nblk // 2
    second = lambda j: jnp.minimum(2 * j + 1, nblk - 1)
    tiles_per_seq = seq // tm
    mod_spec = pl.BlockSpec((1, 1, d), lambda i, j: (i // tiles_per_seq, 0, 0))
    vec_spec = pl.BlockSpec((1, d), lambda i, j: (0, 0))
    col = lambda f: pl.BlockSpec((d, LANES), lambda i, j: (0, f(j)))
    row = lambda f: pl.BlockSpec((LANES, d), lambda i, j: (f(j), 0))
    in_specs = [pl.BlockSpec((tm, d), lambda i, j: (i, 0), pipeline_mode=pl.Buffered(1)),
                vec_spec, mod_spec, mod_spec, mod_spec,
                col(lambda j: 2 * j), col(second),
                col(lambda j: nblk + 2 * j), col(lambda j: nblk + second(j)),
                row(lambda j: 2 * j), row(second)]
    args = [x, g, shift, scale, gate, w_in, w_in, w_in, w_in, w_out, w_out]
    if final_g is not None:
        in_specs.append(vec_spec)
        args.append(final_g)
    return pl.pallas_call(
        functools.partial(_ffn_kernel, nj=nj, n_full=n_full,
                          final_norm=final_g is not None),
        grid=(t // tm, nj),
        in_specs=in_specs,
        out_specs=pl.BlockSpec((tm, d), lambda i, j: (i, 0)),
        out_shape=jax.ShapeDtypeStruct((t, d), F32),
        scratch_shapes=[pltpu.VMEM((tm, d), BF16),
                        pltpu.VMEM((d, 4 * LANES), BF16),
                        pltpu.VMEM((2 * LANES, d), BF16)],
        compiler_params=_cparams(("parallel", "arbitrary"), 60),
        name="ffn_final" if final_g is not None else "ffn",
    )(*args)


def _inproj_kernel(x_ref, g_ref, sh_ref, sc_ref, w_ref, wf_ref, qg_ref, kg_ref,
                   qkv_ref, u_ref, f_ref, h_scr, *, tn, n_qk, n_qkv):
    j = pl.program_id(1)

    @pl.when(j == 0)
    def _():
        h = _norm_mod(x_ref[...], g_ref[...], sh_ref[0], sc_ref[0])
        h_scr[...] = h.astype(BF16)
        f_ref[...] = jnp.dot(h_scr[...], wf_ref[...], preferred_element_type=F32)

    acc = jnp.dot(h_scr[...], w_ref[...], preferred_element_type=F32)

    @pl.when(j < n_qk)
    def _():
        gain = jnp.where(j < n_qk // 2, qg_ref[...], kg_ref[...])
        for hh in range(tn // HEAD_DIM):
            sl = slice(hh * HEAD_DIM, (hh + 1) * HEAD_DIM)
            xh = acc[:, sl]
            ms = jnp.mean(xh * xh, axis=-1, keepdims=True)
            qkv_ref[:, sl] = (xh * lax.rsqrt(ms + EPS) * gain).astype(BF16)

    @pl.when(jnp.logical_and(j >= n_qk, j < n_qkv))
    def _():
        qkv_ref[...] = acc.astype(BF16)

    @pl.when(j >= n_qkv)
    def _():
        u_ref[...] = acc


def _inproj_call(x, g, shift, scale, w_main, w_f, q_gain, k_gain, *, seq, d_attn,
                 tm=1024, tn=512):
    t, d = x.shape
    n_main = w_main.shape[1]
    nj = n_main // tn
    n_qk = 2 * d_attn // tn
    n_qkv = 3 * d_attn // tn
    d_pool = n_main - 3 * d_attn
    tiles_per_seq = seq // tm
    mod_spec = pl.BlockSpec((1, 1, d), lambda i, j: (i // tiles_per_seq, 0, 0))
    return pl.pallas_call(
        functools.partial(_inproj_kernel, tn=tn, n_qk=n_qk, n_qkv=n_qkv),
        grid=(t // tm, nj),
        in_specs=[pl.BlockSpec((tm, d), lambda i, j: (i, 0)),
                  pl.BlockSpec((1, d), lambda i, j: (0, 0)),
                  mod_spec, mod_spec,
                  pl.BlockSpec((d, tn), lambda i, j: (0, j)),
                  pl.BlockSpec((d, LANES), lambda i, j: (0, 0)),
                  pl.BlockSpec((1, HEAD_DIM), lambda i, j: (0, 0)),
                  pl.BlockSpec((1, HEAD_DIM), lambda i, j: (0, 0))],
        out_specs=[
            pl.BlockSpec((tm, tn), lambda i, j: (i, jnp.minimum(j, n_qkv - 1))),
            pl.BlockSpec((tm, tn), lambda i, j: (i, jnp.maximum(j - n_qkv, 0))),
            pl.BlockSpec((tm, LANES), lambda i, j: (i, 0))],
        out_shape=[jax.ShapeDtypeStruct((t, 3 * d_attn), BF16),
                   jax.ShapeDtypeStruct((t, d_pool), F32),
                   jax.ShapeDtypeStruct((t, LANES), F32)],
        scratch_shapes=[pltpu.VMEM((tm, d), BF16)],
        compiler_params=_cparams(("parallel", "arbitrary"), 52),
        name="mixer_inproj",
    )(x, g, shift, scale, w_main, w_f, q_gain, k_gain)


def _forget_kernel(f_ref, b_ref, o_ref, pad_scr, *, seq):
    z = f_ref[0] + b_ref[...]
    x = jnp.minimum(z, 0.0) - jnp.log1p(jnp.exp(-jnp.abs(z)))
    pad_scr[pl.ds(0, seq), :] = jnp.zeros((seq, LANES), F32)
    d = 1
    while d < seq:
        pad_scr[pl.ds(seq, seq), :] = x
        x = x + pad_scr[pl.ds(seq - d, seq), :]
        d *= 2
    o_ref[0] = x * LOG2E


def _forget_call(f_logit, b_pad):
    nb, seq, _ = f_logit.shape
    return pl.pallas_call(
        functools.partial(_forget_kernel, seq=seq),
        grid=(nb,),
        in_specs=[pl.BlockSpec((1, seq, LANES), lambda b: (b, 0, 0)),
                  pl.BlockSpec((1, LANES), lambda b: (0, 0))],
        out_specs=pl.BlockSpec((1, seq, LANES), lambda b: (b, 0, 0)),
        out_shape=jax.ShapeDtypeStruct((nb, seq, LANES), F32),
        scratch_shapes=[pltpu.VMEM((2 * seq, LANES), F32)],
        compiler_params=_cparams(("parallel",), 40),
        name="forget_cumsum",
    )(f_logit, b_pad)


POOL_HALO = 16
POOL_ROWS = 512


def _pool_kernel(u_ref, w_ref, s_ref, o_ref, pad_scr, *, seq):
    g = pl.program_id(1)
    cg = u_ref.shape[2]
    pad_scr[pl.ds(0, POOL_HALO), :] = jnp.zeros((POOL_HALO, cg), F32)
    pad_scr[pl.ds(POOL_HALO, seq), :] = u_ref[0]
    w = w_ref[0].astype(BF16)
    scale = s_ref[...]
    for gi, win in enumerate(POOL_WINDOWS):
        @pl.when(g == gi)
        def _(win=win):
            for r0 in range(0, seq, POOL_ROWS):
                tok = pad_scr[pl.ds(POOL_HALO + r0, POOL_ROWS), :]
                tot = tok
                for dd in range(1, win):
                    tot = tot + pad_scr[pl.ds(POOL_HALO + r0 - dd, POOL_ROWS), :]
                pos = r0 + lax.broadcasted_iota(jnp.int32, (POOL_ROWS, cg), 0)
                cnt = jnp.minimum(pos + 1, win).astype(F32)
                p = tot / cnt - tok
                y = jnp.dot(p.astype(BF16), w, preferred_element_type=F32) * scale
                o_ref[0, pl.ds(r0, POOL_ROWS), :] = y.astype(BF16)


def _pool_call(u, pool_w, pool_scale):
    nb, seq, d_pool = u.shape
    ng, cg, _ = pool_w.shape
    return pl.pallas_call(
        functools.partial(_pool_kernel, seq=seq),
        grid=(nb, ng),
        in_specs=[pl.BlockSpec((1, seq, cg), lambda b, g: (b, 0, g)),
                  pl.BlockSpec((1, cg, cg), lambda b, g: (g, 0, 0)),
                  pl.BlockSpec((1, cg), lambda b, g: (0, g))],
        out_specs=pl.BlockSpec((1, seq, cg), lambda b, g: (b, 0, g)),
        out_shape=jax.ShapeDtypeStruct((nb, seq, d_pool), BF16),
        scratch_shapes=[pltpu.VMEM((POOL_HALO + seq, cg), F32)],
        compiler_params=_cparams(("parallel", "arbitrary"), 40),
        name="ms_pool",
    )(u, pool_w, pool_scale)


NEG_BIG = -1e30


def _lane_tile_reduce(x, op):
    out = x[:, :LANES]
    for t in range(1, x.shape[1] // LANES):
        out = op(out, x[:, t * LANES:(t + 1) * LANES])
    return out


def _attn_kernel(q_ref, k_ref, v_ref, fc_ref, fr_ref, o_ref, s_scr, p_scr, *, tq):
    seq = q_ref.shape[0]
    for i in range(seq // tq):
        slot = i % 2
        q = q_ref[pl.ds(i * tq, tq), :]
        fcol = fc_ref[0, 0, pl.ds(i * tq, tq), :]
        n_chunks = i + 1
        m_tile = None
        for c in range(n_chunks):
            k = k_ref[pl.ds(c * tq, tq), :]
            frow = fr_ref[0, 0, :, pl.ds(c * tq, tq)]
            s = lax.dot_general(q, k, (((1,), (1,)), ((), ())),
                                preferred_element_type=F32)
            s = s + (fcol - frow)
            if c == i:
                row = lax.broadcasted_iota(jnp.int32, (tq, tq), 0)
                col = lax.broadcasted_iota(jnp.int32, (tq, tq), 1)
                s = jnp.where(row >= col, s, NEG_BIG)
            s_scr[slot, :, pl.ds(c * tq, tq)] = s
            cm = _lane_tile_reduce(s, jnp.maximum)
            m_tile = cm if m_tile is None else jnp.maximum(m_tile, cm)
        m_row = jnp.max(m_tile, axis=-1, keepdims=True)
        l_tile = None
        for c in range(n_chunks):
            p = jnp.exp2(s_scr[slot, :, pl.ds(c * tq, tq)] - m_row)
            p_scr[slot, :, pl.ds(c * tq, tq)] = p.astype(BF16)
            cl = _lane_tile_reduce(p, jnp.add)
            l_tile = cl if l_tile is None else l_tile + cl
        l_row = jnp.sum(l_tile, axis=-1, keepdims=True)
        kend = n_chunks * tq
        acc = jnp.dot(p_scr[slot, :, pl.ds(0, kend)], v_ref[pl.ds(0, kend), :],
                      preferred_element_type=F32)
        o_ref[pl.ds(i * tq, tq), :] = (acc / l_row).astype(BF16)


def _attn_call(qkv, fcol, frow, *, nb, seq, n_heads, tq=512):
    t = qkv.shape[0]
    d_attn = n_heads * HEAD_DIM
    head_spec = lambda off: pl.BlockSpec((seq, HEAD_DIM), lambda b, h: (b, off + h))
    return pl.pallas_call(
        functools.partial(_attn_kernel, tq=tq),
        grid=(nb, n_heads),
        in_specs=[head_spec(0), head_spec(n_heads), head_spec(2 * n_heads),
                  pl.BlockSpec((1, 1, seq, 1), lambda b, h: (b, h, 0, 0)),
                  pl.BlockSpec((1, 1, 1, seq), lambda b, h: (b, h, 0, 0))],
        out_specs=head_spec(0),
        out_shape=jax.ShapeDtypeStruct((t, d_attn), BF16),
        scratch_shapes=[pltpu.VMEM((2, tq, seq), F32), pltpu.VMEM((2, tq, seq), BF16)],
        compiler_params=_cparams(("parallel", "parallel"), 56),
        name="fox_attention",
    )(qkv, qkv, qkv, fcol, frow)


def _outproj_kernel(x_ref, attn_ref, pool_ref, wa_ref, wp_ref, gate_ref, o_ref):
    y = jnp.dot(attn_ref[...], wa_ref[...], preferred_element_type=F32)
    y = y + jnp.dot(pool_ref[...], wp_ref[...], preferred_element_type=F32)
    o_ref[...] = x_ref[...] + gate_ref[0] * y


def _outproj_call(x, attn, pool, w_attn, w_pool, gate, *, seq, tm=512):
    t, d = x.shape
    da = attn.shape[1]
    dp = pool.shape[1]
    tiles_per_seq = seq // tm
    return pl.pallas_call(
        _outproj_kernel,
        grid=(t // tm,),
        in_specs=[pl.BlockSpec((tm, d), lambda i: (i, 0)),
                  pl.BlockSpec((tm, da), lambda i: (i, 0)),
                  pl.BlockSpec((tm, dp), lambda i: (i, 0)),
                  pl.BlockSpec((da, d), lambda i: (0, 0)),
                  pl.BlockSpec((dp, d), lambda i: (0, 0)),
                  pl.BlockSpec((1, 1, d), lambda i: (i // tiles_per_seq, 0, 0))],
        out_specs=pl.BlockSpec((tm, d), lambda i: (i, 0)),
        out_shape=jax.ShapeDtypeStruct((t, d), F32),
        compiler_params=_cparams(("parallel",), 48),
        name="mixer_outproj",
    )(x, attn, pool, w_attn, w_pool, gate)


def kernel(x, c, w_ada, b_ada, ffn1_norm_g, ffn1_w_in, ffn1_w_out, mix_norm_g, w_in,
           b_forget, q_norm_g, k_norm_g, pool_w, pool_scale, w_out, ffn2_norm_g,
           ffn2_w_in, ffn2_w_out, final_norm_g):
    nb, seq, d = x.shape
    t = nb * seq
    n_heads = b_forget.shape[1]
    d_attn = n_heads * HEAD_DIM
    d_pool = pool_scale.shape[1]
    depth = w_ada.shape[0]
    xf = x.reshape(t, d)

    c_pad = jnp.pad(c, ((0, 8 - nb), (0, 0)))
    for l in range(depth):
        mod = _ada_call(c_pad, w_ada[l], b_ada[l].reshape(1, -1))[:nb]
        sh1, sc1, g1, sh2, sc2, g2, sh3, sc3, g3 = [
            mod[:, i * d:(i + 1) * d].reshape(nb, 1, d) for i in range(9)]

        xf = _ffn_call(xf, ffn1_norm_g[l].reshape(1, d), sh1, sc1, g1, ffn1_w_in[l],
                       ffn1_w_out[l], None,
                       seq=seq)

        wl = w_in[l]
        f0 = 3 * d_attn
        w_main = jnp.concatenate([wl[:, :f0], wl[:, f0 + n_heads:]], axis=1).astype(BF16)
        w_f = jnp.pad(wl[:, f0:f0 + n_heads], ((0, 0), (0, LANES - n_heads))).astype(BF16)
        q_gain = (q_norm_g[l] * (HEAD_DIM ** -0.5 * LOG2E)).reshape(1, HEAD_DIM)
        k_gain = k_norm_g[l].reshape(1, HEAD_DIM)
        qkv, u, f_logit = _inproj_call(xf, mix_norm_g[l].reshape(1, d), sh2, sc2, w_main,
                                       w_f, q_gain, k_gain, seq=seq, d_attn=d_attn)
        b_pad = jnp.pad(b_forget[l], (0, LANES - n_heads)).reshape(1, LANES)
        fcum = _forget_call(f_logit.reshape(nb, seq, LANES), b_pad)
        fh = fcum[:, :, :n_heads].transpose(0, 2, 1)
        attn = _attn_call(qkv, fh.reshape(nb, n_heads, seq, 1),
                          fh.reshape(nb, n_heads, 1, seq),
                          nb=nb, seq=seq, n_heads=n_heads)
        pool = _pool_call(u.reshape(nb, seq, d_pool), pool_w[l], pool_scale[l].reshape(1, -1))
        wo = w_out[l].astype(BF16)
        xf = _outproj_call(xf, attn, pool.reshape(t, d_pool), wo[:d_attn], wo[d_attn:], g2,
                           seq=seq)

        last = l == depth - 1
        xf = _ffn_call(xf, ffn2_norm_g[l].reshape(1, d), sh3, sc3, g3, ffn2_w_in[l],
                       ffn2_w_out[l],
                       final_norm_g.reshape(1, d) if last else None, seq=seq)
    return xf.reshape(nb, seq, d)
```

```python
import functools

import jax
import jax.numpy as jnp
from jax import lax
from jax.experimental import pallas as pl
from jax.experimental.pallas import tpu as pltpu

F32 = jnp.float32
BF16 = jnp.bfloat16

EPS = 1e-6
HEAD_DIM = 128
POOL_WINDOWS = (2, 4, 8, 16)
LANES = 128
MXU_DIM = 256
MIB = 1024 * 1024
LOG2E = 1.4426950408889634


def _cparams(dims, vmem_mib):
    return pltpu.CompilerParams(dimension_semantics=dims,
                                vmem_limit_bytes=vmem_mib * MIB)


def _norm_mod(x, g, shift, scale):
    ms = jnp.mean(x * x, axis=-1, keepdims=True)
    xn = x * lax.rsqrt(ms + EPS) * g
    return xn * (1.0 + scale) + shift


def _ada_kernel(c_ref, w_ref, b_ref, o_ref):
    c = c_ref[...]
    ca = c * (1.0 / (1.0 + jnp.exp(-c)))
    o_ref[...] = jnp.dot(ca.astype(BF16), w_ref[...].astype(BF16),
                         preferred_element_type=F32) + b_ref[...]


def _ada_call(c_pad, w, b, tn=1024):
    m, d = c_pad.shape
    n = w.shape[1]
    return pl.pallas_call(
        _ada_kernel,
        grid=(n // tn,),
        in_specs=[pl.BlockSpec((m, d), lambda j: (0, 0)),
                  pl.BlockSpec((d, tn), lambda j: (0, j)),
                  pl.BlockSpec((1, tn), lambda j: (0, j))],
        out_specs=pl.BlockSpec((m, tn), lambda j: (0, j)),
        out_shape=jax.ShapeDtypeStruct((m, n), F32),
        compiler_params=_cparams(("arbitrary",), 40),
        name="ada_mod",
    )(c_pad, w, b)


FFN_ROW_CHUNKS = 8


def _ffn_kernel(x_ref, xn_ref, g_ref, sh_ref, sc_ref, shn_ref, scn_ref, gate_ref,
                win_ref, wout_ref, *rest, nj, tf, final_norm):
    if final_norm:
        fg_ref, o_ref, h_scr = rest
    else:
        o_ref, h_scr = rest
    i = pl.program_id(0)
    j = pl.program_id(1)
    cur = i % 2
    tm = x_ref.shape[0]

    @pl.when(jnp.logical_and(i == 0, j == 0))
    def _():
        h = _norm_mod(x_ref[...], g_ref[...], sh_ref[0], sc_ref[0])
        h_scr[0] = h.astype(BF16)

    @pl.when(j == 0)
    def _():
        o_ref[...] = jnp.zeros_like(o_ref)

    hw = jnp.dot(h_scr[cur], win_ref[...], preferred_element_type=F32)
    cps = win_ref.shape[1] // (2 * tf)
    a = jnp.concatenate([hw[:, 2 * c * tf:(2 * c + 1) * tf] for c in range(cps)], axis=1)
    b = jnp.concatenate([hw[:, (2 * c + 1) * tf:(2 * c + 2) * tf] for c in range(cps)],
                        axis=1)
    act = (a * (1.0 / (1.0 + jnp.exp(-a))) * b).astype(BF16)
    o_ref[...] += jnp.dot(act, wout_ref[...], preferred_element_type=F32)

    rc = tm // FFN_ROW_CHUNKS
    r0 = pl.multiple_of(jnp.minimum(j, FFN_ROW_CHUNKS - 1) * rc, rc)
    hn = _norm_mod(xn_ref[pl.ds(r0, rc), :], g_ref[...], shn_ref[0], scn_ref[0])
    h_scr[1 - cur, pl.ds(r0, rc), :] = hn.astype(BF16)

    @pl.when(j == nj - 1)
    def _():
        y = x_ref[...] + 0.5 * gate_ref[0] * o_ref[...]
        if final_norm:
            ms = jnp.mean(y * y, axis=-1, keepdims=True)
            y = y * lax.rsqrt(ms + EPS) * fg_ref[...]
        o_ref[...] = y


def _ffn_call(x, g, shift, scale, gate, w_in_p, w_out_p, final_g, *, seq, tm=512,
              tf=MXU_DIM, cps=2):
    t, d = x.shape
    nj = w_out_p.shape[0] // (cps * tf)
    assert nj >= FFN_ROW_CHUNKS and tm % (8 * FFN_ROW_CHUNKS) == 0
    nt = t // tm
    tiles_per_seq = seq // tm
    nxt = lambda i: jnp.minimum(i + 1, nt - 1)
    mod_spec = pl.BlockSpec((1, 1, d), lambda i, j: (i // tiles_per_seq, 0, 0))
    modn_spec = pl.BlockSpec((1, 1, d), lambda i, j: (nxt(i) // tiles_per_seq, 0, 0))
    vec_spec = pl.BlockSpec((1, d), lambda i, j: (0, 0))
    in_specs = [pl.BlockSpec((tm, d), lambda i, j: (i, 0)),
                pl.BlockSpec((tm, d), lambda i, j: (nxt(i), 0)),
                vec_spec, mod_spec, mod_spec, modn_spec, modn_spec, mod_spec,
                pl.BlockSpec((d, cps * 2 * tf), lambda i, j: (0, j)),
                pl.BlockSpec((cps * tf, d), lambda i, j: (j, 0))]
    args = [x, x, g, shift, scale, shift, scale, gate, w_in_p, w_out_p]
    if final_g is not None:
        in_specs.append(vec_spec)
        args.append(final_g)
    return pl.pallas_call(
        functools.partial(_ffn_kernel, nj=nj, tf=tf, final_norm=final_g is not None),
        grid=(nt, nj),
        in_specs=in_specs,
        out_specs=pl.BlockSpec((tm, d), lambda i, j: (i, 0)),
        out_shape=jax.ShapeDtypeStruct((t, d), F32),
        scratch_shapes=[pltpu.VMEM((2, tm, d), BF16)],
        compiler_params=_cparams(("arbitrary", "arbitrary"), 56),
        name="ffn_final" if final_g is not None else "ffn",
    )(*args)


def _pack_in_kernel(a0_ref, a1_ref, b0_ref, b1_ref, o_ref, *, n_full):
    keep = pl.program_id(0) < n_full
    o_ref[:, 0 * LANES:1 * LANES] = a0_ref[...].astype(BF16)
    o_ref[:, 1 * LANES:2 * LANES] = jnp.where(keep, a1_ref[...], 0.0).astype(BF16)
    o_ref[:, 2 * LANES:3 * LANES] = b0_ref[...].astype(BF16)
    o_ref[:, 3 * LANES:4 * LANES] = jnp.where(keep, b1_ref[...], 0.0).astype(BF16)


def _pack_out_kernel(r0_ref, r1_ref, o_ref, *, n_full):
    keep = pl.program_id(0) < n_full
    o_ref[0 * LANES:1 * LANES, :] = r0_ref[...].astype(BF16)
    o_ref[1 * LANES:2 * LANES, :] = jnp.where(keep, r1_ref[...], 0.0).astype(BF16)


def _pack_ffn_weights(w_in, w_out):
    d, two_ff = w_in.shape
    dff = two_ff // 2
    nblk = dff // LANES
    nj = -(-nblk // 2)
    n_full = nblk // 2
    second = lambda j: jnp.minimum(2 * j + 1, nblk - 1)
    col = lambda f: pl.BlockSpec((d, LANES), lambda j: (0, f(j)))
    w_in_p = pl.pallas_call(
        functools.partial(_pack_in_kernel, n_full=n_full),
        grid=(nj,),
        in_specs=[col(lambda j: 2 * j), col(second),
                  col(lambda j: nblk + 2 * j), col(lambda j: nblk + second(j))],
        out_specs=pl.BlockSpec((d, 4 * LANES), lambda j: (0, j)),
        out_shape=jax.ShapeDtypeStruct((d, nj * 4 * LANES), BF16),
        compiler_params=_cparams(("parallel",), 32),
        name="pack_w_in",
    )(w_in, w_in, w_in, w_in)
    row = lambda f: pl.BlockSpec((LANES, d), lambda j: (f(j), 0))
    w_out_p = pl.pallas_call(
        functools.partial(_pack_out_kernel, n_full=n_full),
        grid=(nj,),
        in_specs=[row(lambda j: 2 * j), row(second)],
        out_specs=pl.BlockSpec((2 * LANES, d), lambda j: (j, 0)),
        out_shape=jax.ShapeDtypeStruct((nj * 2 * LANES, d), BF16),
        compiler_params=_cparams(("parallel",), 32),
        name="pack_w_out",
    )(w_out, w_out)
    return w_in_p, w_out_p


def _inproj_kernel(x_ref, g_ref, sh_ref, sc_ref, w_ref, wf_ref, qg_ref, kg_ref,
                   qkv_ref, u_ref, f_ref, h_scr, *, tn, n_qk, n_qkv):
    j = pl.program_id(1)

    @pl.when(j == 0)
    def _():
        h = _norm_mod(x_ref[...], g_ref[...], sh_ref[0], sc_ref[0])
        h_scr[...] = h.astype(BF16)
        f_ref[...] = jnp.dot(h_scr[...], wf_ref[...], preferred_element_type=F32)

    acc = jnp.dot(h_scr[...], w_ref[...], preferred_element_type=F32)

    @pl.when(j < n_qk)
    def _():
        gain = jnp.where(j < n_qk // 2, qg_ref[...], kg_ref[...])
        for hh in range(tn // HEAD_DIM):
            sl = slice(hh * HEAD_DIM, (hh + 1) * HEAD_DIM)
            xh = acc[:, sl]
            ms = jnp.mean(xh * xh, axis=-1, keepdims=True)
            qkv_ref[:, sl] = (xh * lax.rsqrt(ms + EPS) * gain).astype(BF16)

    @pl.when(jnp.logical_and(j >= n_qk, j < n_qkv))
    def _():
        qkv_ref[...] = acc.astype(BF16)

    @pl.when(j >= n_qkv)
    def _():
        u_ref[...] = acc


def _inproj_call(x, g, shift, scale, w_main, w_f, q_gain, k_gain, *, seq, d_attn,
                 tm=1024, tn=512):
    t, d = x.shape
    n_main = w_main.shape[1]
    nj = n_main // tn
    n_qk = 2 * d_attn // tn
    n_qkv = 3 * d_attn // tn
    d_pool = n_main - 3 * d_attn
    tiles_per_seq = seq // tm
    mod_spec = pl.BlockSpec((1, 1, d), lambda i, j: (i // tiles_per_seq, 0, 0))
    return pl.pallas_call(
        functools.partial(_inproj_kernel, tn=tn, n_qk=n_qk, n_qkv=n_qkv),
        grid=(t // tm, nj),
        in_specs=[pl.BlockSpec((tm, d), lambda i, j: (i, 0)),
                  pl.BlockSpec((1, d), lambda i, j: (0, 0)),
                  mod_spec, mod_spec,
                  pl.BlockSpec((d, tn), lambda i, j: (0, j)),
                  pl.BlockSpec((d, LANES), lambda i, j: (0, 0)),
                  pl.BlockSpec((1, HEAD_DIM), lambda i, j: (0, 0)),
                  pl.BlockSpec((1, HEAD_DIM), lambda i, j: (0, 0))],
        out_specs=[
            pl.BlockSpec((tm, tn), lambda i, j: (i, jnp.minimum(j, n_qkv - 1))),
            pl.BlockSpec((tm, tn), lambda i, j: (i, jnp.maximum(j - n_qkv, 0))),
            pl.BlockSpec((tm, LANES), lambda i, j: (i, 0))],
        out_shape=[jax.ShapeDtypeStruct((t, 3 * d_attn), BF16),
                   jax.ShapeDtypeStruct((t, d_pool), F32),
                   jax.ShapeDtypeStruct((t, LANES), F32)],
        scratch_shapes=[pltpu.VMEM((tm, d), BF16)],
        compiler_params=_cparams(("parallel", "arbitrary"), 52),
        name="mixer_inproj",
    )(x, g, shift, scale, w_main, w_f, q_gain, k_gain)


N_SPLIT = 3


def _forget_kernel(f_ref, b_ref, qe_ref, ke_ref, pad_scr, parts_scr, *, seq, n_heads):
    h = pl.program_id(1)

    @pl.when(h == 0)
    def _():
        z = f_ref[0] + b_ref[...]
        x = jnp.minimum(z, 0.0) - jnp.log1p(jnp.exp(-jnp.abs(z)))
        pad_scr[pl.ds(0, seq), :] = jnp.zeros((seq, LANES), F32)
        d = 1
        while d < seq:
            pad_scr[pl.ds(seq, seq), :] = x
            x = x + pad_scr[pl.ds(seq - d, seq), :]
            d *= 2
        rem = x * LOG2E
        lane = lax.broadcasted_iota(jnp.int32, (seq, LANES), 1)
        for s in range(N_SPLIT):
            piece = rem.astype(BF16)
            rem = rem - piece.astype(F32)
            if s == 0:
                piece = jnp.where(lane == n_heads, jnp.ones_like(piece), piece)
            parts_scr[:, s * LANES:(s + 1) * LANES] = piece

    rows = N_SPLIT * LANES
    r = lax.broadcasted_iota(jnp.int32, (rows, 2 * LANES), 0)
    c = lax.broadcasted_iota(jnp.int32, (rows, 2 * LANES), 1)
    piece_of_row = lax.shift_right_logical(r, LANES.bit_length() - 1)
    is_head_row = (r & (LANES - 1)) == h
    one_row = r == n_heads
    sel = jnp.where(is_head_row & (c == piece_of_row), 1.0, 0.0)
    sel = sel + jnp.where(one_row & (c >= N_SPLIT) & (c < 2 * N_SPLIT), 1.0, 0.0)
    sel = sel + jnp.where(one_row & (c >= LANES) & (c < LANES + N_SPLIT), 1.0, 0.0)
    sel = sel - jnp.where(is_head_row & (c == LANES + N_SPLIT + piece_of_row), 1.0, 0.0)
    e = jnp.dot(parts_scr[...], sel.astype(BF16), preferred_element_type=F32)
    qe_ref[0, 0] = e[:, :LANES].astype(BF16)
    ke_ref[0, 0] = e[:, LANES:].astype(BF16)


def _forget_call(f_logit, b_pad, n_heads):
    nb, seq, _ = f_logit.shape
    out = jax.ShapeDtypeStruct((nb, n_heads, seq, LANES), BF16)
    out_spec = pl.BlockSpec((1, 1, seq, LANES), lambda b, h: (b, h, 0, 0))
    return pl.pallas_call(
        functools.partial(_forget_kernel, seq=seq, n_heads=n_heads),
        grid=(nb, n_heads),
        in_specs=[pl.BlockSpec((1, seq, LANES), lambda b, h: (b, 0, 0)),
                  pl.BlockSpec((1, LANES), lambda b, h: (0, 0))],
        out_specs=[out_spec, out_spec],
        out_shape=[out, out],
        scratch_shapes=[pltpu.VMEM((2 * seq, LANES), F32),
                        pltpu.VMEM((seq, N_SPLIT * LANES), BF16)],
        compiler_params=_cparams(("parallel", "arbitrary"), 40),
        name="forget_cumsum",
    )(f_logit, b_pad)


POOL_HALO = 16
POOL_ROWS = 512


def _pool_kernel(u_ref, w_ref, s_ref, o_ref, pad_scr, *, seq):
    g = pl.program_id(1)
    cg = u_ref.shape[2]
    pad_scr[pl.ds(0, POOL_HALO), :] = jnp.zeros((POOL_HALO, cg), F32)
    pad_scr[pl.ds(POOL_HALO, seq), :] = u_ref[0]
    w = w_ref[0].astype(BF16)
    scale = s_ref[...]
    for gi, win in enumerate(POOL_WINDOWS):
        @pl.when(g == gi)
        def _(win=win):
            for r0 in range(0, seq, POOL_ROWS):
                tok = pad_scr[pl.ds(POOL_HALO + r0, POOL_ROWS), :]
                tot = tok
                for dd in range(1, win):
                    tot = tot + pad_scr[pl.ds(POOL_HALO + r0 - dd, POOL_ROWS), :]
                pos = r0 + lax.broadcasted_iota(jnp.int32, (POOL_ROWS, cg), 0)
                cnt = jnp.minimum(pos + 1, win).astype(F32)
                p = tot / cnt - tok
                y = jnp.dot(p.astype(BF16), w, preferred_element_type=F32) * scale
                o_ref[0, pl.ds(r0, POOL_ROWS), :] = y.astype(BF16)


def _pool_call(u, pool_w, pool_scale):
    nb, seq, d_pool = u.shape
    ng, cg, _ = pool_w.shape
    return pl.pallas_call(
        functools.partial(_pool_kernel, seq=seq),
        grid=(nb, ng),
        in_specs=[pl.BlockSpec((1, seq, cg), lambda b, g: (b, 0, g)),
                  pl.BlockSpec((1, cg, cg), lambda b, g: (g, 0, 0)),
                  pl.BlockSpec((1, cg), lambda b, g: (0, g))],
        out_specs=pl.BlockSpec((1, seq, cg), lambda b, g: (b, 0, g)),
        out_shape=jax.ShapeDtypeStruct((nb, seq, d_pool), BF16),
        scratch_shapes=[pltpu.VMEM((POOL_HALO + seq, cg), F32)],
        compiler_params=_cparams(("parallel", "arbitrary"), 40),
        name="ms_pool",
    )(u, pool_w, pool_scale)


NEG_BIG = -1e30


def _lane_tile_reduce(x, op):
    out = x[:, :LANES]
    for t in range(1, x.shape[1] // LANES):
        out = op(out, x[:, t * LANES:(t + 1) * LANES])
    return out


def _attn_kernel(q_ref, qe_ref, k_ref, ke_ref, v_ref, o_ref, s_scr, p_scr, *, tq):
    seq = q_ref.shape[0]
    for i in range(seq // tq):
        slot = i % 2
        rows = pl.ds(i * tq, tq)
        qx = jnp.concatenate([q_ref[rows, :], qe_ref[0, 0, rows, :]], axis=1)
        n_chunks = i + 1
        m_tile = None
        for c in range(n_chunks):
            cols = pl.ds(c * tq, tq)
            kx = jnp.concatenate([k_ref[cols, :], ke_ref[0, 0, cols, :]], axis=1)
            s = lax.dot_general(qx, kx, (((1,), (1,)), ((), ())),
                                preferred_element_type=F32)
            if c == i:
                row = lax.broadcasted_iota(jnp.int32, (tq, tq), 0)
                col = lax.broadcasted_iota(jnp.int32, (tq, tq), 1)
                s = jnp.where(row >= col, s, NEG_BIG)
            s_scr[slot, :, cols] = s
            cm = _lane_tile_reduce(s, jnp.maximum)
            m_tile = cm if m_tile is None else jnp.maximum(m_tile, cm)
        m_row = jnp.max(m_tile, axis=-1, keepdims=True)
        l_tile = None
        for c in range(n_chunks):
            cols = pl.ds(c * tq, tq)
            p = jnp.exp2(s_scr[slot, :, cols] - m_row)
            p_scr[slot, :, cols] = p.astype(BF16)
            cl = _lane_tile_reduce(p, jnp.add)
            l_tile = cl if l_tile is None else l_tile + cl
        l_row = jnp.sum(l_tile, axis=-1, keepdims=True)
        kend = n_chunks * tq
        acc = jnp.dot(p_scr[slot, :, pl.ds(0, kend)], v_ref[pl.ds(0, kend), :],
                      preferred_element_type=F32)
        o_ref[rows, :] = (acc / l_row).astype(BF16)


def _attn_call(qkv, qe, ke, *, nb, seq, n_heads, tq=512):
    t = qkv.shape[0]
    d_attn = n_heads * HEAD_DIM
    head_spec = lambda off: pl.BlockSpec((seq, HEAD_DIM), lambda b, h: (b, off + h))
    bias_spec = pl.BlockSpec((1, 1, seq, LANES), lambda b, h: (b, h, 0, 0))
    return pl.pallas_call(
        functools.partial(_attn_kernel, tq=tq),
        grid=(nb, n_heads),
        in_specs=[head_spec(0), bias_spec, head_spec(n_heads), bias_spec,
                  head_spec(2 * n_heads)],
        out_specs=head_spec(0),
        out_shape=jax.ShapeDtypeStruct((t, d_attn), BF16),
        scratch_shapes=[pltpu.VMEM((2, tq, seq), F32), pltpu.VMEM((2, tq, seq), BF16)],
        compiler_params=_cparams(("parallel", "parallel"), 56),
        name="fox_attention",
    )(qkv, qe, qkv, ke, qkv)


def _outproj_kernel(x_ref, attn_ref, pool_ref, wa_ref, wp_ref, gate_ref, o_ref):
    y = jnp.dot(attn_ref[...], wa_ref[...], preferred_element_type=F32)
    y = y + jnp.dot(pool_ref[...], wp_ref[...], preferred_element_type=F32)
    o_ref[...] = x_ref[...] + gate_ref[0] * y


def _outproj_call(x, attn, pool, w_attn, w_pool, gate, *, seq, tm=512):
    t, d = x.shape
    da = attn.shape[1]
    dp = pool.shape[1]
    tiles_per_seq = seq // tm
    return pl.pallas_call(
        _outproj_kernel,
        grid=(t // tm,),
        in_specs=[pl.BlockSpec((tm, d), lambda i: (i, 0)),
                  pl.BlockSpec((tm, da), lambda i: (i, 0)),
                  pl.BlockSpec((tm, dp), lambda i: (i, 0)),
                  pl.BlockSpec((da, d), lambda i: (0, 0)),
                  pl.BlockSpec((dp, d), lambda i: (0, 0)),
                  pl.BlockSpec((1, 1, d), lambda i: (i // tiles_per_seq, 0, 0))],
        out_specs=pl.BlockSpec((tm, d), lambda i: (i, 0)),
        out_shape=jax.ShapeDtypeStruct((t, d), F32),
        compiler_params=_cparams(("parallel",), 48),
        name="mixer_outproj",
    )(x, attn, pool, w_attn, w_pool, gate)


def kernel(x, c, w_ada, b_ada, ffn1_norm_g, ffn1_w_in, ffn1_w_out, mix_norm_g, w_in,
           b_forget, q_norm_g, k_norm_g, pool_w, pool_scale, w_out, ffn2_norm_g,
           ffn2_w_in, ffn2_w_out, final_norm_g):
    nb, seq, d = x.shape
    t = nb * seq
    n_heads = b_forget.shape[1]
    d_attn = n_heads * HEAD_DIM
    d_pool = pool_scale.shape[1]
    depth = w_ada.shape[0]
    xf = x.reshape(t, d)

    c_pad = jnp.pad(c, ((0, 8 - nb), (0, 0)))
    for l in range(depth):
        mod = _ada_call(c_pad, w_ada[l], b_ada[l].reshape(1, -1))[:nb]
        sh1, sc1, g1, sh2, sc2, g2, sh3, sc3, g3 = [
            mod[:, i * d:(i + 1) * d].reshape(nb, 1, d) for i in range(9)]

        w1i, w1o = _pack_ffn_weights(ffn1_w_in[l], ffn1_w_out[l])
        xf = _ffn_call(xf, ffn1_norm_g[l].reshape(1, d), sh1, sc1, g1, w1i, w1o, None,
                       seq=seq)

        wl = w_in[l]
        f0 = 3 * d_attn
        w_main = jnp.concatenate([wl[:, :f0], wl[:, f0 + n_heads:]], axis=1).astype(BF16)
        w_f = jnp.pad(wl[:, f0:f0 + n_heads], ((0, 0), (0, LANES - n_heads))).astype(BF16)
        q_gain = (q_norm_g[l] * (HEAD_DIM ** -0.5 * LOG2E)).reshape(1, HEAD_DIM)
        k_gain = k_norm_g[l].reshape(1, HEAD_DIM)
        qkv, u, f_logit = _inproj_call(xf, mix_norm_g[l].reshape(1, d), sh2, sc2, w_main,
                                       w_f, q_gain, k_gain, seq=seq, d_attn=d_attn)
        b_pad = jnp.pad(b_forget[l], (0, LANES - n_heads)).reshape(1, LANES)
        qe, ke = _forget_call(f_logit.reshape(nb, seq, LANES), b_pad, n_heads)
        attn = _attn_call(qkv, qe, ke, nb=nb, seq=seq, n_heads=n_heads)
        pool = _pool_call(u.reshape(nb, seq, d_pool), pool_w[l], pool_scale[l].reshape(1, -1))
        wo = w_out[l].astype(BF16)
        xf = _outproj_call(xf, attn, pool.reshape(t, d_pool), wo[:d_attn], wo[d_attn:], g2,
                           seq=seq)

        w2i, w2o = _pack_ffn_weights(ffn2_w_in[l], ffn2_w_out[l])
        last = l == depth - 1
        xf = _ffn_call(xf, ffn2_norm_g[l].reshape(1, d), sh3, sc3, g3, w2i, w2o,
                       final_norm_g.reshape(1, d) if last else None, seq=seq)
    return xf.reshape(nb, seq, d)
```

```python
import functools

import jax
import jax.numpy as jnp
from jax import lax
from jax.experimental import pallas as pl
from jax.experimental.pallas import tpu as pltpu

F32 = jnp.float32
BF16 = jnp.bfloat16

EPS = 1e-6
HEAD_DIM = 128
POOL_WINDOWS = (2, 4, 8, 16)
LANES = 128
MXU_DIM = 256
MIB = 1024 * 1024
LOG2E = 1.4426950408889634


def _cparams(dims, vmem_mib):
    return pltpu.CompilerParams(dimension_semantics=dims,
                                vmem_limit_bytes=vmem_mib * MIB)


def _norm_mod(x, g, shift, scale):
    ms = jnp.mean(x * x, axis=-1, keepdims=True)
    xn = x * lax.rsqrt(ms + EPS) * g
    return xn * (1.0 + scale) + shift


def _ada_kernel(c_ref, w_ref, b_ref, o_ref):
    c = c_ref[...]
    ca = c * (1.0 / (1.0 + jnp.exp(-c)))
    o_ref[...] = jnp.dot(ca.astype(BF16), w_ref[...].astype(BF16),
                         preferred_element_type=F32) + b_ref[...]


def _ada_call(c_pad, w, b, tn=1024):
    m, d = c_pad.shape
    n = w.shape[1]
    return pl.pallas_call(
        _ada_kernel,
        grid=(n // tn,),
        in_specs=[pl.BlockSpec((m, d), lambda j: (0, 0)),
                  pl.BlockSpec((d, tn), lambda j: (0, j)),
                  pl.BlockSpec((1, tn), lambda j: (0, j))],
        out_specs=pl.BlockSpec((m, tn), lambda j: (0, j)),
        out_shape=jax.ShapeDtypeStruct((m, n), F32),
        compiler_params=_cparams(("arbitrary",), 40),
        name="ada_mod",
    )(c_pad, w, b)


FFN_ROW_CHUNKS = 4
FFN_X_STEP = 3
FFN_XN_STEP = 5


def _ffn_kernel(x_ref, xn_ref, g_ref, sh_ref, sc_ref, shn_ref, scn_ref, gate_ref,
                win_ref, wout_ref, *rest, nj, tf, final_norm):
    if final_norm:
        fg_ref, o_ref, h_scr = rest
    else:
        o_ref, h_scr = rest
    i = pl.program_id(0)
    j = pl.program_id(1)
    cur = i % 2
    tm = x_ref.shape[0]

    @pl.when(jnp.logical_and(i == 0, j == 0))
    def _():
        h = _norm_mod(x_ref[...], g_ref[...], sh_ref[0], sc_ref[0])
        h_scr[0] = h.astype(BF16)

    @pl.when(j == 0)
    def _():
        o_ref[...] = jnp.zeros_like(o_ref)

    hw = jnp.dot(h_scr[cur], win_ref[...], preferred_element_type=F32)
    cps = win_ref.shape[1] // (2 * tf)
    a = jnp.concatenate([hw[:, 2 * c * tf:(2 * c + 1) * tf] for c in range(cps)], axis=1)
    b = jnp.concatenate([hw[:, (2 * c + 1) * tf:(2 * c + 2) * tf] for c in range(cps)],
                        axis=1)
    act = (a * (1.0 / (1.0 + jnp.exp(-a))) * b).astype(BF16)
    o_ref[...] += jnp.dot(act, wout_ref[...], preferred_element_type=F32)

    rc = tm // FFN_ROW_CHUNKS
    r0 = pl.multiple_of(jnp.clip(j - FFN_XN_STEP, 0, FFN_ROW_CHUNKS - 1) * rc, rc)
    hn = _norm_mod(xn_ref[pl.ds(r0, rc), :], g_ref[...], shn_ref[0], scn_ref[0])
    h_scr[1 - cur, pl.ds(r0, rc), :] = hn.astype(BF16)

    @pl.when(j == nj - 1)
    def _():
        y = x_ref[...] + 0.5 * gate_ref[0] * o_ref[...]
        if final_norm:
            ms = jnp.mean(y * y, axis=-1, keepdims=True)
            y = y * lax.rsqrt(ms + EPS) * fg_ref[...]
        o_ref[...] = y


def _ffn_call(x, g, shift, scale, gate, w_in_p, w_out_p, final_g, *, seq, tm=512,
              tf=MXU_DIM, cps=2):
    t, d = x.shape
    nj = w_out_p.shape[0] // (cps * tf)
    assert nj >= FFN_XN_STEP + FFN_ROW_CHUNKS and tm % (8 * FFN_ROW_CHUNKS) == 0
    nt = t // tm
    tiles_per_seq = seq // tm
    nxt = lambda i: jnp.minimum(i + 1, nt - 1)
    mod_spec = pl.BlockSpec((1, 1, d), lambda i, j: (i // tiles_per_seq, 0, 0))
    modn_spec = pl.BlockSpec((1, 1, d), lambda i, j: (nxt(i) // tiles_per_seq, 0, 0))
    vec_spec = pl.BlockSpec((1, d), lambda i, j: (0, 0))
    x_tile = lambda i, j: jnp.where(j >= FFN_X_STEP, i, jnp.maximum(i - 1, 0))
    xn_tile = lambda i, j: jnp.where(j >= FFN_XN_STEP, nxt(i), i)
    in_specs = [pl.BlockSpec((tm, d), lambda i, j: (x_tile(i, j), 0)),
                pl.BlockSpec((tm, d), lambda i, j: (xn_tile(i, j), 0)),
                vec_spec, mod_spec, mod_spec, modn_spec, modn_spec, mod_spec,
                pl.BlockSpec((d, cps * 2 * tf), lambda i, j: (0, j)),
                pl.BlockSpec((cps * tf, d), lambda i, j: (j, 0))]
    args = [x, x, g, shift, scale, shift, scale, gate, w_in_p, w_out_p]
    if final_g is not None:
        in_specs.append(vec_spec)
        args.append(final_g)
    return pl.pallas_call(
        functools.partial(_ffn_kernel, nj=nj, tf=tf, final_norm=final_g is not None),
        grid=(nt, nj),
        in_specs=in_specs,
        out_specs=pl.BlockSpec((tm, d), lambda i, j: (i, 0)),
        out_shape=jax.ShapeDtypeStruct((t, d), F32),
        scratch_shapes=[pltpu.VMEM((2, tm, d), BF16)],
        compiler_params=_cparams(("arbitrary", "arbitrary"), 56),
        name="ffn_final" if final_g is not None else "ffn",
    )(*args)


def _pack_in_kernel(a0_ref, a1_ref, b0_ref, b1_ref, o_ref, *, n_full):
    keep = pl.program_id(0) < n_full
    o_ref[:, 0 * LANES:1 * LANES] = a0_ref[...].astype(BF16)
    o_ref[:, 1 * LANES:2 * LANES] = jnp.where(keep, a1_ref[...], 0.0).astype(BF16)
    o_ref[:, 2 * LANES:3 * LANES] = b0_ref[...].astype(BF16)
    o_ref[:, 3 * LANES:4 * LANES] = jnp.where(keep, b1_ref[...], 0.0).astype(BF16)


def _pack_out_kernel(r0_ref, r1_ref, o_ref, *, n_full):
    keep = pl.program_id(0) < n_full
    o_ref[0 * LANES:1 * LANES, :] = r0_ref[...].astype(BF16)
    o_ref[1 * LANES:2 * LANES, :] = jnp.where(keep, r1_ref[...], 0.0).astype(BF16)


def _pack_ffn_weights(w_in, w_out):
    d, two_ff = w_in.shape
    dff = two_ff // 2
    nblk = dff // LANES
    nj = -(-nblk // 2)
    n_full = nblk // 2
    second = lambda j: jnp.minimum(2 * j + 1, nblk - 1)
    col = lambda f: pl.BlockSpec((d, LANES), lambda j: (0, f(j)))
    w_in_p = pl.pallas_call(
        functools.partial(_pack_in_kernel, n_full=n_full),
        grid=(nj,),
        in_specs=[col(lambda j: 2 * j), col(second),
                  col(lambda j: nblk + 2 * j), col(lambda j: nblk + second(j))],
        out_specs=pl.BlockSpec((d, 4 * LANES), lambda j: (0, j)),
        out_shape=jax.ShapeDtypeStruct((d, nj * 4 * LANES), BF16),
        compiler_params=_cparams(("parallel",), 32),
        name="pack_w_in",
    )(w_in, w_in, w_in, w_in)
    row = lambda f: pl.BlockSpec((LANES, d), lambda j: (f(j), 0))
    w_out_p = pl.pallas_call(
        functools.partial(_pack_out_kernel, n_full=n_full),
        grid=(nj,),
        in_specs=[row(lambda j: 2 * j), row(second)],
        out_specs=pl.BlockSpec((2 * LANES, d), lambda j: (j, 0)),
        out_shape=jax.ShapeDtypeStruct((nj * 2 * LANES, d), BF16),
        compiler_params=_cparams(("parallel",), 32),
        name="pack_w_out",
    )(w_out, w_out)
    return w_in_p, w_out_p


def _inproj_kernel(x_ref, g_ref, sh_ref, sc_ref, w_ref, wf_ref, qg_ref, kg_ref,
                   qkv_ref, u_ref, f_ref, h_scr, *, tn, n_qk, n_qkv):
    j = pl.program_id(1)

    @pl.when(j == 0)
    def _():
        h = _norm_mod(x_ref[...], g_ref[...], sh_ref[0], sc_ref[0])
        h_scr[...] = h.astype(BF16)
        f_ref[...] = jnp.dot(h_scr[...], wf_ref[...], preferred_element_type=F32)

    acc = jnp.dot(h_scr[...], w_ref[...], preferred_element_type=F32)

    @pl.when(j < n_qk)
    def _():
        gain = jnp.where(j < n_qk // 2, qg_ref[...], kg_ref[...])
        for hh in range(tn // HEAD_DIM):
            sl = slice(hh * HEAD_DIM, (hh + 1) * HEAD_DIM)
            xh = acc[:, sl]
            ms = jnp.mean(xh * xh, axis=-1, keepdims=True)
            qkv_ref[:, sl] = (xh * lax.rsqrt(ms + EPS) * gain).astype(BF16)

    @pl.when(jnp.logical_and(j >= n_qk, j < n_qkv))
    def _():
        qkv_ref[...] = acc.astype(BF16)

    @pl.when(j >= n_qkv)
    def _():
        u_ref[...] = acc


def _inproj_call(x, g, shift, scale, w_main, w_f, q_gain, k_gain, *, seq, d_attn,
                 tm=1024, tn=512):
    t, d = x.shape
    n_main = w_main.shape[1]
    nj = n_main // tn
    n_qk = 2 * d_attn // tn
    n_qkv = 3 * d_attn // tn
    d_pool = n_main - 3 * d_attn
    tiles_per_seq = seq // tm
    mod_spec = pl.BlockSpec((1, 1, d), lambda i, j: (i // tiles_per_seq, 0, 0))
    return pl.pallas_call(
        functools.partial(_inproj_kernel, tn=tn, n_qk=n_qk, n_qkv=n_qkv),
        grid=(t // tm, nj),
        in_specs=[pl.BlockSpec((tm, d), lambda i, j: (i, 0)),
                  pl.BlockSpec((1, d), lambda i, j: (0, 0)),
                  mod_spec, mod_spec,
                  pl.BlockSpec((d, tn), lambda i, j: (0, j)),
                  pl.BlockSpec((d, LANES), lambda i, j: (0, 0)),
                  pl.BlockSpec((1, HEAD_DIM), lambda i, j: (0, 0)),
                  pl.BlockSpec((1, HEAD_DIM), lambda i, j: (0, 0))],
        out_specs=[
            pl.BlockSpec((tm, tn), lambda i, j: (i, jnp.minimum(j, n_qkv - 1))),
            pl.BlockSpec((tm, tn), lambda i, j: (i, jnp.maximum(j - n_qkv, 0))),
            pl.BlockSpec((tm, LANES), lambda i, j: (i, 0))],
        out_shape=[jax.ShapeDtypeStruct((t, 3 * d_attn), BF16),
                   jax.ShapeDtypeStruct((t, d_pool), F32),
                   jax.ShapeDtypeStruct((t, LANES), F32)],
        scratch_shapes=[pltpu.VMEM((tm, d), BF16)],
        compiler_params=_cparams(("parallel", "arbitrary"), 52),
        name="mixer_inproj",
    )(x, g, shift, scale, w_main, w_f, q_gain, k_gain)


N_SPLIT = 3


def _bias_selector(n_heads):
    rows = N_SPLIT * LANES
    hbits = n_heads.bit_length() - 1
    r = lax.broadcasted_iota(jnp.int32, (rows, 2 * LANES), 0)
    c = lax.broadcasted_iota(jnp.int32, (rows, 2 * LANES), 1)
    in_q = c < 2 * N_SPLIT * n_heads
    in_k = (c >= LANES) & (c < LANES + 2 * N_SPLIT * n_heads)
    ck = c - LANES
    slot_q = lax.shift_right_logical(c, hbits)
    slot_k = lax.shift_right_logical(jnp.maximum(ck, 0), hbits)
    head_q = c & (n_heads - 1)
    head_k = ck & (n_heads - 1)
    one_row = r == n_heads
    q_piece = in_q & (slot_q < N_SPLIT) & (r == slot_q * LANES + head_q)
    q_one = in_q & (slot_q >= N_SPLIT) & one_row
    k_one = in_k & (slot_k < N_SPLIT) & one_row
    k_piece = in_k & (slot_k >= N_SPLIT) & (r == (slot_k - N_SPLIT) * LANES + head_k)
    sel = jnp.where(q_piece | q_one | k_one, 1.0, 0.0) - jnp.where(k_piece, 1.0, 0.0)
    return sel.astype(BF16)


def _forget_kernel(f_ref, b_ref, qe_ref, ke_ref, pad_scr, *, seq, n_heads):
    z = f_ref[0] + b_ref[...]
    x = jnp.minimum(z, 0.0) - jnp.log1p(jnp.exp(-jnp.abs(z)))
    pad_scr[pl.ds(0, seq), :] = jnp.zeros((seq, LANES), F32)
    d = 1
    while d < seq:
        pad_scr[pl.ds(seq, seq), :] = x
        x = x + pad_scr[pl.ds(seq - d, seq), :]
        d *= 2
    rem = x * LOG2E
    lane = lax.broadcasted_iota(jnp.int32, (seq, LANES), 1)
    pieces = []
    for s in range(N_SPLIT):
        piece = rem.astype(BF16)
        rem = rem - piece.astype(F32)
        if s == 0:
            piece = jnp.where(lane == n_heads, jnp.ones_like(piece), piece)
        pieces.append(piece)
    e = jnp.dot(jnp.concatenate(pieces, axis=1), _bias_selector(n_heads),
                preferred_element_type=F32)
    qe_ref[0] = e[:, :LANES].astype(BF16)
    ke_ref[0] = e[:, LANES:].astype(BF16)


def _forget_call(f_logit, b_pad, n_heads):
    nb, seq, _ = f_logit.shape
    assert n_heads & (n_heads - 1) == 0 and 2 * N_SPLIT * n_heads <= LANES
    out = jax.ShapeDtypeStruct((nb, seq, LANES), BF16)
    spec = pl.BlockSpec((1, seq, LANES), lambda b: (b, 0, 0))
    return pl.pallas_call(
        functools.partial(_forget_kernel, seq=seq, n_heads=n_heads),
        grid=(nb,),
        in_specs=[spec, pl.BlockSpec((1, LANES), lambda b: (0, 0))],
        out_specs=[spec, spec],
        out_shape=[out, out],
        scratch_shapes=[pltpu.VMEM((2 * seq, LANES), F32)],
        compiler_params=_cparams(("parallel",), 40),
        name="forget_cumsum",
    )(f_logit, b_pad)


POOL_HALO = 16
POOL_ROWS = 512


def _pool_kernel(u_ref, w_ref, s_ref, o_ref, pad_scr, *, seq):
    g = pl.program_id(1)
    cg = u_ref.shape[2]
    pad_scr[pl.ds(0, POOL_HALO), :] = jnp.zeros((POOL_HALO, cg), F32)
    pad_scr[pl.ds(POOL_HALO, seq), :] = u_ref[0]
    w = w_ref[0].astype(BF16)
    scale = s_ref[...]
    for gi, win in enumerate(POOL_WINDOWS):
        @pl.when(g == gi)
        def _(win=win):
            for r0 in range(0, seq, POOL_ROWS):
                tok = pad_scr[pl.ds(POOL_HALO + r0, POOL_ROWS), :]
                tot = tok
                for dd in range(1, win):
                    tot = tot + pad_scr[pl.ds(POOL_HALO + r0 - dd, POOL_ROWS), :]
                pos = r0 + lax.broadcasted_iota(jnp.int32, (POOL_ROWS, cg), 0)
                cnt = jnp.minimum(pos + 1, win).astype(F32)
                p = tot / cnt - tok
                y = jnp.dot(p.astype(BF16), w, preferred_element_type=F32) * scale
                o_ref[0, pl.ds(r0, POOL_ROWS), :] = y.astype(BF16)


def _pool_call(u, pool_w, pool_scale):
    nb, seq, d_pool = u.shape
    ng, cg, _ = pool_w.shape
    return pl.pallas_call(
        functools.partial(_pool_kernel, seq=seq),
        grid=(nb, ng),
        in_specs=[pl.BlockSpec((1, seq, cg), lambda b, g: (b, 0, g)),
                  pl.BlockSpec((1, cg, cg), lambda b, g: (g, 0, 0)),
                  pl.BlockSpec((1, cg), lambda b, g: (0, g))],
        out_specs=pl.BlockSpec((1, seq, cg), lambda b, g: (b, 0, g)),
        out_shape=jax.ShapeDtypeStruct((nb, seq, d_pool), BF16),
        scratch_shapes=[pltpu.VMEM((POOL_HALO + seq, cg), F32)],
        compiler_params=_cparams(("parallel", "arbitrary"), 40),
        name="ms_pool",
    )(u, pool_w, pool_scale)


NEG_BIG = -1e30


def _lane_tile_reduce(x, op):
    out = x[:, :LANES]
    for t in range(1, x.shape[1] // LANES):
        out = op(out, x[:, t * LANES:(t + 1) * LANES])
    return out


def _attn_kernel(q_ref, qe_ref, k_ref, ke_ref, v_ref, o_ref, s_scr, p_scr, ke_scr,
                 *, tq, n_heads):
    seq = q_ref.shape[0]
    lane = lax.broadcasted_iota(jnp.int32, (seq, LANES), 1)
    mine = ((lane & (n_heads - 1)) == pl.program_id(1)) & (lane < 2 * N_SPLIT * n_heads)
    ke = ke_ref[0]
    ke_scr[...] = jnp.where(mine, ke, jnp.zeros_like(ke))
    for i in range(seq // tq):
        slot = i % 2
        rows = pl.ds(i * tq, tq)
        qx = jnp.concatenate([q_ref[rows, :], qe_ref[0, rows, :]], axis=1)
        n_chunks = i + 1
        m_tile = None
        for c in range(n_chunks):
            cols = pl.ds(c * tq, tq)
            kx = jnp.concatenate([k_ref[cols, :], ke_scr[cols, :]], axis=1)
            s = lax.dot_general(qx, kx, (((1,), (1,)), ((), ())),
                                preferred_element_type=F32)
            if c == i:
                row = lax.broadcasted_iota(jnp.int32, (tq, tq), 0)
                col = lax.broadcasted_iota(jnp.int32, (tq, tq), 1)
                s = jnp.where(row >= col, s, NEG_BIG)
            s_scr[slot, :, cols] = s
            cm = _lane_tile_reduce(s, jnp.maximum)
            m_tile = cm if m_tile is None else jnp.maximum(m_tile, cm)
        m_row = jnp.max(m_tile, axis=-1, keepdims=True)
        l_tile = None
        for c in range(n_chunks):
            cols = pl.ds(c * tq, tq)
            p = jnp.exp2(s_scr[slot, :, cols] - m_row)
            p_scr[slot, :, cols] = p.astype(BF16)
            cl = _lane_tile_reduce(p, jnp.add)
            l_tile = cl if l_tile is None else l_tile + cl
        l_row = jnp.sum(l_tile, axis=-1, keepdims=True)
        kend = n_chunks * tq
        acc = jnp.dot(p_scr[slot, :, pl.ds(0, kend)], v_ref[pl.ds(0, kend), :],
                      preferred_element_type=F32)
        o_ref[rows, :] = (acc / l_row).astype(BF16)


def _attn_call(qkv, qe, ke, *, nb, seq, n_heads, tq=512):
    t = qkv.shape[0]
    d_attn = n_heads * HEAD_DIM
    head_spec = lambda off: pl.BlockSpec((seq, HEAD_DIM), lambda b, h: (b, off + h))
    bias_spec = pl.BlockSpec((1, seq, LANES), lambda b, h: (b, 0, 0))
    return pl.pallas_call(
        functools.partial(_attn_kernel, tq=tq, n_heads=n_heads),
        grid=(nb, n_heads),
        in_specs=[head_spec(0), bias_spec, head_spec(n_heads), bias_spec,
                  head_spec(2 * n_heads)],
        out_specs=head_spec(0),
        out_shape=jax.ShapeDtypeStruct((t, d_attn), BF16),
        scratch_shapes=[pltpu.VMEM((2, tq, seq), F32), pltpu.VMEM((2, tq, seq), BF16),
                        pltpu.VMEM((seq, LANES), BF16)],
        compiler_params=_cparams(("parallel", "parallel"), 56),
        name="fox_attention",
    )(qkv, qe, qkv, ke, qkv)


def _outproj_kernel(x_ref, attn_ref, pool_ref, wa_ref, wp_ref, gate_ref, o_ref):
    y = jnp.dot(attn_ref[...], wa_ref[...], preferred_element_type=F32)
    y = y + jnp.dot(pool_ref[...], wp_ref[...], preferred_element_type=F32)
    o_ref[...] = x_ref[...] + gate_ref[0] * y


def _outproj_call(x, attn, pool, w_attn, w_pool, gate, *, seq, tm=512):
    t, d = x.shape
    da = attn.shape[1]
    dp = pool.shape[1]
    tiles_per_seq = seq // tm
    return pl.pallas_call(
        _outproj_kernel,
        grid=(t // tm,),
        in_specs=[pl.BlockSpec((tm, d), lambda i: (i, 0)),
                  pl.BlockSpec((tm, da), lambda i: (i, 0)),
                  pl.BlockSpec((tm, dp), lambda i: (i, 0)),
                  pl.BlockSpec((da, d), lambda i: (0, 0)),
                  pl.BlockSpec((dp, d), lambda i: (0, 0)),
                  pl.BlockSpec((1, 1, d), lambda i: (i // tiles_per_seq, 0, 0))],
        out_specs=pl.BlockSpec((tm, d), lambda i: (i, 0)),
        out_shape=jax.ShapeDtypeStruct((t, d), F32),
        compiler_params=_cparams(("parallel",), 48),
        name="mixer_outproj",
    )(x, attn, pool, w_attn, w_pool, gate)


def kernel(x, c, w_ada, b_ada, ffn1_norm_g, ffn1_w_in, ffn1_w_out, mix_norm_g, w_in,
           b_forget, q_norm_g, k_norm_g, pool_w, pool_scale, w_out, ffn2_norm_g,
           ffn2_w_in, ffn2_w_out, final_norm_g):
    nb, seq, d = x.shape
    t = nb * seq
    n_heads = b_forget.shape[1]
    d_attn = n_heads * HEAD_DIM
    d_pool = pool_scale.shape[1]
    depth = w_ada.shape[0]
    xf = x.reshape(t, d)

    c_pad = jnp.pad(c, ((0, 8 - nb), (0, 0)))
    for l in range(depth):
        mod = _ada_call(c_pad, w_ada[l], b_ada[l].reshape(1, -1))[:nb]
        sh1, sc1, g1, sh2, sc2, g2, sh3, sc3, g3 = [
            mod[:, i * d:(i + 1) * d].reshape(nb, 1, d) for i in range(9)]

        w1i, w1o = _pack_ffn_weights(ffn1_w_in[l], ffn1_w_out[l])
        xf = _ffn_call(xf, ffn1_norm_g[l].reshape(1, d), sh1, sc1, g1, w1i, w1o, None,
                       seq=seq)

        wl = w_in[l]
        f0 = 3 * d_attn
        w_main = jnp.concatenate([wl[:, :f0], wl[:, f0 + n_heads:]], axis=1).astype(BF16)
        w_f = jnp.pad(wl[:, f0:f0 + n_heads], ((0, 0), (0, LANES - n_heads))).astype(BF16)
        q_gain = (q_norm_g[l] * (HEAD_DIM ** -0.5 * LOG2E)).reshape(1, HEAD_DIM)
        k_gain = k_norm_g[l].reshape(1, HEAD_DIM)
        qkv, u, f_logit = _inproj_call(xf, mix_norm_g[l].reshape(1, d), sh2, sc2, w_main,
                                       w_f, q_gain, k_gain, seq=seq, d_attn=d_attn)
        b_pad = jnp.pad(b_forget[l], (0, LANES - n_heads)).reshape(1, LANES)
        qe, ke = _forget_call(f_logit.reshape(nb, seq, LANES), b_pad, n_heads)
        attn = _attn_call(qkv, qe, ke, nb=nb, seq=seq, n_heads=n_heads)
        pool = _pool_call(u.reshape(nb, seq, d_pool), pool_w[l], pool_scale[l].reshape(1, -1))
        wo = w_out[l].astype(BF16)
        xf = _outproj_call(xf, attn, pool.reshape(t, d_pool), wo[:d_attn], wo[d_attn:], g2,
                           seq=seq)

        w2i, w2o = _pack_ffn_weights(ffn2_w_in[l], ffn2_w_out[l])
        last = l == depth - 1
        xf = _ffn_call(xf, ffn2_norm_g[l].reshape(1, d), sh3, sc3, g3, w2i, w2o,
                       final_norm_g.reshape(1, d) if last else None, seq=seq)
    return xf.reshape(nb, seq, d)
```

```python
import functools

import jax
import jax.numpy as jnp
from jax import lax
from jax.experimental import pallas as pl
from jax.experimental.pallas import tpu as pltpu

F32 = jnp.float32
BF16 = jnp.bfloat16

EPS = 1e-6
HEAD_DIM = 128
POOL_WINDOWS = (2, 4, 8, 16)
LANES = 128
MXU_DIM = 256
MIB = 1024 * 1024
LOG2E = 1.4426950408889634


def _cparams(dims, vmem_mib):
    return pltpu.CompilerParams(dimension_semantics=dims,
                                vmem_limit_bytes=vmem_mib * MIB)


def _norm_mod(x, g, shift, scale):
    ms = jnp.mean(x * x, axis=-1, keepdims=True)
    xn = x * lax.rsqrt(ms + EPS) * g
    return xn * (1.0 + scale) + shift


def _ada_kernel(c_ref, w_ref, b_ref, o_ref):
    c = c_ref[...]
    ca = c * (1.0 / (1.0 + jnp.exp(-c)))
    o_ref[...] = jnp.dot(ca.astype(BF16), w_ref[...].astype(BF16),
                         preferred_element_type=F32) + b_ref[...]


def _ada_call(c_pad, w, b, tn=1024):
    m, d = c_pad.shape
    n = w.shape[1]
    return pl.pallas_call(
        _ada_kernel,
        grid=(n // tn,),
        in_specs=[pl.BlockSpec((m, d), lambda j: (0, 0)),
                  pl.BlockSpec((d, tn), lambda j: (0, j)),
                  pl.BlockSpec((1, tn), lambda j: (0, j))],
        out_specs=pl.BlockSpec((m, tn), lambda j: (0, j)),
        out_shape=jax.ShapeDtypeStruct((m, n), F32),
        compiler_params=_cparams(("arbitrary",), 40),
        name="ada_mod",
    )(c_pad, w, b)


def _ffn_kernel(x_ref, g_ref, sh_ref, sc_ref, gate_ref, win_ref, wout_ref,
                *rest, nj, tf, final_norm):
    if final_norm:
        fg_ref, o_ref, h_scr = rest
    else:
        o_ref, h_scr = rest
    j = pl.program_id(1)

    @pl.when(j == 0)
    def _():
        h = _norm_mod(x_ref[...], g_ref[...], sh_ref[0], sc_ref[0])
        h_scr[...] = h.astype(BF16)
        o_ref[...] = jnp.zeros_like(o_ref)

    hw = jnp.dot(h_scr[...], win_ref[...], preferred_element_type=F32)
    cps = win_ref.shape[1] // (2 * tf)
    a = jnp.concatenate([hw[:, 2 * c * tf:(2 * c + 1) * tf] for c in range(cps)], axis=1)
    b = jnp.concatenate([hw[:, (2 * c + 1) * tf:(2 * c + 2) * tf] for c in range(cps)],
                        axis=1)
    act = (a * (1.0 / (1.0 + jnp.exp(-a))) * b).astype(BF16)
    o_ref[...] += jnp.dot(act, wout_ref[...], preferred_element_type=F32)

    @pl.when(j == nj - 1)
    def _():
        y = x_ref[...] + 0.5 * gate_ref[0] * o_ref[...]
        if final_norm:
            ms = jnp.mean(y * y, axis=-1, keepdims=True)
            y = y * lax.rsqrt(ms + EPS) * fg_ref[...]
        o_ref[...] = y


def _ffn_call(x, g, shift, scale, gate, w_in_p, w_out_p, final_g, *, seq, tm=512,
              tf=MXU_DIM, cps=2):
    t, d = x.shape
    nj = w_out_p.shape[0] // (cps * tf)
    tiles_per_seq = seq // tm
    mod_spec = pl.BlockSpec((1, 1, d), lambda i, j: (i // tiles_per_seq, 0, 0))
    vec_spec = pl.BlockSpec((1, d), lambda i, j: (0, 0))
    in_specs = [pl.BlockSpec((tm, d), lambda i, j: (i, 0)),
                vec_spec, mod_spec, mod_spec, mod_spec,
                pl.BlockSpec((d, cps * 2 * tf), lambda i, j: (0, j)),
                pl.BlockSpec((cps * tf, d), lambda i, j: (j, 0))]
    args = [x, g, shift, scale, gate, w_in_p, w_out_p]
    if final_g is not None:
        in_specs.append(vec_spec)
        args.append(final_g)
    return pl.pallas_call(
        functools.partial(_ffn_kernel, nj=nj, tf=tf, final_norm=final_g is not None),
        grid=(t // tm, nj),
        in_specs=in_specs,
        out_specs=pl.BlockSpec((tm, d), lambda i, j: (i, 0)),
        out_shape=jax.ShapeDtypeStruct((t, d), F32),
        scratch_shapes=[pltpu.VMEM((tm, d), BF16)],
        compiler_params=_cparams(("parallel", "arbitrary"), 48),
        name="ffn_final" if final_g is not None else "ffn",
    )(*args)


def _pack_in_kernel(a0_ref, a1_ref, b0_ref, b1_ref, o_ref, *, n_full):
    keep = pl.program_id(0) < n_full
    o_ref[:, 0 * LANES:1 * LANES] = a0_ref[...].astype(BF16)
    o_ref[:, 1 * LANES:2 * LANES] = jnp.where(keep, a1_ref[...], 0.0).astype(BF16)
    o_ref[:, 2 * LANES:3 * LANES] = b0_ref[...].astype(BF16)
    o_ref[:, 3 * LANES:4 * LANES] = jnp.where(keep, b1_ref[...], 0.0).astype(BF16)


def _pack_out_kernel(r0_ref, r1_ref, o_ref, *, n_full):
    keep = pl.program_id(0) < n_full
    o_ref[0 * LANES:1 * LANES, :] = r0_ref[...].astype(BF16)
    o_ref[1 * LANES:2 * LANES, :] = jnp.where(keep, r1_ref[...], 0.0).astype(BF16)


def _pack_ffn_weights(w_in, w_out):
    d, two_ff = w_in.shape
    dff = two_ff // 2
    nblk = dff // LANES
    nj = -(-nblk // 2)
    n_full = nblk // 2
    second = lambda j: jnp.minimum(2 * j + 1, nblk - 1)
    col = lambda f: pl.BlockSpec((d, LANES), lambda j: (0, f(j)))
    w_in_p = pl.pallas_call(
        functools.partial(_pack_in_kernel, n_full=n_full),
        grid=(nj,),
        in_specs=[col(lambda j: 2 * j), col(second),
                  col(lambda j: nblk + 2 * j), col(lambda j: nblk + second(j))],
        out_specs=pl.BlockSpec((d, 4 * LANES), lambda j: (0, j)),
        out_shape=jax.ShapeDtypeStruct((d, nj * 4 * LANES), BF16),
        compiler_params=_cparams(("parallel",), 32),
        name="pack_w_in",
    )(w_in, w_in, w_in, w_in)
    row = lambda f: pl.BlockSpec((LANES, d), lambda j: (f(j), 0))
    w_out_p = pl.pallas_call(
        functools.partial(_pack_out_kernel, n_full=n_full),
        grid=(nj,),
        in_specs=[row(lambda j: 2 * j), row(second)],
        out_specs=pl.BlockSpec((2 * LANES, d), lambda j: (j, 0)),
        out_shape=jax.ShapeDtypeStruct((nj * 2 * LANES, d), BF16),
        compiler_params=_cparams(("parallel",), 32),
        name="pack_w_out",
    )(w_out, w_out)
    return w_in_p, w_out_p


def _inproj_kernel(x_ref, g_ref, sh_ref, sc_ref, w_ref, wf_ref, qg_ref, kg_ref,
                   qkv_ref, u_ref, f_ref, h_scr):
    j = pl.program_id(1)
    tn = w_ref.shape[1]

    def chunks():
        for s in range(tn // MXU_DIM):
            cols = slice(s * MXU_DIM, (s + 1) * MXU_DIM)
            yield cols, jnp.dot(h_scr[...], w_ref[:, cols], preferred_element_type=F32)

    def qk_step(gain_ref):
        gain = gain_ref[...]
        for cols, acc in chunks():
            for hh in range(MXU_DIM // HEAD_DIM):
                xh = acc[:, hh * HEAD_DIM:(hh + 1) * HEAD_DIM]
                ms = jnp.mean(xh * xh, axis=-1, keepdims=True)
                lo = cols.start + hh * HEAD_DIM
                qkv_ref[:, lo:lo + HEAD_DIM] = (
                    xh * lax.rsqrt(ms + EPS) * gain).astype(BF16)

    @pl.when(j == 0)
    def _():
        h = _norm_mod(x_ref[...], g_ref[...], sh_ref[0], sc_ref[0])
        h_scr[...] = h.astype(BF16)
        f_ref[...] = jnp.dot(h_scr[...], wf_ref[...], preferred_element_type=F32)
        qk_step(qg_ref)

    @pl.when(j == 1)
    def _():
        qk_step(kg_ref)

    @pl.when(j == 2)
    def _():
        for cols, acc in chunks():
            qkv_ref[:, cols] = acc.astype(BF16)

    @pl.when(j == 3)
    def _():
        for cols, acc in chunks():
            u_ref[:, cols] = acc


def _inproj_call(x, g, shift, scale, w_main, w_f, q_gain, k_gain, *, seq, d_attn,
                 tm=1024):
    t, d = x.shape
    tn = d_attn
    n_main = w_main.shape[1]
    nj = n_main // tn
    n_qkv = 3
    d_pool = n_main - 3 * d_attn
    assert nj == 4 and d_pool == tn
    tiles_per_seq = seq // tm
    mod_spec = pl.BlockSpec((1, 1, d), lambda i, j: (i // tiles_per_seq, 0, 0))
    return pl.pallas_call(
        _inproj_kernel,
        grid=(t // tm, nj),
        in_specs=[pl.BlockSpec((tm, d), lambda i, j: (i, 0)),
                  pl.BlockSpec((1, d), lambda i, j: (0, 0)),
                  mod_spec, mod_spec,
                  pl.BlockSpec((d, tn), lambda i, j: (0, j)),
                  pl.BlockSpec((d, LANES), lambda i, j: (0, 0)),
                  pl.BlockSpec((1, HEAD_DIM), lambda i, j: (0, 0)),
                  pl.BlockSpec((1, HEAD_DIM), lambda i, j: (0, 0))],
        out_specs=[
            pl.BlockSpec((tm, tn), lambda i, j: (i, jnp.minimum(j, n_qkv - 1))),
            pl.BlockSpec((tm, tn), lambda i, j: (i, jnp.maximum(j - n_qkv, 0))),
            pl.BlockSpec((tm, LANES), lambda i, j: (i, 0))],
        out_shape=[jax.ShapeDtypeStruct((t, 3 * d_attn), BF16),
                   jax.ShapeDtypeStruct((t, d_pool), F32),
                   jax.ShapeDtypeStruct((t, LANES), F32)],
        scratch_shapes=[pltpu.VMEM((tm, d), BF16)],
        compiler_params=_cparams(("parallel", "arbitrary"), 52),
        name="mixer_inproj",
    )(x, g, shift, scale, w_main, w_f, q_gain, k_gain)


N_SPLIT = 3


def _bias_selector(n_heads):
    rows = N_SPLIT * LANES
    hbits = n_heads.bit_length() - 1
    r = lax.broadcasted_iota(jnp.int32, (rows, 2 * LANES), 0)
    c = lax.broadcasted_iota(jnp.int32, (rows, 2 * LANES), 1)
    in_q = c < 2 * N_SPLIT * n_heads
    in_k = (c >= LANES) & (c < LANES + 2 * N_SPLIT * n_heads)
    ck = c - LANES
    slot_q = lax.shift_right_logical(c, hbits)
    slot_k = lax.shift_right_logical(jnp.maximum(ck, 0), hbits)
    head_q = c & (n_heads - 1)
    head_k = ck & (n_heads - 1)
    one_row = r == n_heads
    q_piece = in_q & (slot_q < N_SPLIT) & (r == slot_q * LANES + head_q)
    q_one = in_q & (slot_q >= N_SPLIT) & one_row
    k_one = in_k & (slot_k < N_SPLIT) & one_row
    k_piece = in_k & (slot_k >= N_SPLIT) & (r == (slot_k - N_SPLIT) * LANES + head_k)
    sel = jnp.where(q_piece | q_one | k_one, 1.0, 0.0) - jnp.where(k_piece, 1.0, 0.0)
    return sel.astype(BF16)


def _forget_kernel(f_ref, b_ref, qe_ref, ke_ref, pad_scr, *, seq, n_heads):
    z = f_ref[0] + b_ref[...]
    x = jnp.minimum(z, 0.0) - jnp.log1p(jnp.exp(-jnp.abs(z)))
    pad_scr[pl.ds(0, seq), :] = jnp.zeros((seq, LANES), F32)
    d = 1
    while d < seq:
        pad_scr[pl.ds(seq, seq), :] = x
        x = x + pad_scr[pl.ds(seq - d, seq), :]
        d *= 2
    rem = x * LOG2E
    lane = lax.broadcasted_iota(jnp.int32, (seq, LANES), 1)
    pieces = []
    for s in range(N_SPLIT):
        piece = rem.astype(BF16)
        rem = rem - piece.astype(F32)
        if s == 0:
            piece = jnp.where(lane == n_heads, jnp.ones_like(piece), piece)
        pieces.append(piece)
    e = jnp.dot(jnp.concatenate(pieces, axis=1), _bias_selector(n_heads),
                preferred_element_type=F32)
    qe_ref[0] = e[:, :LANES].astype(BF16)
    ke_ref[0] = e[:, LANES:].astype(BF16)


def _forget_call(f_logit, b_pad, n_heads):
    nb, seq, _ = f_logit.shape
    assert n_heads & (n_heads - 1) == 0 and 2 * N_SPLIT * n_heads <= LANES
    out = jax.ShapeDtypeStruct((nb, seq, LANES), BF16)
    spec = pl.BlockSpec((1, seq, LANES), lambda b: (b, 0, 0))
    return pl.pallas_call(
        functools.partial(_forget_kernel, seq=seq, n_heads=n_heads),
        grid=(nb,),
        in_specs=[spec, pl.BlockSpec((1, LANES), lambda b: (0, 0))],
        out_specs=[spec, spec],
        out_shape=[out, out],
        scratch_shapes=[pltpu.VMEM((2 * seq, LANES), F32)],
        compiler_params=_cparams(("parallel",), 40),
        name="forget_cumsum",
    )(f_logit, b_pad)


POOL_HALO = 16
POOL_ROWS = 512


def _pool_kernel(u_ref, w_ref, s_ref, o_ref, pad_scr, *, seq):
    g = pl.program_id(1)
    cg = u_ref.shape[2]
    pad_scr[pl.ds(0, POOL_HALO), :] = jnp.zeros((POOL_HALO, cg), F32)
    pad_scr[pl.ds(POOL_HALO, seq), :] = u_ref[0]
    w = w_ref[0].astype(BF16)
    scale = s_ref[...]
    for gi, win in enumerate(POOL_WINDOWS):
        @pl.when(g == gi)
        def _(win=win):
            for r0 in range(0, seq, POOL_ROWS):
                tok = pad_scr[pl.ds(POOL_HALO + r0, POOL_ROWS), :]
                tot = tok
                for dd in range(1, win):
                    tot = tot + pad_scr[pl.ds(POOL_HALO + r0 - dd, POOL_ROWS), :]
                pos = r0 + lax.broadcasted_iota(jnp.int32, (POOL_ROWS, cg), 0)
                cnt = jnp.minimum(pos + 1, win).astype(F32)
                p = tot / cnt - tok
                y = jnp.dot(p.astype(BF16), w, preferred_element_type=F32) * scale
                o_ref[0, pl.ds(r0, POOL_ROWS), :] = y.astype(BF16)


def _pool_call(u, pool_w, pool_scale):
    nb, seq, d_pool = u.shape
    ng, cg, _ = pool_w.shape
    return pl.pallas_call(
        functools.partial(_pool_kernel, seq=seq),
        grid=(nb, ng),
        in_specs=[pl.BlockSpec((1, seq, cg), lambda b, g: (b, 0, g)),
                  pl.BlockSpec((1, cg, cg), lambda b, g: (g, 0, 0)),
                  pl.BlockSpec((1, cg), lambda b, g: (0, g))],
        out_specs=pl.BlockSpec((1, seq, cg), lambda b, g: (b, 0, g)),
        out_shape=jax.ShapeDtypeStruct((nb, seq, d_pool), BF16),
        scratch_shapes=[pltpu.VMEM((POOL_HALO + seq, cg), F32)],
        compiler_params=_cparams(("parallel", "arbitrary"), 40),
        name="ms_pool",
    )(u, pool_w, pool_scale)


NEG_BIG = -1e30


def _lane_tile_reduce(x, op):
    out = x[:, :LANES]
    for t in range(1, x.shape[1] // LANES):
        out = op(out, x[:, t * LANES:(t + 1) * LANES])
    return out


def _attn_kernel(q_ref, qe_ref, k_ref, ke_ref, v_ref, o_ref, s_scr, p_scr, ke_scr,
                 *, tq, n_heads):
    seq = q_ref.shape[0]
    lane = lax.broadcasted_iota(jnp.int32, (seq, LANES), 1)
    mine = ((lane & (n_heads - 1)) == pl.program_id(1)) & (lane < 2 * N_SPLIT * n_heads)
    ke = ke_ref[0]
    ke_scr[...] = jnp.where(mine, ke, jnp.zeros_like(ke))
    for i in reversed(range(seq // tq)):
        slot = i % 2
        rows = pl.ds(i * tq, tq)
        qx = jnp.concatenate([q_ref[rows, :], qe_ref[0, rows, :]], axis=1)
        n_chunks = i + 1
        m_tile = None
        for c in range(n_chunks):
            cols = pl.ds(c * tq, tq)
            kx = jnp.concatenate([k_ref[cols, :], ke_scr[cols, :]], axis=1)
            s = lax.dot_general(qx, kx, (((1,), (1,)), ((), ())),
                                preferred_element_type=F32)
            if c == i:
                row = lax.broadcasted_iota(jnp.int32, (tq, tq), 0)
                col = lax.broadcasted_iota(jnp.int32, (tq, tq), 1)
                s = jnp.where(row >= col, s, NEG_BIG)
            s_scr[slot, :, cols] = s
            cm = _lane_tile_reduce(s, jnp.maximum)
            m_tile = cm if m_tile is None else jnp.maximum(m_tile, cm)
        m_row = jnp.max(m_tile, axis=-1, keepdims=True)
        l_tile = None
        for c in range(n_chunks):
            cols = pl.ds(c * tq, tq)
            p = jnp.exp2(s_scr[slot, :, cols] - m_row)
            p_scr[slot, :, cols] = p.astype(BF16)
            cl = _lane_tile_reduce(p, jnp.add)
            l_tile = cl if l_tile is None else l_tile + cl
        l_row = jnp.sum(l_tile, axis=-1, keepdims=True)
        kend = n_chunks * tq
        acc = jnp.dot(p_scr[slot, :, pl.ds(0, kend)], v_ref[pl.ds(0, kend), :],
                      preferred_element_type=F32)
        o_ref[rows, :] = (acc / l_row).astype(BF16)


def _attn_call(qkv, qe, ke, *, nb, seq, n_heads, tq=512):
    t = qkv.shape[0]
    d_attn = n_heads * HEAD_DIM
    head_spec = lambda off: pl.BlockSpec((seq, HEAD_DIM), lambda b, h: (b, off + h))
    bias_spec = pl.BlockSpec((1, seq, LANES), lambda b, h: (b, 0, 0))
    return pl.pallas_call(
        functools.partial(_attn_kernel, tq=tq, n_heads=n_heads),
        grid=(nb, n_heads),
        in_specs=[head_spec(0), bias_spec, head_spec(n_heads), bias_spec,
                  head_spec(2 * n_heads)],
        out_specs=head_spec(0),
        out_shape=jax.ShapeDtypeStruct((t, d_attn), BF16),
        scratch_shapes=[pltpu.VMEM((2, tq, seq), F32), pltpu.VMEM((2, tq, seq), BF16),
                        pltpu.VMEM((seq, LANES), BF16)],
        compiler_params=_cparams(("parallel", "parallel"), 56),
        name="fox_attention",
    )(qkv, qe, qkv, ke, qkv)


def _outproj_kernel(x_ref, attn_ref, pool_ref, wa_ref, wp_ref, gate_ref, o_ref):
    y = jnp.dot(attn_ref[...], wa_ref[...], preferred_element_type=F32)
    y = y + jnp.dot(pool_ref[...], wp_ref[...], preferred_element_type=F32)
    o_ref[...] = x_ref[...] + gate_ref[0] * y


def _outproj_call(x, attn, pool, w_attn, w_pool, gate, *, seq, tm=512):
    t, d = x.shape
    da = attn.shape[1]
    dp = pool.shape[1]
    tiles_per_seq = seq // tm
    return pl.pallas_call(
        _outproj_kernel,
        grid=(t // tm,),
        in_specs=[pl.BlockSpec((tm, d), lambda i: (i, 0)),
                  pl.BlockSpec((tm, da), lambda i: (i, 0)),
                  pl.BlockSpec((tm, dp), lambda i: (i, 0)),
                  pl.BlockSpec((da, d), lambda i: (0, 0)),
                  pl.BlockSpec((dp, d), lambda i: (0, 0)),
                  pl.BlockSpec((1, 1, d), lambda i: (i // tiles_per_seq, 0, 0))],
        out_specs=pl.BlockSpec((tm, d), lambda i: (i, 0)),
        out_shape=jax.ShapeDtypeStruct((t, d), F32),
        compiler_params=_cparams(("parallel",), 48),
        name="mixer_outproj",
    )(x, attn, pool, w_attn, w_pool, gate)


def kernel(x, c, w_ada, b_ada, ffn1_norm_g, ffn1_w_in, ffn1_w_out, mix_norm_g, w_in,
           b_forget, q_norm_g, k_norm_g, pool_w, pool_scale, w_out, ffn2_norm_g,
           ffn2_w_in, ffn2_w_out, final_norm_g):
    nb, seq, d = x.shape
    t = nb * seq
    n_heads = b_forget.shape[1]
    d_attn = n_heads * HEAD_DIM
    d_pool = pool_scale.shape[1]
    depth = w_ada.shape[0]
    xf = x.reshape(t, d)

    c_pad = jnp.pad(c, ((0, 8 - nb), (0, 0)))
    for l in range(depth):
        mod = _ada_call(c_pad, w_ada[l], b_ada[l].reshape(1, -1))[:nb]
        sh1, sc1, g1, sh2, sc2, g2, sh3, sc3, g3 = [
            mod[:, i * d:(i + 1) * d].reshape(nb, 1, d) for i in range(9)]

        w1i, w1o = _pack_ffn_weights(ffn1_w_in[l], ffn1_w_out[l])
        xf = _ffn_call(xf, ffn1_norm_g[l].reshape(1, d), sh1, sc1, g1, w1i, w1o, None,
                       seq=seq)

        wl = w_in[l]
        f0 = 3 * d_attn
        w_main = jnp.concatenate([wl[:, :f0], wl[:, f0 + n_heads:]], axis=1).astype(BF16)
        w_f = jnp.pad(wl[:, f0:f0 + n_heads], ((0, 0), (0, LANES - n_heads))).astype(BF16)
        q_gain = (q_norm_g[l] * (HEAD_DIM ** -0.5 * LOG2E)).reshape(1, HEAD_DIM)
        k_gain = k_norm_g[l].reshape(1, HEAD_DIM)
        qkv, u, f_logit = _inproj_call(xf, mix_norm_g[l].reshape(1, d), sh2, sc2, w_main,
                                       w_f, q_gain, k_gain, seq=seq, d_attn=d_attn)
        b_pad = jnp.pad(b_forget[l], (0, LANES - n_heads)).reshape(1, LANES)
        qe, ke = _forget_call(f_logit.reshape(nb, seq, LANES), b_pad, n_heads)
        attn = _attn_call(qkv, qe, ke, nb=nb, seq=seq, n_heads=n_heads)
        pool = _pool_call(u.reshape(nb, seq, d_pool), pool_w[l], pool_scale[l].reshape(1, -1))
        wo = w_out[l].astype(BF16)
        xf = _outproj_call(xf, attn, pool.reshape(t, d_pool), wo[:d_attn], wo[d_attn:], g2,
                           seq=seq)

        w2i, w2o = _pack_ffn_weights(ffn2_w_in[l], ffn2_w_out[l])
        last = l == depth - 1
        xf = _ffn_call(xf, ffn2_norm_g[l].reshape(1, d), sh3, sc3, g3, w2i, w2o,
                       final_norm_g.reshape(1, d) if last else None, seq=seq)
    return xf.reshape(nb, seq, d)
```

```python
import functools

import jax
import jax.numpy as jnp
from jax import lax
from jax.experimental import pallas as pl
from jax.experimental.pallas import tpu as pltpu

F32 = jnp.float32
BF16 = jnp.bfloat16

EPS = 1e-6
HEAD_DIM = 128
POOL_WINDOWS = (2, 4, 8, 16)
LANES = 128
MXU_DIM = 256
MIB = 1024 * 1024
LOG2E = 1.4426950408889634


def _cparams(dims, vmem_mib):
    return pltpu.CompilerParams(dimension_semantics=dims,
                                vmem_limit_bytes=vmem_mib * MIB)


def _norm_mod(x, g, shift, scale):
    ms = jnp.mean(x * x, axis=-1, keepdims=True)
    xn = x * lax.rsqrt(ms + EPS) * g
    return xn * (1.0 + scale) + shift


def _ada_kernel(c_ref, w_ref, b_ref, o_ref):
    c = c_ref[...]
    ca = c * (1.0 / (1.0 + jnp.exp(-c)))
    o_ref[...] = jnp.dot(ca.astype(BF16), w_ref[...].astype(BF16),
                         preferred_element_type=F32) + b_ref[...]


def _ada_call(c_pad, w, b, tn=1024):
    m, d = c_pad.shape
    n = w.shape[1]
    return pl.pallas_call(
        _ada_kernel,
        grid=(n // tn,),
        in_specs=[pl.BlockSpec((m, d), lambda j: (0, 0)),
                  pl.BlockSpec((d, tn), lambda j: (0, j)),
                  pl.BlockSpec((1, tn), lambda j: (0, j))],
        out_specs=pl.BlockSpec((m, tn), lambda j: (0, j)),
        out_shape=jax.ShapeDtypeStruct((m, n), F32),
        compiler_params=_cparams(("arbitrary",), 40),
        name="ada_mod",
    )(c_pad, w, b)


def _ffn_kernel(x_ref, g_ref, sh_ref, sc_ref, gate_ref, win_ref, wout_ref,
                *rest, nj, final_norm):
    if final_norm:
        fg_ref, o_ref, h_scr = rest
    else:
        o_ref, h_scr = rest
    j = pl.program_id(1)

    @pl.when(j == 0)
    def _():
        h = _norm_mod(x_ref[...], g_ref[...], sh_ref[0], sc_ref[0])
        h_scr[...] = h.astype(BF16)
        o_ref[...] = jnp.zeros_like(o_ref)

    hw = jnp.dot(h_scr[...], win_ref[...], preferred_element_type=F32)
    tiles = [hw[:, k * LANES:(k + 1) * LANES] for k in range(win_ref.shape[1] // LANES)]
    a = jnp.concatenate(tiles[0::2], axis=1)
    b = jnp.concatenate(tiles[1::2], axis=1)
    act = (a * (1.0 / (1.0 + jnp.exp(-a))) * b).astype(BF16)
    o_ref[...] += jnp.dot(act, wout_ref[...], preferred_element_type=F32)

    @pl.when(j == nj - 1)
    def _():
        y = x_ref[...] + 0.5 * gate_ref[0] * o_ref[...]
        if final_norm:
            ms = jnp.mean(y * y, axis=-1, keepdims=True)
            y = y * lax.rsqrt(ms + EPS) * fg_ref[...]
        o_ref[...] = y


FFN_BLOCKS_PER_STEP = 4


def _ffn_call(x, g, shift, scale, gate, w_in_p, w_out_p, final_g, *, seq, tm=512):
    t, d = x.shape
    bps = FFN_BLOCKS_PER_STEP
    nj = w_out_p.shape[0] // (bps * LANES)
    tiles_per_seq = seq // tm
    mod_spec = pl.BlockSpec((1, 1, d), lambda i, j: (i // tiles_per_seq, 0, 0))
    vec_spec = pl.BlockSpec((1, d), lambda i, j: (0, 0))
    in_specs = [pl.BlockSpec((tm, d), lambda i, j: (i, 0)),
                vec_spec, mod_spec, mod_spec, mod_spec,
                pl.BlockSpec((d, bps * 2 * LANES), lambda i, j: (0, j)),
                pl.BlockSpec((bps * LANES, d), lambda i, j: (j, 0))]
    args = [x, g, shift, scale, gate, w_in_p, w_out_p]
    if final_g is not None:
        in_specs.append(vec_spec)
        args.append(final_g)
    return pl.pallas_call(
        functools.partial(_ffn_kernel, nj=nj, final_norm=final_g is not None),
        grid=(t // tm, nj),
        in_specs=in_specs,
        out_specs=pl.BlockSpec((tm, d), lambda i, j: (i, 0)),
        out_shape=jax.ShapeDtypeStruct((t, d), F32),
        scratch_shapes=[pltpu.VMEM((tm, d), BF16)],
        compiler_params=_cparams(("parallel", "arbitrary"), 48),
        name="ffn_final" if final_g is not None else "ffn",
    )(*args)


def _pack_block(keep, a_ref, b_ref, r_ref, wi_ref, wo_ref):
    wi_ref[:, :LANES] = jnp.where(keep, a_ref[...], 0.0).astype(BF16)
    wi_ref[:, LANES:] = jnp.where(keep, b_ref[...], 0.0).astype(BF16)
    wo_ref[...] = jnp.where(keep, r_ref[...], 0.0).astype(BF16)


def _pack_geometry(w_in, w_out):
    d = w_in.shape[0]
    nblk = w_out.shape[0] // LANES
    nblk_pad = -(-nblk // FFN_BLOCKS_PER_STEP) * FFN_BLOCKS_PER_STEP
    return d, nblk, nblk_pad


def _pack_specs(w_in, w_out, step):
    d, nblk, nblk_pad = _pack_geometry(w_in, w_out)
    src = lambda *g: jnp.minimum(step(*g), nblk - 1)
    dst = lambda *g: jnp.minimum(step(*g), nblk_pad - 1)
    in_specs = [pl.BlockSpec((d, LANES), lambda *g: (0, src(*g))),
                pl.BlockSpec((d, LANES), lambda *g: (0, nblk + src(*g))),
                pl.BlockSpec((LANES, d), lambda *g: (src(*g), 0))]
    out_specs = [pl.BlockSpec((d, 2 * LANES), lambda *g: (0, dst(*g))),
                 pl.BlockSpec((LANES, d), lambda *g: (dst(*g), 0))]
    out_shape = [jax.ShapeDtypeStruct((d, nblk_pad * 2 * LANES), BF16),
                 jax.ShapeDtypeStruct((nblk_pad * LANES, d), BF16)]
    return in_specs, out_specs, out_shape


def _pack_kernel(a_ref, b_ref, r_ref, wi_ref, wo_ref, *, nblk):
    _pack_block(pl.program_id(0) < nblk, a_ref, b_ref, r_ref, wi_ref, wo_ref)


def _pack_ffn_weights(w_in, w_out):
    _, nblk, nblk_pad = _pack_geometry(w_in, w_out)
    in_specs, out_specs, out_shape = _pack_specs(w_in, w_out, lambda s: s)
    return pl.pallas_call(
        functools.partial(_pack_kernel, nblk=nblk),
        grid=(nblk_pad,),
        in_specs=in_specs, out_specs=out_specs, out_shape=out_shape,
        compiler_params=_cparams(("parallel",), 32),
        name="pack_ffn_weights",
    )(w_in, w_in, w_out)


def _inproj_kernel(x_ref, g_ref, sh_ref, sc_ref, w_ref, wu_ref, wf_ref, qg_ref, kg_ref,
                   pa_ref, pb_ref, pr_ref, qkv_ref, u_ref, f_ref, wi_ref, wo_ref, h_scr,
                   *, nblk, nblk_pad):
    i = pl.program_id(0)
    j = pl.program_id(1)
    tn = w_ref.shape[1]

    def chunks(src_ref):
        for s in range(tn // MXU_DIM):
            cols = slice(s * MXU_DIM, (s + 1) * MXU_DIM)
            yield cols, jnp.dot(h_scr[...], src_ref[:, cols], preferred_element_type=F32)

    def qk_step(gain_ref):
        gain = gain_ref[...]
        for cols, acc in chunks(w_ref):
            for hh in range(MXU_DIM // HEAD_DIM):
                xh = acc[:, hh * HEAD_DIM:(hh + 1) * HEAD_DIM]
                ms = jnp.mean(xh * xh, axis=-1, keepdims=True)
                lo = cols.start + hh * HEAD_DIM
                qkv_ref[:, lo:lo + HEAD_DIM] = (
                    xh * lax.rsqrt(ms + EPS) * gain).astype(BF16)

    @pl.when(j == 0)
    def _():
        h = _norm_mod(x_ref[...], g_ref[...], sh_ref[0], sc_ref[0])
        h_scr[...] = h.astype(BF16)
        f_ref[...] = jnp.dot(h_scr[...], wf_ref[...], preferred_element_type=F32)
        qk_step(qg_ref)

    @pl.when(j == 1)
    def _():
        qk_step(kg_ref)

    @pl.when(j == 2)
    def _():
        for cols, acc in chunks(w_ref):
            qkv_ref[:, cols] = acc.astype(BF16)

    @pl.when(j == 3)
    def _():
        for cols, acc in chunks(wu_ref):
            u_ref[:, cols] = acc

    step = i * pl.num_programs(1) + j

    @pl.when(step < nblk_pad)
    def _():
        _pack_block(step < nblk, pa_ref, pb_ref, pr_ref, wi_ref, wo_ref)


def _inproj_call(x, g, shift, scale, w_qkv, w_u, w_f, q_gain, k_gain, next_w_in,
                 next_w_out, *, seq, d_attn, tm=512):
    t, d = x.shape
    tn = d_attn
    n_qkv = w_qkv.shape[1] // tn
    nj = n_qkv + 1
    d_pool = w_u.shape[1]
    assert n_qkv == 3 and d_pool == tn
    tiles_per_seq = seq // tm
    mod_spec = pl.BlockSpec((1, 1, d), lambda i, j: (i // tiles_per_seq, 0, 0))
    _, nblk, nblk_pad = _pack_geometry(next_w_in, next_w_out)
    assert (t // tm) * nj >= nblk_pad
    p_in, p_out, p_shape = _pack_specs(next_w_in, next_w_out, lambda i, j: i * nj + j)
    return pl.pallas_call(
        functools.partial(_inproj_kernel, nblk=nblk, nblk_pad=nblk_pad),
        grid=(t // tm, nj),
        in_specs=[pl.BlockSpec((tm, d), lambda i, j: (i, 0)),
                  pl.BlockSpec((1, d), lambda i, j: (0, 0)),
                  mod_spec, mod_spec,
                  pl.BlockSpec((d, tn), lambda i, j: (0, jnp.minimum(j, n_qkv - 1))),
                  pl.BlockSpec((d, tn), lambda i, j: (0, 0)),
                  pl.BlockSpec((d, LANES), lambda i, j: (0, 0)),
                  pl.BlockSpec((1, HEAD_DIM), lambda i, j: (0, 0)),
                  pl.BlockSpec((1, HEAD_DIM), lambda i, j: (0, 0))] + p_in,
        out_specs=[
            pl.BlockSpec((tm, tn), lambda i, j: (i, jnp.minimum(j, n_qkv - 1))),
            pl.BlockSpec((tm, tn), lambda i, j: (i, 0)),
            pl.BlockSpec((tm, LANES), lambda i, j: (i, 0))] + p_out,
        out_shape=[jax.ShapeDtypeStruct((t, n_qkv * d_attn), BF16),
                   jax.ShapeDtypeStruct((t, d_pool), F32),
                   jax.ShapeDtypeStruct((t, LANES), F32)] + p_shape,
        scratch_shapes=[pltpu.VMEM((tm, d), BF16)],
        compiler_params=_cparams(("arbitrary", "arbitrary"), 52),
        name="mixer_inproj",
    )(x, g, shift, scale, w_qkv, w_u, w_f, q_gain, k_gain, next_w_in, next_w_in,
      next_w_out)


N_SPLIT = 3


def _bias_selector(n_heads):
    rows = N_SPLIT * LANES
    hbits = n_heads.bit_length() - 1
    r = lax.broadcasted_iota(jnp.int32, (rows, 2 * LANES), 0)
    c = lax.broadcasted_iota(jnp.int32, (rows, 2 * LANES), 1)
    in_q = c < 2 * N_SPLIT * n_heads
    in_k = (c >= LANES) & (c < LANES + 2 * N_SPLIT * n_heads)
    ck = c - LANES
    slot_q = lax.shift_right_logical(c, hbits)
    slot_k = lax.shift_right_logical(jnp.maximum(ck, 0), hbits)
    head_q = c & (n_heads - 1)
    head_k = ck & (n_heads - 1)
    one_row = r == n_heads
    q_piece = in_q & (slot_q < N_SPLIT) & (r == slot_q * LANES + head_q)
    q_one = in_q & (slot_q >= N_SPLIT) & one_row
    k_one = in_k & (slot_k < N_SPLIT) & one_row
    k_piece = in_k & (slot_k >= N_SPLIT) & (r == (slot_k - N_SPLIT) * LANES + head_k)
    sel = jnp.where(q_piece | q_one | k_one, 1.0, 0.0) - jnp.where(k_piece, 1.0, 0.0)
    return sel.astype(BF16)


def _forget_kernel(f_ref, b_ref, qe_ref, ke_ref, pad_scr, *, seq, n_heads):
    z = f_ref[0] + b_ref[...]
    x = jnp.minimum(z, 0.0) - jnp.log1p(jnp.exp(-jnp.abs(z)))
    pad_scr[pl.ds(0, seq), :] = jnp.zeros((seq, LANES), F32)
    d = 1
    while d < seq:
        pad_scr[pl.ds(seq, seq), :] = x
        x = x + pad_scr[pl.ds(seq - d, seq), :]
        d *= 2
    rem = x * LOG2E
    lane = lax.broadcasted_iota(jnp.int32, (seq, LANES), 1)
    pieces = []
    for s in range(N_SPLIT):
        piece = rem.astype(BF16)
        rem = rem - piece.astype(F32)
        if s == 0:
            piece = jnp.where(lane == n_heads, jnp.ones_like(piece), piece)
        pieces.append(piece)
    e = jnp.dot(jnp.concatenate(pieces, axis=1), _bias_selector(n_heads),
                preferred_element_type=F32)
    qe_ref[0] = e[:, :LANES].astype(BF16)
    ke_ref[0] = e[:, LANES:].astype(BF16)


def _forget_call(f_logit, b_pad, n_heads):
    nb, seq, _ = f_logit.shape
    assert n_heads & (n_heads - 1) == 0 and 2 * N_SPLIT * n_heads <= LANES
    out = jax.ShapeDtypeStruct((nb, seq, LANES), BF16)
    spec = pl.BlockSpec((1, seq, LANES), lambda b: (b, 0, 0))
    return pl.pallas_call(
        functools.partial(_forget_kernel, seq=seq, n_heads=n_heads),
        grid=(nb,),
        in_specs=[spec, pl.BlockSpec((1, LANES), lambda b: (0, 0))],
        out_specs=[spec, spec],
        out_shape=[out, out],
        scratch_shapes=[pltpu.VMEM((2 * seq, LANES), F32)],
        compiler_params=_cparams(("parallel",), 40),
        name="forget_cumsum",
    )(f_logit, b_pad)


POOL_HALO = 16
POOL_ROWS = 512


def _pool_kernel(u_ref, w_ref, s_ref, o_ref, pad_scr, *, seq):
    g = pl.program_id(1)
    cg = u_ref.shape[2]
    pad_scr[pl.ds(0, POOL_HALO), :] = jnp.zeros((POOL_HALO, cg), F32)
    pad_scr[pl.ds(POOL_HALO, seq), :] = u_ref[0]
    w = w_ref[0].astype(BF16)
    scale = s_ref[...]
    for gi, win in enumerate(POOL_WINDOWS):
        @pl.when(g == gi)
        def _(win=win):
            for r0 in range(0, seq, POOL_ROWS):
                tok = pad_scr[pl.ds(POOL_HALO + r0, POOL_ROWS), :]
                tot = tok
                for dd in range(1, win):
                    tot = tot + pad_scr[pl.ds(POOL_HALO + r0 - dd, POOL_ROWS), :]
                pos = r0 + lax.broadcasted_iota(jnp.int32, (POOL_ROWS, cg), 0)
                cnt = jnp.minimum(pos + 1, win).astype(F32)
                p = tot / cnt - tok
                y = jnp.dot(p.astype(BF16), w, preferred_element_type=F32) * scale
                o_ref[0, pl.ds(r0, POOL_ROWS), :] = y.astype(BF16)


def _pool_call(u, pool_w, pool_scale):
    nb, seq, d_pool = u.shape
    ng, cg, _ = pool_w.shape
    return pl.pallas_call(
        functools.partial(_pool_kernel, seq=seq),
        grid=(nb, ng),
        in_specs=[pl.BlockSpec((1, seq, cg), lambda b, g: (b, 0, g)),
                  pl.BlockSpec((1, cg, cg), lambda b, g: (g, 0, 0)),
                  pl.BlockSpec((1, cg), lambda b, g: (0, g))],
        out_specs=pl.BlockSpec((1, seq, cg), lambda b, g: (b, 0, g)),
        out_shape=jax.ShapeDtypeStruct((nb, seq, d_pool), BF16),
        scratch_shapes=[pltpu.VMEM((POOL_HALO + seq, cg), F32)],
        compiler_params=_cparams(("parallel", "arbitrary"), 40),
        name="ms_pool",
    )(u, pool_w, pool_scale)


NEG_BIG = -1e30


def _lane_tile_reduce(x, op):
    out = x[:, :LANES]
    for t in range(1, x.shape[1] // LANES):
        out = op(out, x[:, t * LANES:(t + 1) * LANES])
    return out


def _attn_kernel(q_ref, qe_ref, k_ref, ke_ref, v_ref, o_ref, s_scr, p_scr, ke_scr,
                 *, tq, n_heads):
    seq = q_ref.shape[0]
    lane = lax.broadcasted_iota(jnp.int32, (seq, LANES), 1)
    mine = ((lane & (n_heads - 1)) == pl.program_id(1)) & (lane < 2 * N_SPLIT * n_heads)
    ke = ke_ref[0]
    ke_scr[...] = jnp.where(mine, ke, jnp.zeros_like(ke))
    for i in reversed(range(seq // tq)):
        slot = i % 2
        rows = pl.ds(i * tq, tq)
        qx = jnp.concatenate([q_ref[rows, :], qe_ref[0, rows, :]], axis=1)
        n_chunks = i + 1
        m_tile = None
        for c in range(n_chunks):
            cols = pl.ds(c * tq, tq)
            kx = jnp.concatenate([k_ref[cols, :], ke_scr[cols, :]], axis=1)
            s = lax.dot_general(qx, kx, (((1,), (1,)), ((), ())),
                                preferred_element_type=F32)
            if c == i:
                row = lax.broadcasted_iota(jnp.int32, (tq, tq), 0)
                col = lax.broadcasted_iota(jnp.int32, (tq, tq), 1)
                s = jnp.where(row >= col, s, NEG_BIG)
            s_scr[slot, :, cols] = s
            cm = _lane_tile_reduce(s, jnp.maximum)
            m_tile = cm if m_tile is None else jnp.maximum(m_tile, cm)
        m_row = jnp.max(m_tile, axis=-1, keepdims=True)
        l_tile = None
        for c in range(n_chunks):
            cols = pl.ds(c * tq, tq)
            p = jnp.exp2(s_scr[slot, :, cols] - m_row)
            p_scr[slot, :, cols] = p.astype(BF16)
            cl = _lane_tile_reduce(p, jnp.add)
            l_tile = cl if l_tile is None else l_tile + cl
        l_row = jnp.sum(l_tile, axis=-1, keepdims=True)
        kend = n_chunks * tq
        acc = jnp.dot(p_scr[slot, :, pl.ds(0, kend)], v_ref[pl.ds(0, kend), :],
                      preferred_element_type=F32)
        o_ref[rows, :] = (acc / l_row).astype(BF16)


def _attn_call(qkv, qe, ke, *, nb, seq, n_heads, tq=512):
    t = qkv.shape[0]
    d_attn = n_heads * HEAD_DIM
    head_spec = lambda off: pl.BlockSpec((seq, HEAD_DIM), lambda b, h: (b, off + h))
    bias_spec = pl.BlockSpec((1, seq, LANES), lambda b, h: (b, 0, 0))
    return pl.pallas_call(
        functools.partial(_attn_kernel, tq=tq, n_heads=n_heads),
        grid=(nb, n_heads),
        in_specs=[head_spec(0), bias_spec, head_spec(n_heads), bias_spec,
                  head_spec(2 * n_heads)],
        out_specs=head_spec(0),
        out_shape=jax.ShapeDtypeStruct((t, d_attn), BF16),
        scratch_shapes=[pltpu.VMEM((2, tq, seq), F32), pltpu.VMEM((2, tq, seq), BF16),
                        pltpu.VMEM((seq, LANES), BF16)],
        compiler_params=_cparams(("parallel", "parallel"), 56),
        name="fox_attention",
    )(qkv, qe, qkv, ke, qkv)


def _outproj_kernel(x_ref, attn_ref, pool_ref, wa_ref, wp_ref, gate_ref, o_ref, w_scr):
    da = attn_ref.shape[1]

    @pl.when(pl.program_id(0) == 0)
    def _():
        w_scr[:da, :] = wa_ref[...].astype(BF16)
        w_scr[da:, :] = wp_ref[...].astype(BF16)

    y = jnp.dot(attn_ref[...], w_scr[:da, :], preferred_element_type=F32)
    y = y + jnp.dot(pool_ref[...], w_scr[da:, :], preferred_element_type=F32)
    o_ref[...] = x_ref[...] + gate_ref[0] * y


def _outproj_call(x, attn, pool, w_out, gate, *, seq, tm=512):
    t, d = x.shape
    da = attn.shape[1]
    dp = pool.shape[1]
    assert da == dp and w_out.shape == (da + dp, d)
    tiles_per_seq = seq // tm
    w_spec = lambda half: pl.BlockSpec((da, d), lambda i: (half, 0),
                                       pipeline_mode=pl.Buffered(1))
    return pl.pallas_call(
        _outproj_kernel,
        grid=(t // tm,),
        in_specs=[pl.BlockSpec((tm, d), lambda i: (i, 0)),
                  pl.BlockSpec((tm, da), lambda i: (i, 0)),
                  pl.BlockSpec((tm, dp), lambda i: (i, 0)),
                  w_spec(0), w_spec(1),
                  pl.BlockSpec((1, 1, d), lambda i: (i // tiles_per_seq, 0, 0))],
        out_specs=pl.BlockSpec((tm, d), lambda i: (i, 0)),
        out_shape=jax.ShapeDtypeStruct((t, d), F32),
        scratch_shapes=[pltpu.VMEM((da + dp, d), BF16)],
        compiler_params=_cparams(("arbitrary",), 56),
        name="mixer_outproj",
    )(x, attn, pool, w_out, w_out, gate)


def kernel(x, c, w_ada, b_ada, ffn1_norm_g, ffn1_w_in, ffn1_w_out, mix_norm_g, w_in,
           b_forget, q_norm_g, k_norm_g, pool_w, pool_scale, w_out, ffn2_norm_g,
           ffn2_w_in, ffn2_w_out, final_norm_g):
    nb, seq, d = x.shape
    t = nb * seq
    n_heads = b_forget.shape[1]
    d_attn = n_heads * HEAD_DIM
    d_pool = pool_scale.shape[1]
    depth = w_ada.shape[0]
    xf = x.reshape(t, d)

    c_pad = jnp.pad(c, ((0, 8 - nb), (0, 0)))
    for l in range(depth):
        mod = _ada_call(c_pad, w_ada[l], b_ada[l].reshape(1, -1))[:nb]
        sh1, sc1, g1, sh2, sc2, g2, sh3, sc3, g3 = [
            mod[:, i * d:(i + 1) * d].reshape(nb, 1, d) for i in range(9)]

        w1i, w1o = _pack_ffn_weights(ffn1_w_in[l], ffn1_w_out[l])
        xf = _ffn_call(xf, ffn1_norm_g[l].reshape(1, d), sh1, sc1, g1, w1i, w1o, None,
                       seq=seq)

        wl = w_in[l]
        f0 = 3 * d_attn
        w_qkv = wl[:, :f0].astype(BF16)
        w_u = wl[:, f0 + n_heads:].astype(BF16)
        w_f = jnp.pad(wl[:, f0:f0 + n_heads], ((0, 0), (0, LANES - n_heads))).astype(BF16)
        q_gain = (q_norm_g[l] * (HEAD_DIM ** -0.5 * LOG2E)).reshape(1, HEAD_DIM)
        k_gain = k_norm_g[l].reshape(1, HEAD_DIM)
        qkv, u, f_logit, w2i, w2o = _inproj_call(
            xf, mix_norm_g[l].reshape(1, d), sh2, sc2, w_qkv, w_u, w_f, q_gain, k_gain,
            ffn2_w_in[l], ffn2_w_out[l], seq=seq, d_attn=d_attn)
        b_pad = jnp.pad(b_forget[l], (0, LANES - n_heads)).reshape(1, LANES)
        qe, ke = _forget_call(f_logit.reshape(nb, seq, LANES), b_pad, n_heads)
        attn = _attn_call(qkv, qe, ke, nb=nb, seq=seq, n_heads=n_heads)
        pool = _pool_call(u.reshape(nb, seq, d_pool), pool_w[l], pool_scale[l].reshape(1, -1))
        xf = _outproj_call(xf, attn, pool.reshape(t, d_pool), w_out[l], g2, seq=seq)

        last = l == depth - 1
        xf = _ffn_call(xf, ffn2_norm_g[l].reshape(1, d), sh3, sc3, g3, w2i, w2o,
                       final_norm_g.reshape(1, d) if last else None, seq=seq)
    return xf.reshape(nb, seq, d)
```

```python
import functools

import jax
import jax.numpy as jnp
from jax import lax
from jax.experimental import pallas as pl
from jax.experimental.pallas import tpu as pltpu

F32 = jnp.float32
BF16 = jnp.bfloat16

EPS = 1e-6
HEAD_DIM = 128
POOL_WINDOWS = (2, 4, 8, 16)
LANES = 128
MXU_DIM = 256
MIB = 1024 * 1024
LOG2E = 1.4426950408889634


def _cparams(dims, vmem_mib):
    return pltpu.CompilerParams(dimension_semantics=dims,
                                vmem_limit_bytes=vmem_mib * MIB)


def _norm_mod(x, g, shift, scale):
    ms = jnp.mean(x * x, axis=-1, keepdims=True)
    xn = x * lax.rsqrt(ms + EPS) * g
    return xn * (1.0 + scale) + shift


def _ada_kernel(c_ref, w_ref, b_ref, o_ref):
    c = c_ref[...]
    ca = c * (1.0 / (1.0 + jnp.exp(-c)))
    o_ref[...] = jnp.dot(ca.astype(BF16), w_ref[...].astype(BF16),
                         preferred_element_type=F32) + b_ref[...]


def _ada_call(c_pad, w, b, tn=1024):
    m, d = c_pad.shape
    n = w.shape[1]
    return pl.pallas_call(
        _ada_kernel,
        grid=(n // tn,),
        in_specs=[pl.BlockSpec((m, d), lambda j: (0, 0)),
                  pl.BlockSpec((d, tn), lambda j: (0, j)),
                  pl.BlockSpec((1, tn), lambda j: (0, j))],
        out_specs=pl.BlockSpec((m, tn), lambda j: (0, j)),
        out_shape=jax.ShapeDtypeStruct((m, n), F32),
        compiler_params=_cparams(("arbitrary",), 40),
        name="ada_mod",
    )(c_pad, w, b)


def _ffn_kernel(x_ref, g_ref, sh_ref, sc_ref, gate_ref, win_ref, wout_ref,
                *rest, nj, final_norm, pack):
    rest = list(rest)
    fg_ref = rest.pop(0) if final_norm else None
    pack_in = [rest.pop(0) for _ in range(3)] if pack else None
    o_ref = rest.pop(0)
    pack_out = [rest.pop(0) for _ in range(2)] if pack else None
    (h_scr,) = rest
    j = pl.program_id(1)

    @pl.when(j == 0)
    def _():
        h = _norm_mod(x_ref[...], g_ref[...], sh_ref[0], sc_ref[0])
        h_scr[...] = h.astype(BF16)
        o_ref[...] = jnp.zeros_like(o_ref)

    hw = jnp.dot(h_scr[...], win_ref[...], preferred_element_type=F32)
    tiles = [hw[:, k * LANES:(k + 1) * LANES] for k in range(win_ref.shape[1] // LANES)]
    a = jnp.concatenate(tiles[0::2], axis=1)
    b = jnp.concatenate(tiles[1::2], axis=1)
    act = (a * (1.0 / (1.0 + jnp.exp(-a))) * b).astype(BF16)
    o_ref[...] += jnp.dot(act, wout_ref[...], preferred_element_type=F32)

    @pl.when(j == nj - 1)
    def _():
        y = x_ref[...] + 0.5 * gate_ref[0] * o_ref[...]
        if final_norm:
            ms = jnp.mean(y * y, axis=-1, keepdims=True)
            y = y * lax.rsqrt(ms + EPS) * fg_ref[...]
        o_ref[...] = y

    if pack:
        nblk, splits, n_steps = pack
        step = pl.program_id(0) * nj + j

        @pl.when(step < n_steps)
        def _():
            _pack_block(step // splits < nblk, *pack_in, *pack_out)


FFN_BLOCKS_PER_STEP = 4
FFN_PACK_SPLITS = 2


def _ffn_call(x, g, shift, scale, gate, w_in_p, w_out_p, final_g, *, seq, tm=512,
              pack_next=None):
    t, d = x.shape
    bps = FFN_BLOCKS_PER_STEP
    nj = w_out_p.shape[0] // (bps * LANES)
    tiles_per_seq = seq // tm
    mod_spec = pl.BlockSpec((1, 1, d), lambda i, j: (i // tiles_per_seq, 0, 0))
    vec_spec = pl.BlockSpec((1, d), lambda i, j: (0, 0))
    in_specs = [pl.BlockSpec((tm, d), lambda i, j: (i, 0)),
                vec_spec, mod_spec, mod_spec, mod_spec,
                pl.BlockSpec((d, bps * 2 * LANES), lambda i, j: (0, j)),
                pl.BlockSpec((bps * LANES, d), lambda i, j: (j, 0))]
    args = [x, g, shift, scale, gate, w_in_p, w_out_p]
    if final_g is not None:
        in_specs.append(vec_spec)
        args.append(final_g)
    out_specs = [pl.BlockSpec((tm, d), lambda i, j: (i, 0))]
    out_shape = [jax.ShapeDtypeStruct((t, d), F32)]
    pack = None
    if pack_next is not None:
        _, nblk, nblk_pad = _pack_geometry(*pack_next)
        pack = (nblk, FFN_PACK_SPLITS, nblk_pad * FFN_PACK_SPLITS)
        assert (t // tm) * nj >= pack[2]
        p_in, p_out, p_shape = _pack_specs(*pack_next, lambda i, j: i * nj + j,
                                           FFN_PACK_SPLITS)
        in_specs += p_in
        args += [pack_next[0], pack_next[0], pack_next[1]]
        out_specs += p_out
        out_shape += p_shape
    outs = pl.pallas_call(
        functools.partial(_ffn_kernel, nj=nj, final_norm=final_g is not None, pack=pack),
        grid=(t // tm, nj),
        in_specs=in_specs,
        out_specs=out_specs,
        out_shape=out_shape,
        scratch_shapes=[pltpu.VMEM((tm, d), BF16)],
        compiler_params=_cparams(("arbitrary" if pack else "parallel", "arbitrary"), 52),
        name="ffn_final" if final_g is not None else "ffn",
    )(*args)
    return outs[0] if pack is None else tuple(outs)


def _pack_block(keep, a_ref, b_ref, r_ref, wi_ref, wo_ref):
    wi_ref[:, :LANES] = jnp.where(keep, a_ref[...], 0.0).astype(BF16)
    wi_ref[:, LANES:] = jnp.where(keep, b_ref[...], 0.0).astype(BF16)
    wo_ref[...] = jnp.where(keep, r_ref[...], 0.0).astype(BF16)


def _pack_geometry(w_in, w_out):
    d = w_in.shape[0]
    nblk = w_out.shape[0] // LANES
    nblk_pad = -(-nblk // FFN_BLOCKS_PER_STEP) * FFN_BLOCKS_PER_STEP
    return d, nblk, nblk_pad


def _pack_specs(w_in, w_out, step, splits=1):
    d, nblk, nblk_pad = _pack_geometry(w_in, w_out)
    dp = d // splits
    sub = lambda *g: jnp.minimum(step(*g), nblk_pad * splits - 1)
    part = lambda *g: sub(*g) % splits
    dst = lambda *g: sub(*g) // splits
    src = lambda *g: jnp.minimum(dst(*g), nblk - 1)
    in_specs = [pl.BlockSpec((dp, LANES), lambda *g: (part(*g), src(*g))),
                pl.BlockSpec((dp, LANES), lambda *g: (part(*g), nblk + src(*g))),
                pl.BlockSpec((LANES, dp), lambda *g: (src(*g), part(*g)))]
    out_specs = [pl.BlockSpec((dp, 2 * LANES), lambda *g: (part(*g), dst(*g))),
                 pl.BlockSpec((LANES, dp), lambda *g: (dst(*g), part(*g)))]
    out_shape = [jax.ShapeDtypeStruct((d, nblk_pad * 2 * LANES), BF16),
                 jax.ShapeDtypeStruct((nblk_pad * LANES, d), BF16)]
    return in_specs, out_specs, out_shape


def _pack_kernel(a_ref, b_ref, r_ref, wi_ref, wo_ref, *, nblk):
    _pack_block(pl.program_id(0) < nblk, a_ref, b_ref, r_ref, wi_ref, wo_ref)


def _pack_ffn_weights(w_in, w_out):
    _, nblk, nblk_pad = _pack_geometry(w_in, w_out)
    in_specs, out_specs, out_shape = _pack_specs(w_in, w_out, lambda s: s)
    return pl.pallas_call(
        functools.partial(_pack_kernel, nblk=nblk),
        grid=(nblk_pad,),
        in_specs=in_specs, out_specs=out_specs, out_shape=out_shape,
        compiler_params=_cparams(("parallel",), 32),
        name="pack_ffn_weights",
    )(w_in, w_in, w_out)


def _inproj_kernel(x_ref, g_ref, sh_ref, sc_ref, w_ref, wu_ref, wf_ref, qg_ref, kg_ref,
                   qkv_ref, u_ref, f_ref, h_scr):
    j = pl.program_id(1)
    tn = w_ref.shape[1]

    def chunks(src_ref):
        for s in range(tn // MXU_DIM):
            cols = slice(s * MXU_DIM, (s + 1) * MXU_DIM)
            yield cols, jnp.dot(h_scr[...], src_ref[:, cols], preferred_element_type=F32)

    def qk_step(gain_ref):
        gain = gain_ref[...]
        for cols, acc in chunks(w_ref):
            for hh in range(MXU_DIM // HEAD_DIM):
                xh = acc[:, hh * HEAD_DIM:(hh + 1) * HEAD_DIM]
                ms = jnp.mean(xh * xh, axis=-1, keepdims=True)
                lo = cols.start + hh * HEAD_DIM
                qkv_ref[:, lo:lo + HEAD_DIM] = (
                    xh * lax.rsqrt(ms + EPS) * gain).astype(BF16)

    @pl.when(j == 0)
    def _():
        h = _norm_mod(x_ref[...], g_ref[...], sh_ref[0], sc_ref[0])
        h_scr[...] = h.astype(BF16)
        f_ref[...] = jnp.dot(h_scr[...], wf_ref[...], preferred_element_type=F32)
        qk_step(qg_ref)

    @pl.when(j == 1)
    def _():
        qk_step(kg_ref)

    @pl.when(j == 2)
    def _():
        for cols, acc in chunks(w_ref):
            qkv_ref[:, cols] = acc.astype(BF16)

    @pl.when(j == 3)
    def _():
        for cols, acc in chunks(wu_ref):
            u_ref[:, cols] = acc


def _inproj_call(x, g, shift, scale, w_qkv, w_u, w_f, q_gain, k_gain, *, seq, d_attn,
                 tm=1024):
    t, d = x.shape
    tn = d_attn
    n_qkv = w_qkv.shape[1] // tn
    nj = n_qkv + 1
    d_pool = w_u.shape[1]
    assert n_qkv == 3 and d_pool == tn
    tiles_per_seq = seq // tm
    mod_spec = pl.BlockSpec((1, 1, d), lambda i, j: (i // tiles_per_seq, 0, 0))
    return pl.pallas_call(
        _inproj_kernel,
        grid=(t // tm, nj),
        in_specs=[pl.BlockSpec((tm, d), lambda i, j: (i, 0)),
                  pl.BlockSpec((1, d), lambda i, j: (0, 0)),
                  mod_spec, mod_spec,
                  pl.BlockSpec((d, tn), lambda i, j: (0, jnp.minimum(j, n_qkv - 1))),
                  pl.BlockSpec((d, tn), lambda i, j: (0, 0)),
                  pl.BlockSpec((d, LANES), lambda i, j: (0, 0)),
                  pl.BlockSpec((1, HEAD_DIM), lambda i, j: (0, 0)),
                  pl.BlockSpec((1, HEAD_DIM), lambda i, j: (0, 0))],
        out_specs=[
            pl.BlockSpec((tm, tn), lambda i, j: (i, jnp.minimum(j, n_qkv - 1))),
            pl.BlockSpec((tm, tn), lambda i, j: (i, 0)),
            pl.BlockSpec((tm, LANES), lambda i, j: (i, 0))],
        out_shape=[jax.ShapeDtypeStruct((t, n_qkv * d_attn), BF16),
                   jax.ShapeDtypeStruct((t, d_pool), F32),
                   jax.ShapeDtypeStruct((t, LANES), F32)],
        scratch_shapes=[pltpu.VMEM((tm, d), BF16)],
        compiler_params=_cparams(("parallel", "arbitrary"), 56),
        name="mixer_inproj",
    )(x, g, shift, scale, w_qkv, w_u, w_f, q_gain, k_gain)


N_SPLIT = 3


def _bias_selector(n_heads):
    rows = N_SPLIT * LANES
    hbits = n_heads.bit_length() - 1
    r = lax.broadcasted_iota(jnp.int32, (rows, 2 * LANES), 0)
    c = lax.broadcasted_iota(jnp.int32, (rows, 2 * LANES), 1)
    in_q = c < 2 * N_SPLIT * n_heads
    in_k = (c >= LANES) & (c < LANES + 2 * N_SPLIT * n_heads)
    ck = c - LANES
    slot_q = lax.shift_right_logical(c, hbits)
    slot_k = lax.shift_right_logical(jnp.maximum(ck, 0), hbits)
    head_q = c & (n_heads - 1)
    head_k = ck & (n_heads - 1)
    one_row = r == n_heads
    q_piece = in_q & (slot_q < N_SPLIT) & (r == slot_q * LANES + head_q)
    q_one = in_q & (slot_q >= N_SPLIT) & one_row
    k_one = in_k & (slot_k < N_SPLIT) & one_row
    k_piece = in_k & (slot_k >= N_SPLIT) & (r == (slot_k - N_SPLIT) * LANES + head_k)
    sel = jnp.where(q_piece | q_one | k_one, 1.0, 0.0) - jnp.where(k_piece, 1.0, 0.0)
    return sel.astype(BF16)


def _forget_kernel(f_ref, b_ref, qe_ref, ke_ref, pad_scr, *, seq, n_heads):
    z = f_ref[0] + b_ref[...]
    x = jnp.minimum(z, 0.0) - jnp.log1p(jnp.exp(-jnp.abs(z)))
    pad_scr[pl.ds(0, seq), :] = jnp.zeros((seq, LANES), F32)
    d = 1
    while d < seq:
        pad_scr[pl.ds(seq, seq), :] = x
        x = x + pad_scr[pl.ds(seq - d, seq), :]
        d *= 2
    rem = x * LOG2E
    lane = lax.broadcasted_iota(jnp.int32, (seq, LANES), 1)
    pieces = []
    for s in range(N_SPLIT):
        piece = rem.astype(BF16)
        rem = rem - piece.astype(F32)
        if s == 0:
            piece = jnp.where(lane == n_heads, jnp.ones_like(piece), piece)
        pieces.append(piece)
    e = jnp.dot(jnp.concatenate(pieces, axis=1), _bias_selector(n_heads),
                preferred_element_type=F32)
    qe_ref[0] = e[:, :LANES].astype(BF16)
    ke_ref[0] = e[:, LANES:].astype(BF16)


def _forget_call(f_logit, b_pad, n_heads):
    nb, seq, _ = f_logit.shape
    assert n_heads & (n_heads - 1) == 0 and 2 * N_SPLIT * n_heads <= LANES
    out = jax.ShapeDtypeStruct((nb, seq, LANES), BF16)
    spec = pl.BlockSpec((1, seq, LANES), lambda b: (b, 0, 0))
    return pl.pallas_call(
        functools.partial(_forget_kernel, seq=seq, n_heads=n_heads),
        grid=(nb,),
        in_specs=[spec, pl.BlockSpec((1, LANES), lambda b: (0, 0))],
        out_specs=[spec, spec],
        out_shape=[out, out],
        scratch_shapes=[pltpu.VMEM((2 * seq, LANES), F32)],
        compiler_params=_cparams(("parallel",), 40),
        name="forget_cumsum",
    )(f_logit, b_pad)


POOL_HALO = 16
POOL_ROWS = 512


def _pool_kernel(u_ref, w_ref, s_ref, o_ref, pad_scr, *, seq):
    g = pl.program_id(1)
    cg = u_ref.shape[2]
    pad_scr[pl.ds(0, POOL_HALO), :] = jnp.zeros((POOL_HALO, cg), F32)
    pad_scr[pl.ds(POOL_HALO, seq), :] = u_ref[0]
    w = w_ref[0].astype(BF16)
    scale = s_ref[...]
    for gi, win in enumerate(POOL_WINDOWS):
        @pl.when(g == gi)
        def _(win=win):
            for r0 in range(0, seq, POOL_ROWS):
                tok = pad_scr[pl.ds(POOL_HALO + r0, POOL_ROWS), :]
                tot = tok
                for dd in range(1, win):
                    tot = tot + pad_scr[pl.ds(POOL_HALO + r0 - dd, POOL_ROWS), :]
                pos = r0 + lax.broadcasted_iota(jnp.int32, (POOL_ROWS, cg), 0)
                cnt = jnp.minimum(pos + 1, win).astype(F32)
                p = tot / cnt - tok
                y = jnp.dot(p.astype(BF16), w, preferred_element_type=F32) * scale
                o_ref[0, pl.ds(r0, POOL_ROWS), :] = y.astype(BF16)


def _pool_call(u, pool_w, pool_scale):
    nb, seq, d_pool = u.shape
    ng, cg, _ = pool_w.shape
    return pl.pallas_call(
        functools.partial(_pool_kernel, seq=seq),
        grid=(nb, ng),
        in_specs=[pl.BlockSpec((1, seq, cg), lambda b, g: (b, 0, g)),
                  pl.BlockSpec((1, cg, cg), lambda b, g: (g, 0, 0)),
                  pl.BlockSpec((1, cg), lambda b, g: (0, g))],
        out_specs=pl.BlockSpec((1, seq, cg), lambda b, g: (b, 0, g)),
        out_shape=jax.ShapeDtypeStruct((nb, seq, d_pool), BF16),
        scratch_shapes=[pltpu.VMEM((POOL_HALO + seq, cg), F32)],
        compiler_params=_cparams(("parallel", "arbitrary"), 40),
        name="ms_pool",
    )(u, pool_w, pool_scale)


NEG_BIG = -1e30


def _lane_tile_reduce(x, op):
    out = x[:, :LANES]
    for t in range(1, x.shape[1] // LANES):
        out = op(out, x[:, t * LANES:(t + 1) * LANES])
    return out


def _attn_kernel(q_ref, qe_ref, k_ref, ke_ref, v_ref, o_ref, s_scr, p_scr, ke_scr,
                 *, tq, n_heads):
    seq = q_ref.shape[0]
    lane = lax.broadcasted_iota(jnp.int32, (seq, LANES), 1)
    mine = ((lane & (n_heads - 1)) == pl.program_id(1)) & (lane < 2 * N_SPLIT * n_heads)
    ke = ke_ref[0]
    ke_scr[...] = jnp.where(mine, ke, jnp.zeros_like(ke))
    for i in reversed(range(seq // tq)):
        slot = i % 2
        rows = pl.ds(i * tq, tq)
        qx = jnp.concatenate([q_ref[rows, :], qe_ref[0, rows, :]], axis=1)
        n_chunks = i + 1
        m_tile = None
        for c in range(n_chunks):
            cols = pl.ds(c * tq, tq)
            kx = jnp.concatenate([k_ref[cols, :], ke_scr[cols, :]], axis=1)
            s = lax.dot_general(qx, kx, (((1,), (1,)), ((), ())),
                                preferred_element_type=F32)
            if c == i:
                row = lax.broadcasted_iota(jnp.int32, (tq, tq), 0)
                col = lax.broadcasted_iota(jnp.int32, (tq, tq), 1)
                s = jnp.where(row >= col, s, NEG_BIG)
            s_scr[slot, :, cols] = s
            cm = _lane_tile_reduce(s, jnp.maximum)
            m_tile = cm if m_tile is None else jnp.maximum(m_tile, cm)
        m_row = jnp.max(m_tile, axis=-1, keepdims=True)
        l_tile = None
        for c in range(n_chunks):
            cols = pl.ds(c * tq, tq)
            p = jnp.exp2(s_scr[slot, :, cols] - m_row)
            p_scr[slot, :, cols] = p.astype(BF16)
            cl = _lane_tile_reduce(p, jnp.add)
            l_tile = cl if l_tile is None else l_tile + cl
        l_row = jnp.sum(l_tile, axis=-1, keepdims=True)
        kend = n_chunks * tq
        acc = jnp.dot(p_scr[slot, :, pl.ds(0, kend)], v_ref[pl.ds(0, kend), :],
                      preferred_element_type=F32)
        o_ref[rows, :] = (acc / l_row).astype(BF16)


def _attn_call(qkv, qe, ke, *, nb, seq, n_heads, tq=512):
    t = qkv.shape[0]
    d_attn = n_heads * HEAD_DIM
    head_spec = lambda off: pl.BlockSpec((seq, HEAD_DIM), lambda b, h: (b, off + h))
    bias_spec = pl.BlockSpec((1, seq, LANES), lambda b, h: (b, 0, 0))
    return pl.pallas_call(
        functools.partial(_attn_kernel, tq=tq, n_heads=n_heads),
        grid=(nb, n_heads),
        in_specs=[head_spec(0), bias_spec, head_spec(n_heads), bias_spec,
                  head_spec(2 * n_heads)],
        out_specs=head_spec(0),
        out_shape=jax.ShapeDtypeStruct((t, d_attn), BF16),
        scratch_shapes=[pltpu.VMEM((2, tq, seq), F32), pltpu.VMEM((2, tq, seq), BF16),
                        pltpu.VMEM((seq, LANES), BF16)],
        compiler_params=_cparams(("parallel", "parallel"), 56),
        name="fox_attention",
    )(qkv, qe, qkv, ke, qkv)


def _outproj_kernel(x_ref, attn_ref, pool_ref, wa_ref, wp_ref, gate_ref, o_ref, w_scr):
    da = attn_ref.shape[1]

    @pl.when(pl.program_id(0) == 0)
    def _():
        w_scr[:da, :] = wa_ref[...].astype(BF16)
        w_scr[da:, :] = wp_ref[...].astype(BF16)

    y = jnp.dot(attn_ref[...], w_scr[:da, :], preferred_element_type=F32)
    y = y + jnp.dot(pool_ref[...], w_scr[da:, :], preferred_element_type=F32)
    o_ref[...] = x_ref[...] + gate_ref[0] * y


def _outproj_call(x, attn, pool, w_out, gate, *, seq, tm=512):
    t, d = x.shape
    da = attn.shape[1]
    dp = pool.shape[1]
    assert da == dp and w_out.shape == (da + dp, d)
    tiles_per_seq = seq // tm
    w_spec = lambda half: pl.BlockSpec((da, d), lambda i: (half, 0),
                                       pipeline_mode=pl.Buffered(1))
    return pl.pallas_call(
        _outproj_kernel,
        grid=(t // tm,),
        in_specs=[pl.BlockSpec((tm, d), lambda i: (i, 0)),
                  pl.BlockSpec((tm, da), lambda i: (i, 0)),
                  pl.BlockSpec((tm, dp), lambda i: (i, 0)),
                  w_spec(0), w_spec(1),
                  pl.BlockSpec((1, 1, d), lambda i: (i // tiles_per_seq, 0, 0))],
        out_specs=pl.BlockSpec((tm, d), lambda i: (i, 0)),
        out_shape=jax.ShapeDtypeStruct((t, d), F32),
        scratch_shapes=[pltpu.VMEM((da + dp, d), BF16)],
        compiler_params=_cparams(("arbitrary",), 56),
        name="mixer_outproj",
    )(x, attn, pool, w_out, w_out, gate)


def kernel(x, c, w_ada, b_ada, ffn1_norm_g, ffn1_w_in, ffn1_w_out, mix_norm_g, w_in,
           b_forget, q_norm_g, k_norm_g, pool_w, pool_scale, w_out, ffn2_norm_g,
           ffn2_w_in, ffn2_w_out, final_norm_g):
    nb, seq, d = x.shape
    t = nb * seq
    n_heads = b_forget.shape[1]
    d_attn = n_heads * HEAD_DIM
    d_pool = pool_scale.shape[1]
    depth = w_ada.shape[0]
    xf = x.reshape(t, d)

    c_pad = jnp.pad(c, ((0, 8 - nb), (0, 0)))
    for l in range(depth):
        mod = _ada_call(c_pad, w_ada[l], b_ada[l].reshape(1, -1))[:nb]
        sh1, sc1, g1, sh2, sc2, g2, sh3, sc3, g3 = [
            mod[:, i * d:(i + 1) * d].reshape(nb, 1, d) for i in range(9)]

        w1i, w1o = _pack_ffn_weights(ffn1_w_in[l], ffn1_w_out[l])
        xf, w2i, w2o = _ffn_call(xf, ffn1_norm_g[l].reshape(1, d), sh1, sc1, g1, w1i, w1o,
                                 None, seq=seq, pack_next=(ffn2_w_in[l], ffn2_w_out[l]))

        wl = w_in[l]
        f0 = 3 * d_attn
        w_qkv = wl[:, :f0].astype(BF16)
        w_u = wl[:, f0 + n_heads:].astype(BF16)
        w_f = jnp.pad(wl[:, f0:f0 + n_heads], ((0, 0), (0, LANES - n_heads))).astype(BF16)
        q_gain = (q_norm_g[l] * (HEAD_DIM ** -0.5 * LOG2E)).reshape(1, HEAD_DIM)
        k_gain = k_norm_g[l].reshape(1, HEAD_DIM)
        qkv, u, f_logit = _inproj_call(
            xf, mix_norm_g[l].reshape(1, d), sh2, sc2, w_qkv, w_u, w_f, q_gain, k_gain,
            seq=seq, d_attn=d_attn)
        b_pad = jnp.pad(b_forget[l], (0, LANES - n_heads)).reshape(1, LANES)
        qe, ke = _forget_call(f_logit.reshape(nb, seq, LANES), b_pad, n_heads)
        attn = _attn_call(qkv, qe, ke, nb=nb, seq=seq, n_heads=n_heads)
        pool = _pool_call(u.reshape(nb, seq, d_pool), pool_w[l], pool_scale[l].reshape(1, -1))
        xf = _outproj_call(xf, attn, pool.reshape(t, d_pool), w_out[l], g2, seq=seq)

        last = l == depth - 1
        xf = _ffn_call(xf, ffn2_norm_g[l].reshape(1, d), sh3, sc3, g3, w2i, w2o,
                       final_norm_g.reshape(1, d) if last else None, seq=seq)
    return xf.reshape(nb, seq, d)
```

```python
import functools

import jax
import jax.numpy as jnp
from jax import lax
from jax.experimental import pallas as pl
from jax.experimental.pallas import tpu as pltpu

F32 = jnp.float32
BF16 = jnp.bfloat16

EPS = 1e-6
HEAD_DIM = 128
POOL_WINDOWS = (2, 4, 8, 16)
LANES = 128
MXU_DIM = 256
MIB = 1024 * 1024
LOG2E = 1.4426950408889634


def _cparams(dims, vmem_mib):
    return pltpu.CompilerParams(dimension_semantics=dims,
                                vmem_limit_bytes=vmem_mib * MIB)


def _norm_mod(x, g, shift, scale):
    ms = jnp.mean(x * x, axis=-1, keepdims=True)
    xn = x * lax.rsqrt(ms + EPS) * g
    return xn * (1.0 + scale) + shift


def _ada_kernel(c_ref, w_ref, b_ref, o_ref):
    c = c_ref[...]
    ca = c * (1.0 / (1.0 + jnp.exp(-c)))
    o_ref[...] = jnp.dot(ca.astype(BF16), w_ref[...].astype(BF16),
                         preferred_element_type=F32) + b_ref[...]


def _ada_call(c_pad, w, b, tn=1024):
    m, d = c_pad.shape
    n = w.shape[1]
    return pl.pallas_call(
        _ada_kernel,
        grid=(n // tn,),
        in_specs=[pl.BlockSpec((m, d), lambda j: (0, 0)),
                  pl.BlockSpec((d, tn), lambda j: (0, j)),
                  pl.BlockSpec((1, tn), lambda j: (0, j))],
        out_specs=pl.BlockSpec((m, tn), lambda j: (0, j)),
        out_shape=jax.ShapeDtypeStruct((m, n), F32),
        compiler_params=_cparams(("arbitrary",), 40),
        name="ada_mod",
    )(c_pad, w, b)


def _ffn_kernel(x_ref, g_ref, sh_ref, sc_ref, gate_ref, win_ref, wout_ref,
                *rest, nj, final_norm, pack):
    rest = list(rest)
    fg_ref = rest.pop(0) if final_norm else None
    pack_in = [rest.pop(0) for _ in range(3)] if pack else None
    o_ref = rest.pop(0)
    pack_out = [rest.pop(0) for _ in range(2)] if pack else None
    (h_scr,) = rest
    j = pl.program_id(1)

    @pl.when(j == 0)
    def _():
        h = _norm_mod(x_ref[...], g_ref[...], sh_ref[0], sc_ref[0])
        h_scr[...] = h.astype(BF16)
        o_ref[...] = jnp.zeros_like(o_ref)

    hw = jnp.dot(h_scr[...], win_ref[...], preferred_element_type=F32)
    tiles = [hw[:, k * LANES:(k + 1) * LANES] for k in range(win_ref.shape[1] // LANES)]
    a = jnp.concatenate(tiles[0::2], axis=1)
    b = jnp.concatenate(tiles[1::2], axis=1)
    act = (a * (1.0 / (1.0 + jnp.exp(-a))) * b).astype(BF16)
    o_ref[...] += jnp.dot(act, wout_ref[...], preferred_element_type=F32)

    @pl.when(j == nj - 1)
    def _():
        y = x_ref[...] + 0.5 * gate_ref[0] * o_ref[...]
        if final_norm:
            ms = jnp.mean(y * y, axis=-1, keepdims=True)
            y = y * lax.rsqrt(ms + EPS) * fg_ref[...]
        o_ref[...] = y

    if pack:
        nblk, splits, n_steps = pack
        step = pl.program_id(0) * nj + j

        @pl.when(step < n_steps)
        def _():
            _pack_block(step // splits < nblk, *pack_in, *pack_out)


FFN_BLOCKS_PER_STEP = 4
FFN_PACK_SPLITS = 2


def _ffn_call(x, g, shift, scale, gate, w_in_p, w_out_p, final_g, *, seq, tm=512,
              pack_next=None):
    t, d = x.shape
    bps = FFN_BLOCKS_PER_STEP
    nj = w_out_p.shape[0] // (bps * LANES)
    tiles_per_seq = seq // tm
    mod_spec = pl.BlockSpec((1, 1, d), lambda i, j: (i // tiles_per_seq, 0, 0))
    vec_spec = pl.BlockSpec((1, d), lambda i, j: (0, 0))
    in_specs = [pl.BlockSpec((tm, d), lambda i, j: (i, 0)),
                vec_spec, mod_spec, mod_spec, mod_spec,
                pl.BlockSpec((d, bps * 2 * LANES), lambda i, j: (0, j)),
                pl.BlockSpec((bps * LANES, d), lambda i, j: (j, 0))]
    args = [x, g, shift, scale, gate, w_in_p, w_out_p]
    if final_g is not None:
        in_specs.append(vec_spec)
        args.append(final_g)
    out_specs = [pl.BlockSpec((tm, d), lambda i, j: (i, 0))]
    out_shape = [jax.ShapeDtypeStruct((t, d), F32)]
    pack = None
    if pack_next is not None:
        _, nblk, nblk_pad = _pack_geometry(*pack_next)
        pack = (nblk, FFN_PACK_SPLITS, nblk_pad * FFN_PACK_SPLITS)
        assert (t // tm) * nj >= pack[2]
        p_in, p_out, p_shape = _pack_specs(*pack_next, lambda i, j: i * nj + j,
                                           FFN_PACK_SPLITS)
        in_specs += p_in
        args += [pack_next[0], pack_next[0], pack_next[1]]
        out_specs += p_out
        out_shape += p_shape
    outs = pl.pallas_call(
        functools.partial(_ffn_kernel, nj=nj, final_norm=final_g is not None, pack=pack),
        grid=(t // tm, nj),
        in_specs=in_specs,
        out_specs=out_specs,
        out_shape=out_shape,
        scratch_shapes=[pltpu.VMEM((tm, d), BF16)],
        compiler_params=_cparams(("arbitrary" if pack else "parallel", "arbitrary"), 52),
        name="ffn_final" if final_g is not None else "ffn",
    )(*args)
    return outs[0] if pack is None else tuple(outs)


def _pack_block(keep, a_ref, b_ref, r_ref, wi_ref, wo_ref):
    wi_ref[:, :LANES] = jnp.where(keep, a_ref[...], 0.0).astype(BF16)
    wi_ref[:, LANES:] = jnp.where(keep, b_ref[...], 0.0).astype(BF16)
    wo_ref[...] = jnp.where(keep, r_ref[...], 0.0).astype(BF16)


def _pack_geometry(w_in, w_out):
    d = w_in.shape[0]
    nblk = w_out.shape[0] // LANES
    nblk_pad = -(-nblk // FFN_BLOCKS_PER_STEP) * FFN_BLOCKS_PER_STEP
    return d, nblk, nblk_pad


def _pack_specs(w_in, w_out, step, splits=1):
    d, nblk, nblk_pad = _pack_geometry(w_in, w_out)
    dp = d // splits
    sub = lambda *g: jnp.minimum(step(*g), nblk_pad * splits - 1)
    part = lambda *g: sub(*g) % splits
    dst = lambda *g: sub(*g) // splits
    src = lambda *g: jnp.minimum(dst(*g), nblk - 1)
    in_specs = [pl.BlockSpec((dp, LANES), lambda *g: (part(*g), src(*g))),
                pl.BlockSpec((dp, LANES), lambda *g: (part(*g), nblk + src(*g))),
                pl.BlockSpec((LANES, dp), lambda *g: (src(*g), part(*g)))]
    out_specs = [pl.BlockSpec((dp, 2 * LANES), lambda *g: (part(*g), dst(*g))),
                 pl.BlockSpec((LANES, dp), lambda *g: (dst(*g), part(*g)))]
    out_shape = [jax.ShapeDtypeStruct((d, nblk_pad * 2 * LANES), BF16),
                 jax.ShapeDtypeStruct((nblk_pad * LANES, d), BF16)]
    return in_specs, out_specs, out_shape


def _pack_kernel(a_ref, b_ref, r_ref, wi_ref, wo_ref, *, nblk):
    _pack_block(pl.program_id(0) < nblk, a_ref, b_ref, r_ref, wi_ref, wo_ref)


def _pack_ffn_weights(w_in, w_out):
    _, nblk, nblk_pad = _pack_geometry(w_in, w_out)
    in_specs, out_specs, out_shape = _pack_specs(w_in, w_out, lambda s: s)
    return pl.pallas_call(
        functools.partial(_pack_kernel, nblk=nblk),
        grid=(nblk_pad,),
        in_specs=in_specs, out_specs=out_specs, out_shape=out_shape,
        compiler_params=_cparams(("parallel",), 32),
        name="pack_ffn_weights",
    )(w_in, w_in, w_out)


def _ffn_up_kernel(xn_ref, g_ref, shn_ref, scn_ref, a0_ref, a1_ref, b0_ref, b1_ref,
                   hid_ref, w_scr, h_scr, *, n_load, nblk, half_blocks):
    half = pl.program_id(0)
    s = pl.program_id(1)

    @pl.when(s < n_load)
    def _():
        k0 = half * half_blocks + 2 * s
        for q, ref in enumerate((a0_ref, a1_ref, b0_ref, b1_ref)):
            keep = k0 + (q % 2) < nblk
            w_scr[s, :, q * LANES:(q + 1) * LANES] = jnp.where(
                keep, ref[...], 0.0).astype(BF16)

    @pl.when(s == n_load - 1)
    def _():
        h_scr[0] = _norm_mod(xn_ref[...], g_ref[...], shn_ref[0], scn_ref[0]).astype(BF16)

    @pl.when(s >= n_load)
    def _():
        cur = (s - n_load) % 2
        for c in range(n_load):
            hw = jnp.dot(h_scr[cur], w_scr[c], preferred_element_type=F32)
            a = hw[:, :2 * LANES]
            b = hw[:, 2 * LANES:]
            hid_ref[:, c * 2 * LANES:(c + 1) * 2 * LANES] = (
                a * (1.0 / (1.0 + jnp.exp(-a))) * b).astype(BF16)
        h_scr[1 - cur] = _norm_mod(xn_ref[...], g_ref[...], shn_ref[0],
                                   scn_ref[0]).astype(BF16)


def _ffn_up_call(x, g, shift, scale, w_in, *, seq, nblk, nblk_pad, tm=512):
    t, d = x.shape
    nt = t // tm
    half_blocks = nblk_pad // 2
    n_load = half_blocks // 2
    tiles_per_seq = seq // tm
    nxt = lambda s: jnp.clip(s - n_load + 1, 0, nt - 1)
    blk = lambda c, s, q: jnp.minimum(c * half_blocks + 2 * jnp.minimum(s, n_load - 1) + q,
                                      nblk - 1)
    col = lambda off, q: pl.BlockSpec((d, LANES), lambda c, s: (0, off + blk(c, s, q)))
    mod_spec = pl.BlockSpec((1, 1, d), lambda c, s: (nxt(s) // tiles_per_seq, 0, 0))
    return pl.pallas_call(
        functools.partial(_ffn_up_kernel, n_load=n_load, nblk=nblk,
                          half_blocks=half_blocks),
        grid=(2, n_load + nt),
        in_specs=[pl.BlockSpec((tm, d), lambda c, s: (nxt(s), 0)),
                  pl.BlockSpec((1, d), lambda c, s: (0, 0)),
                  mod_spec, mod_spec,
                  col(0, 0), col(0, 1), col(nblk, 0), col(nblk, 1)],
        out_specs=pl.BlockSpec((tm, half_blocks * LANES),
                               lambda c, s: (jnp.maximum(s - n_load, 0), c)),
        out_shape=jax.ShapeDtypeStruct((t, nblk_pad * LANES), BF16),
        scratch_shapes=[pltpu.VMEM((n_load, d, 4 * LANES), BF16),
                        pltpu.VMEM((2, tm, d), BF16)],
        compiler_params=_cparams(("arbitrary", "arbitrary"), 58),
        name="ffn_up",
    )(x, g, shift, scale, w_in, w_in, w_in, w_in)


FFN_DOWN_LOAD_BLOCKS = 2
FFN_DOWN_COLS = 512


def _ffn_down_kernel(hid_ref, x_ref, gate_ref, *rest, n_load, nblk, final_norm):
    rest = list(rest)
    r_refs = [rest.pop(0) for _ in range(FFN_DOWN_LOAD_BLOCKS)]
    fg_ref = rest.pop(0) if final_norm else None
    o_ref, w_scr = rest
    s = pl.program_id(0)
    d = o_ref.shape[1]

    @pl.when(s < n_load)
    def _():
        for q, ref in enumerate(r_refs):
            k = FFN_DOWN_LOAD_BLOCKS * s + q
            rows = pl.ds(pl.multiple_of(k * LANES, LANES), LANES)
            w_scr[rows, :] = jnp.where(k < nblk, ref[...], 0.0).astype(BF16)

    @pl.when(s >= n_load)
    def _():
        ssq = None
        for n in range(d // FFN_DOWN_COLS):
            cols = slice(n * FFN_DOWN_COLS, (n + 1) * FFN_DOWN_COLS)
            acc = jnp.dot(hid_ref[...], w_scr[:, cols], preferred_element_type=F32)
            y = x_ref[:, cols] + 0.5 * gate_ref[0, :, cols] * acc
            if final_norm:
                part = jnp.sum(y * y, axis=-1, keepdims=True)
                ssq = part if ssq is None else ssq + part
            o_ref[:, cols] = y
        if final_norm:
            o_ref[...] = o_ref[...] * lax.rsqrt(ssq * (1.0 / d) + EPS) * fg_ref[...]


def _ffn_down_call(hidden, x, gate, w_out, final_g, *, seq, nblk, nblk_pad, tm=512):
    t, d = x.shape
    nt = t // tm
    lb = FFN_DOWN_LOAD_BLOCKS
    n_load = nblk_pad // lb
    tiles_per_seq = seq // tm
    tile = lambda s: jnp.maximum(s - n_load, 0)
    row = lambda q: pl.BlockSpec(
        (LANES, d), lambda s: (jnp.minimum(lb * jnp.minimum(s, n_load - 1) + q, nblk - 1), 0))
    in_specs = [pl.BlockSpec((tm, nblk_pad * LANES), lambda s: (tile(s), 0)),
                pl.BlockSpec((tm, d), lambda s: (tile(s), 0)),
                pl.BlockSpec((1, 1, d), lambda s: (tile(s) // tiles_per_seq, 0, 0))]
    in_specs += [row(q) for q in range(lb)]
    args = [hidden, x, gate] + [w_out] * lb
    if final_g is not None:
        in_specs.append(pl.BlockSpec((1, d), lambda s: (0, 0)))
        args.append(final_g)
    return pl.pallas_call(
        functools.partial(_ffn_down_kernel, n_load=n_load, nblk=nblk,
                          final_norm=final_g is not None),
        grid=(n_load + nt,),
        in_specs=in_specs,
        out_specs=pl.BlockSpec((tm, d), lambda s: (tile(s), 0)),
        out_shape=jax.ShapeDtypeStruct((t, d), F32),
        scratch_shapes=[pltpu.VMEM((nblk_pad * LANES, d), BF16)],
        compiler_params=_cparams(("arbitrary",), 62),
        name="ffn_down_final" if final_g is not None else "ffn_down",
    )(*args)


def _ffn_resident(x, g, shift, scale, gate, w_in, w_out, final_g, *, seq):
    _, nblk, nblk_pad = _pack_geometry(w_in, w_out)
    assert nblk_pad % 4 == 0
    hidden = _ffn_up_call(x, g, shift, scale, w_in, seq=seq, nblk=nblk, nblk_pad=nblk_pad)
    return _ffn_down_call(hidden, x, gate, w_out, final_g, seq=seq, nblk=nblk,
                          nblk_pad=nblk_pad)


def _inproj_kernel(x_ref, g_ref, sh_ref, sc_ref, w_ref, wu_ref, wf_ref, qg_ref, kg_ref,
                   qkv_ref, u_ref, f_ref, h_scr):
    j = pl.program_id(1)
    tn = w_ref.shape[1]

    def chunks(src_ref):
        for s in range(tn // MXU_DIM):
            cols = slice(s * MXU_DIM, (s + 1) * MXU_DIM)
            yield cols, jnp.dot(h_scr[...], src_ref[:, cols], preferred_element_type=F32)

    def qk_step(gain_ref):
        gain = gain_ref[...]
        for cols, acc in chunks(w_ref):
            for hh in range(MXU_DIM // HEAD_DIM):
                xh = acc[:, hh * HEAD_DIM:(hh + 1) * HEAD_DIM]
                ms = jnp.mean(xh * xh, axis=-1, keepdims=True)
                lo = cols.start + hh * HEAD_DIM
                qkv_ref[:, lo:lo + HEAD_DIM] = (
                    xh * lax.rsqrt(ms + EPS) * gain).astype(BF16)

    @pl.when(j == 0)
    def _():
        h = _norm_mod(x_ref[...], g_ref[...], sh_ref[0], sc_ref[0])
        h_scr[...] = h.astype(BF16)
        f_ref[...] = jnp.dot(h_scr[...], wf_ref[...], preferred_element_type=F32)
        qk_step(qg_ref)

    @pl.when(j == 1)
    def _():
        qk_step(kg_ref)

    @pl.when(j == 2)
    def _():
        for cols, acc in chunks(w_ref):
            qkv_ref[:, cols] = acc.astype(BF16)

    @pl.when(j == 3)
    def _():
        for cols, acc in chunks(wu_ref):
            u_ref[:, cols] = acc


def _inproj_call(x, g, shift, scale, w_qkv, w_u, w_f, q_gain, k_gain, *, seq, d_attn,
                 tm=1024):
    t, d = x.shape
    tn = d_attn
    n_qkv = 3
    nj = n_qkv + 1
    d_pool = w_u.shape[1]
    assert w_qkv.shape[1] >= n_qkv * tn and d_pool == tn
    tiles_per_seq = seq // tm
    mod_spec = pl.BlockSpec((1, 1, d), lambda i, j: (i // tiles_per_seq, 0, 0))
    return pl.pallas_call(
        _inproj_kernel,
        grid=(t // tm, nj),
        in_specs=[pl.BlockSpec((tm, d), lambda i, j: (i, 0)),
                  pl.BlockSpec((1, d), lambda i, j: (0, 0)),
                  mod_spec, mod_spec,
                  pl.BlockSpec((d, tn), lambda i, j: (0, jnp.minimum(j, n_qkv - 1))),
                  pl.BlockSpec((d, tn), lambda i, j: (0, 0)),
                  pl.BlockSpec((d, LANES), lambda i, j: (0, 0)),
                  pl.BlockSpec((1, HEAD_DIM), lambda i, j: (0, 0)),
                  pl.BlockSpec((1, HEAD_DIM), lambda i, j: (0, 0))],
        out_specs=[
            pl.BlockSpec((tm, tn), lambda i, j: (i, jnp.minimum(j, n_qkv - 1))),
            pl.BlockSpec((tm, tn), lambda i, j: (i, 0)),
            pl.BlockSpec((tm, LANES), lambda i, j: (i, 0))],
        out_shape=[jax.ShapeDtypeStruct((t, n_qkv * d_attn), BF16),
                   jax.ShapeDtypeStruct((t, d_pool), F32),
                   jax.ShapeDtypeStruct((t, LANES), F32)],
        scratch_shapes=[pltpu.VMEM((tm, d), BF16)],
        compiler_params=_cparams(("parallel", "arbitrary"), 56),
        name="mixer_inproj",
    )(x, g, shift, scale, w_qkv, w_u, w_f, q_gain, k_gain)


N_SPLIT = 3


def _bias_selector(n_heads):
    rows = N_SPLIT * LANES
    hbits = n_heads.bit_length() - 1
    r = lax.broadcasted_iota(jnp.int32, (rows, 2 * LANES), 0)
    c = lax.broadcasted_iota(jnp.int32, (rows, 2 * LANES), 1)
    in_q = c < 2 * N_SPLIT * n_heads
    in_k = (c >= LANES) & (c < LANES + 2 * N_SPLIT * n_heads)
    ck = c - LANES
    slot_q = lax.shift_right_logical(c, hbits)
    slot_k = lax.shift_right_logical(jnp.maximum(ck, 0), hbits)
    head_q = c & (n_heads - 1)
    head_k = ck & (n_heads - 1)
    one_row = r == n_heads
    q_piece = in_q & (slot_q < N_SPLIT) & (r == slot_q * LANES + head_q)
    q_one = in_q & (slot_q >= N_SPLIT) & one_row
    k_one = in_k & (slot_k < N_SPLIT) & one_row
    k_piece = in_k & (slot_k >= N_SPLIT) & (r == (slot_k - N_SPLIT) * LANES + head_k)
    sel = jnp.where(q_piece | q_one | k_one, 1.0, 0.0) - jnp.where(k_piece, 1.0, 0.0)
    return sel.astype(BF16)


def _forget_kernel(f_ref, b_ref, qe_ref, ke_ref, pad_scr, *, seq, n_heads):
    z = f_ref[0] + b_ref[...]
    x = jnp.minimum(z, 0.0) - jnp.log1p(jnp.exp(-jnp.abs(z)))
    pad_scr[pl.ds(0, seq), :] = jnp.zeros((seq, LANES), F32)
    d = 1
    while d < seq:
        pad_scr[pl.ds(seq, seq), :] = x
        x = x + pad_scr[pl.ds(seq - d, seq), :]
        d *= 2
    rem = x * LOG2E
    lane = lax.broadcasted_iota(jnp.int32, (seq, LANES), 1)
    pieces = []
    for s in range(N_SPLIT):
        piece = rem.astype(BF16)
        rem = rem - piece.astype(F32)
        if s == 0:
            piece = jnp.where(lane == n_heads, jnp.ones_like(piece), piece)
        pieces.append(piece)
    e = jnp.dot(jnp.concatenate(pieces, axis=1), _bias_selector(n_heads),
                preferred_element_type=F32)
    qe_ref[0] = e[:, :LANES].astype(BF16)
    ke_ref[0] = e[:, LANES:].astype(BF16)


def _forget_call(f_logit, b_pad, n_heads):
    nb, seq, _ = f_logit.shape
    assert n_heads & (n_heads - 1) == 0 and 2 * N_SPLIT * n_heads <= LANES
    out = jax.ShapeDtypeStruct((nb, seq, LANES), BF16)
    spec = pl.BlockSpec((1, seq, LANES), lambda b: (b, 0, 0))
    return pl.pallas_call(
        functools.partial(_forget_kernel, seq=seq, n_heads=n_heads),
        grid=(nb,),
        in_specs=[spec, pl.BlockSpec((1, LANES), lambda b: (0, 0))],
        out_specs=[spec, spec],
        out_shape=[out, out],
        scratch_shapes=[pltpu.VMEM((2 * seq, LANES), F32)],
        compiler_params=_cparams(("parallel",), 40),
        name="forget_cumsum",
    )(f_logit, b_pad)


POOL_HALO = 16
POOL_ROWS = 512


def _pool_kernel(u_ref, w_ref, s_ref, o_ref, pad_scr, *, seq):
    g = pl.program_id(1)
    cg = u_ref.shape[2]
    pad_scr[pl.ds(0, POOL_HALO), :] = jnp.zeros((POOL_HALO, cg), F32)
    pad_scr[pl.ds(POOL_HALO, seq), :] = u_ref[0]
    w = w_ref[0].astype(BF16)
    scale = s_ref[...]
    for gi, win in enumerate(POOL_WINDOWS):
        @pl.when(g == gi)
        def _(win=win):
            for r0 in range(0, seq, POOL_ROWS):
                tok = pad_scr[pl.ds(POOL_HALO + r0, POOL_ROWS), :]
                tot = tok
                for dd in range(1, win):
                    tot = tot + pad_scr[pl.ds(POOL_HALO + r0 - dd, POOL_ROWS), :]
                pos = r0 + lax.broadcasted_iota(jnp.int32, (POOL_ROWS, cg), 0)
                cnt = jnp.minimum(pos + 1, win).astype(F32)
                p = tot / cnt - tok
                y = jnp.dot(p.astype(BF16), w, preferred_element_type=F32) * scale
                o_ref[0, pl.ds(r0, POOL_ROWS), :] = y.astype(BF16)


def _pool_call(u, pool_w, pool_scale):
    nb, seq, d_pool = u.shape
    ng, cg, _ = pool_w.shape
    return pl.pallas_call(
        functools.partial(_pool_kernel, seq=seq),
        grid=(nb, ng),
        in_specs=[pl.BlockSpec((1, seq, cg), lambda b, g: (b, 0, g)),
                  pl.BlockSpec((1, cg, cg), lambda b, g: (g, 0, 0)),
                  pl.BlockSpec((1, cg), lambda b, g: (0, g))],
        out_specs=pl.BlockSpec((1, seq, cg), lambda b, g: (b, 0, g)),
        out_shape=jax.ShapeDtypeStruct((nb, seq, d_pool), BF16),
        scratch_shapes=[pltpu.VMEM((POOL_HALO + seq, cg), F32)],
        compiler_params=_cparams(("parallel", "arbitrary"), 40),
        name="ms_pool",
    )(u, pool_w, pool_scale)


NEG_BIG = -1e30


def _lane_tile_reduce(x, op):
    out = x[:, :LANES]
    for t in range(1, x.shape[1] // LANES):
        out = op(out, x[:, t * LANES:(t + 1) * LANES])
    return out


def _attn_kernel(q_ref, qe_ref, k_ref, ke_ref, v_ref, o_ref, s_scr, p_scr, ke_scr,
                 *, tq, n_heads):
    seq = q_ref.shape[0]
    lane = lax.broadcasted_iota(jnp.int32, (seq, LANES), 1)
    mine = ((lane & (n_heads - 1)) == pl.program_id(1)) & (lane < 2 * N_SPLIT * n_heads)
    ke = ke_ref[0]
    ke_scr[...] = jnp.where(mine, ke, jnp.zeros_like(ke))
    for i in reversed(range(seq // tq)):
        slot = i % 2
        rows = pl.ds(i * tq, tq)
        qx = jnp.concatenate([q_ref[rows, :], qe_ref[0, rows, :]], axis=1)
        n_chunks = i + 1
        m_tile = None
        for c in range(n_chunks):
            cols = pl.ds(c * tq, tq)
            kx = jnp.concatenate([k_ref[cols, :], ke_scr[cols, :]], axis=1)
            s = lax.dot_general(qx, kx, (((1,), (1,)), ((), ())),
                                preferred_element_type=F32)
            if c == i:
                row = lax.broadcasted_iota(jnp.int32, (tq, tq), 0)
                col = lax.broadcasted_iota(jnp.int32, (tq, tq), 1)
                s = jnp.where(row >= col, s, NEG_BIG)
            s_scr[slot, :, cols] = s
            cm = _lane_tile_reduce(s, jnp.maximum)
            m_tile = cm if m_tile is None else jnp.maximum(m_tile, cm)
        m_row = jnp.max(m_tile, axis=-1, keepdims=True)
        l_tile = None
        for c in range(n_chunks):
            cols = pl.ds(c * tq, tq)
            p = jnp.exp2(s_scr[slot, :, cols] - m_row)
            p_scr[slot, :, cols] = p.astype(BF16)
            cl = _lane_tile_reduce(p, jnp.add)
            l_tile = cl if l_tile is None else l_tile + cl
        l_row = jnp.sum(l_tile, axis=-1, keepdims=True)
        kend = n_chunks * tq
        acc = jnp.dot(p_scr[slot, :, pl.ds(0, kend)], v_ref[pl.ds(0, kend), :],
                      preferred_element_type=F32)
        o_ref[rows, :] = (acc / l_row).astype(BF16)


def _attn_call(qkv, qe, ke, *, nb, seq, n_heads, tq=512):
    t = qkv.shape[0]
    d_attn = n_heads * HEAD_DIM
    head_spec = lambda off: pl.BlockSpec((seq, HEAD_DIM), lambda b, h: (b, off + h))
    bias_spec = pl.BlockSpec((1, seq, LANES), lambda b, h: (b, 0, 0))
    return pl.pallas_call(
        functools.partial(_attn_kernel, tq=tq, n_heads=n_heads),
        grid=(nb, n_heads),
        in_specs=[head_spec(0), bias_spec, head_spec(n_heads), bias_spec,
                  head_spec(2 * n_heads)],
        out_specs=head_spec(0),
        out_shape=jax.ShapeDtypeStruct((t, d_attn), BF16),
        scratch_shapes=[pltpu.VMEM((2, tq, seq), F32), pltpu.VMEM((2, tq, seq), BF16),
                        pltpu.VMEM((seq, LANES), BF16)],
        compiler_params=_cparams(("parallel", "parallel"), 56),
        name="fox_attention",
    )(qkv, qe, qkv, ke, qkv)


def _outproj_kernel(x_ref, attn_ref, pool_ref, wa_ref, wp_ref, gate_ref, o_ref, w_scr):
    da = attn_ref.shape[1]

    @pl.when(pl.program_id(0) == 0)
    def _():
        w_scr[:da, :] = wa_ref[...].astype(BF16)
        w_scr[da:, :] = wp_ref[...].astype(BF16)

    y = jnp.dot(attn_ref[...], w_scr[:da, :], preferred_element_type=F32)
    y = y + jnp.dot(pool_ref[...], w_scr[da:, :], preferred_element_type=F32)
    o_ref[...] = x_ref[...] + gate_ref[0] * y


def _outproj_call(x, attn, pool, w_out, gate, *, seq, tm=512):
    t, d = x.shape
    da = attn.shape[1]
    dp = pool.shape[1]
    assert da == dp and w_out.shape == (da + dp, d)
    tiles_per_seq = seq // tm
    w_spec = lambda half: pl.BlockSpec((da, d), lambda i: (half, 0),
                                       pipeline_mode=pl.Buffered(1))
    return pl.pallas_call(
        _outproj_kernel,
        grid=(t // tm,),
        in_specs=[pl.BlockSpec((tm, d), lambda i: (i, 0)),
                  pl.BlockSpec((tm, da), lambda i: (i, 0)),
                  pl.BlockSpec((tm, dp), lambda i: (i, 0)),
                  w_spec(0), w_spec(1),
                  pl.BlockSpec((1, 1, d), lambda i: (i // tiles_per_seq, 0, 0))],
        out_specs=pl.BlockSpec((tm, d), lambda i: (i, 0)),
        out_shape=jax.ShapeDtypeStruct((t, d), F32),
        scratch_shapes=[pltpu.VMEM((da + dp, d), BF16)],
        compiler_params=_cparams(("arbitrary",), 56),
        name="mixer_outproj",
    )(x, attn, pool, w_out, w_out, gate)


def kernel(x, c, w_ada, b_ada, ffn1_norm_g, ffn1_w_in, ffn1_w_out, mix_norm_g, w_in,
           b_forget, q_norm_g, k_norm_g, pool_w, pool_scale, w_out, ffn2_norm_g,
           ffn2_w_in, ffn2_w_out, final_norm_g):
    nb, seq, d = x.shape
    t = nb * seq
    n_heads = b_forget.shape[1]
    d_attn = n_heads * HEAD_DIM
    d_pool = pool_scale.shape[1]
    depth = w_ada.shape[0]
    xf = x.reshape(t, d)

    c_pad = jnp.pad(c, ((0, 8 - nb), (0, 0)))
    for l in range(depth):
        mod = _ada_call(c_pad, w_ada[l], b_ada[l].reshape(1, -1))[:nb]
        sh1, sc1, g1, sh2, sc2, g2, sh3, sc3, g3 = [
            mod[:, i * d:(i + 1) * d].reshape(nb, 1, d) for i in range(9)]

        xf = _ffn_resident(xf, ffn1_norm_g[l].reshape(1, d), sh1, sc1, g1, ffn1_w_in[l],
                           ffn1_w_out[l], None, seq=seq)

        wl = w_in[l]
        f0 = 3 * d_attn
        w_qkv = wl.astype(BF16)
        w_u = w_qkv[:, f0 + n_heads:]
        w_f = jnp.pad(w_qkv[:, f0:f0 + n_heads], ((0, 0), (0, LANES - n_heads)))
        q_gain = (q_norm_g[l] * (HEAD_DIM ** -0.5 * LOG2E)).reshape(1, HEAD_DIM)
        k_gain = k_norm_g[l].reshape(1, HEAD_DIM)
        qkv, u, f_logit = _inproj_call(
            xf, mix_norm_g[l].reshape(1, d), sh2, sc2, w_qkv, w_u, w_f, q_gain, k_gain,
            seq=seq, d_attn=d_attn)
        b_pad = jnp.pad(b_forget[l], (0, LANES - n_heads)).reshape(1, LANES)
        qe, ke = _forget_call(f_logit.reshape(nb, seq, LANES), b_pad, n_heads)
        attn = _attn_call(qkv, qe, ke, nb=nb, seq=seq, n_heads=n_heads)
        pool = _pool_call(u.reshape(nb, seq, d_pool), pool_w[l], pool_scale[l].reshape(1, -1))
        xf = _outproj_call(xf, attn, pool.reshape(t, d_pool), w_out[l], g2, seq=seq)

        last = l == depth - 1
        xf = _ffn_resident(xf, ffn2_norm_g[l].reshape(1, d), sh3, sc3, g3, ffn2_w_in[l],
                           ffn2_w_out[l], final_norm_g.reshape(1, d) if last else None,
                           seq=seq)
    return xf.reshape(nb, seq, d)
```

```python
import functools

import jax
import jax.numpy as jnp
from jax import lax
from jax.experimental import pallas as pl
from jax.experimental.pallas import tpu as pltpu

F32 = jnp.float32
BF16 = jnp.bfloat16

EPS = 1e-6
HEAD_DIM = 128
POOL_WINDOWS = (2, 4, 8, 16)
LANES = 128
MXU_DIM = 256
MIB = 1024 * 1024
LOG2E = 1.4426950408889634


def _cparams(dims, vmem_mib):
    return pltpu.CompilerParams(dimension_semantics=dims,
                                vmem_limit_bytes=vmem_mib * MIB)


def _norm_mod(x, g, shift, scale):
    ms = jnp.mean(x * x, axis=-1, keepdims=True)
    xn = x * lax.rsqrt(ms + EPS) * g
    return xn * (1.0 + scale) + shift


def _ada_kernel(c_ref, w_ref, b_ref, o_ref):
    c = c_ref[...]
    ca = c * (1.0 / (1.0 + jnp.exp(-c)))
    o_ref[...] = jnp.dot(ca.astype(BF16), w_ref[...].astype(BF16),
                         preferred_element_type=F32) + b_ref[...]


def _ada_call(c_pad, w, b, tn=1024):
    m, d = c_pad.shape
    n = w.shape[1]
    return pl.pallas_call(
        _ada_kernel,
        grid=(n // tn,),
        in_specs=[pl.BlockSpec((m, d), lambda j: (0, 0)),
                  pl.BlockSpec((d, tn), lambda j: (0, j)),
                  pl.BlockSpec((1, tn), lambda j: (0, j))],
        out_specs=pl.BlockSpec((m, tn), lambda j: (0, j)),
        out_shape=jax.ShapeDtypeStruct((m, n), F32),
        compiler_params=_cparams(("arbitrary",), 40),
        name="ada_mod",
    )(c_pad, w, b)


def _ffn_up_kernel(xn_ref, g_ref, shn_ref, scn_ref, a0_ref, a1_ref, b0_ref, b1_ref,
                   hid_ref, w_scr, h_scr, *, n_load, nblk, half_blocks):
    half = pl.program_id(0)
    s = pl.program_id(1)

    @pl.when(s < n_load)
    def _():
        k0 = half * half_blocks + 2 * s
        for q, ref in enumerate((a0_ref, a1_ref, b0_ref, b1_ref)):
            keep = k0 + (q % 2) < nblk
            w_scr[s, :, q * LANES:(q + 1) * LANES] = jnp.where(
                keep, ref[...], 0.0).astype(BF16)

    @pl.when(s == n_load - 1)
    def _():
        h_scr[0] = _norm_mod(xn_ref[...], g_ref[...], shn_ref[0], scn_ref[0]).astype(BF16)

    @pl.when(s >= n_load)
    def _():
        cur = (s - n_load) % 2
        for c in range(n_load):
            hw = jnp.dot(h_scr[cur], w_scr[c], preferred_element_type=F32)
            a = hw[:, :2 * LANES]
            b = hw[:, 2 * LANES:]
            hid_ref[:, c * 2 * LANES:(c + 1) * 2 * LANES] = (
                a * (1.0 / (1.0 + jnp.exp(-a))) * b).astype(BF16)
        h_scr[1 - cur] = _norm_mod(xn_ref[...], g_ref[...], shn_ref[0],
                                   scn_ref[0]).astype(BF16)


def _ffn_up_call(x, g, shift, scale, w_in, *, seq, nblk, nblk_pad, tm=512):
    t, d = x.shape
    nt = t // tm
    half_blocks = nblk_pad // 2
    n_load = half_blocks // 2
    tiles_per_seq = seq // tm
    nxt = lambda s: jnp.clip(s - n_load + 1, 0, nt - 1)
    blk = lambda c, s, q: jnp.minimum(c * half_blocks + 2 * jnp.minimum(s, n_load - 1) + q,
                                      nblk - 1)
    col = lambda off, q: pl.BlockSpec((d, LANES), lambda c, s: (0, off + blk(c, s, q)))
    mod_spec = pl.BlockSpec((1, 1, d), lambda c, s: (nxt(s) // tiles_per_seq, 0, 0))
    return pl.pallas_call(
        functools.partial(_ffn_up_kernel, n_load=n_load, nblk=nblk,
                          half_blocks=half_blocks),
        grid=(2, n_load + nt),
        in_specs=[pl.BlockSpec((tm, d), lambda c, s: (nxt(s), 0)),
                  pl.BlockSpec((1, d), lambda c, s: (0, 0)),
                  mod_spec, mod_spec,
                  col(0, 0), col(0, 1), col(nblk, 0), col(nblk, 1)],
        out_specs=pl.BlockSpec((tm, half_blocks * LANES),
                               lambda c, s: (jnp.maximum(s - n_load, 0), c)),
        out_shape=jax.ShapeDtypeStruct((t, nblk_pad * LANES), BF16),
        scratch_shapes=[pltpu.VMEM((n_load, d, 4 * LANES), BF16),
                        pltpu.VMEM((2, tm, d), BF16)],
        compiler_params=_cparams(("arbitrary", "arbitrary"), 58),
        name="ffn_up",
    )(x, g, shift, scale, w_in, w_in, w_in, w_in)


FFN_DOWN_LOAD_BLOCKS = 2
FFN_DOWN_COLS = 512


def _ffn_down_kernel(hid_ref, x_ref, gate_ref, *rest, n_load, nblk, final_norm):
    rest = list(rest)
    r_refs = [rest.pop(0) for _ in range(FFN_DOWN_LOAD_BLOCKS)]
    fg_ref = rest.pop(0) if final_norm else None
    o_ref, w_scr = rest
    s = pl.program_id(0)
    d = o_ref.shape[1]

    @pl.when(s < n_load)
    def _():
        for q, ref in enumerate(r_refs):
            k = FFN_DOWN_LOAD_BLOCKS * s + q
            rows = pl.ds(pl.multiple_of(k * LANES, LANES), LANES)
            w_scr[rows, :] = jnp.where(k < nblk, ref[...], 0.0).astype(BF16)

    @pl.when(s >= n_load)
    def _():
        ssq = None
        for n in range(d // FFN_DOWN_COLS):
            cols = slice(n * FFN_DOWN_COLS, (n + 1) * FFN_DOWN_COLS)
            acc = jnp.dot(hid_ref[...], w_scr[:, cols], preferred_element_type=F32)
            y = x_ref[:, cols] + 0.5 * gate_ref[0, :, cols] * acc
            if final_norm:
                part = jnp.sum(y * y, axis=-1, keepdims=True)
                ssq = part if ssq is None else ssq + part
            o_ref[:, cols] = y
        if final_norm:
            o_ref[...] = o_ref[...] * lax.rsqrt(ssq * (1.0 / d) + EPS) * fg_ref[...]


def _ffn_down_call(hidden, x, gate, w_out, final_g, *, seq, nblk, nblk_pad, tm=512):
    t, d = x.shape
    nt = t // tm
    lb = FFN_DOWN_LOAD_BLOCKS
    n_load = nblk_pad // lb
    tiles_per_seq = seq // tm
    tile = lambda s: jnp.maximum(s - n_load, 0)
    row = lambda q: pl.BlockSpec(
        (LANES, d), lambda s: (jnp.minimum(lb * jnp.minimum(s, n_load - 1) + q, nblk - 1), 0))
    in_specs = [pl.BlockSpec((tm, nblk_pad * LANES), lambda s: (tile(s), 0)),
                pl.BlockSpec((tm, d), lambda s: (tile(s), 0)),
                pl.BlockSpec((1, 1, d), lambda s: (tile(s) // tiles_per_seq, 0, 0))]
    in_specs += [row(q) for q in range(lb)]
    args = [hidden, x, gate] + [w_out] * lb
    if final_g is not None:
        in_specs.append(pl.BlockSpec((1, d), lambda s: (0, 0)))
        args.append(final_g)
    return pl.pallas_call(
        functools.partial(_ffn_down_kernel, n_load=n_load, nblk=nblk,
                          final_norm=final_g is not None),
        grid=(n_load + nt,),
        in_specs=in_specs,
        out_specs=pl.BlockSpec((tm, d), lambda s: (tile(s), 0)),
        out_shape=jax.ShapeDtypeStruct((t, d), F32),
        scratch_shapes=[pltpu.VMEM((nblk_pad * LANES, d), BF16)],
        compiler_params=_cparams(("arbitrary",), 62),
        name="ffn_down_final" if final_g is not None else "ffn_down",
    )(*args)


FFN_BLOCK_ALIGN = 4


def _ffn_resident(x, g, shift, scale, gate, w_in, w_out, final_g, *, seq):
    nblk = w_out.shape[0] // LANES
    nblk_pad = -(-nblk // FFN_BLOCK_ALIGN) * FFN_BLOCK_ALIGN
    hidden = _ffn_up_call(x, g, shift, scale, w_in, seq=seq, nblk=nblk, nblk_pad=nblk_pad)
    return _ffn_down_call(hidden, x, gate, w_out, final_g, seq=seq, nblk=nblk,
                          nblk_pad=nblk_pad)


def _inproj_kernel(xn_ref, g_ref, shn_ref, scn_ref, w_ref, wu_ref, wf_ref, qg_ref, kg_ref,
                   qkv_ref, u_ref, f_ref, h_scr, *, d_attn):
    s = pl.program_id(0)

    @pl.when(s == 0)
    def _():
        h_scr[0] = _norm_mod(xn_ref[...], g_ref[...], shn_ref[0], scn_ref[0]).astype(BF16)

    @pl.when(s >= 1)
    def _():
        cur = (s - 1) % 2

        def chunks(src_ref, start, width):
            for c0 in range(start, start + width, MXU_DIM):
                yield c0 - start, jnp.dot(h_scr[cur], src_ref[:, c0:c0 + MXU_DIM],
                                          preferred_element_type=F32)

        f_ref[...] = jnp.dot(h_scr[cur], wf_ref[...], preferred_element_type=F32)
        for part, gain_ref in enumerate((qg_ref, kg_ref)):
            gain = gain_ref[...]
            for off, acc in chunks(w_ref, part * d_attn, d_attn):
                for hh in range(MXU_DIM // HEAD_DIM):
                    xh = acc[:, hh * HEAD_DIM:(hh + 1) * HEAD_DIM]
                    ms = jnp.mean(xh * xh, axis=-1, keepdims=True)
                    lo = part * d_attn + off + hh * HEAD_DIM
                    qkv_ref[:, lo:lo + HEAD_DIM] = (
                        xh * lax.rsqrt(ms + EPS) * gain).astype(BF16)
        for off, acc in chunks(w_ref, 2 * d_attn, d_attn):
            lo = 2 * d_attn + off
            qkv_ref[:, lo:lo + MXU_DIM] = acc.astype(BF16)
        for off, acc in chunks(wu_ref, 0, wu_ref.shape[1]):
            u_ref[:, off:off + MXU_DIM] = acc
        h_scr[1 - cur] = _norm_mod(xn_ref[...], g_ref[...], shn_ref[0],
                                   scn_ref[0]).astype(BF16)


def _inproj_call(x, g, shift, scale, w_qkv, w_u, w_f, q_gain, k_gain, *, seq, d_attn,
                 tm=512):
    t, d = x.shape
    nt = t // tm
    d_pool = w_u.shape[1]
    assert w_qkv.shape[1] >= 3 * d_attn
    tiles_per_seq = seq // tm
    nxt = lambda s: jnp.minimum(s, nt - 1)
    tile = lambda s: jnp.maximum(s - 1, 0)
    mod_spec = pl.BlockSpec((1, 1, d), lambda s: (nxt(s) // tiles_per_seq, 0, 0))
    resident = lambda a: pl.BlockSpec(a.shape, lambda s: (0, 0),
                                      pipeline_mode=pl.Buffered(1))
    small = lambda a: pl.BlockSpec(a.shape, lambda s: (0, 0))
    return pl.pallas_call(
        functools.partial(_inproj_kernel, d_attn=d_attn),
        grid=(nt + 1,),
        in_specs=[pl.BlockSpec((tm, d), lambda s: (nxt(s), 0)),
                  small(g), mod_spec, mod_spec,
                  resident(w_qkv), resident(w_u), resident(w_f),
                  small(q_gain), small(k_gain)],
        out_specs=[pl.BlockSpec((tm, 3 * d_attn), lambda s: (tile(s), 0)),
                   pl.BlockSpec((tm, d_pool), lambda s: (tile(s), 0)),
                   pl.BlockSpec((tm, LANES), lambda s: (tile(s), 0))],
        out_shape=[jax.ShapeDtypeStruct((t, 3 * d_attn), BF16),
                   jax.ShapeDtypeStruct((t, d_pool), F32),
                   jax.ShapeDtypeStruct((t, LANES), F32)],
        scratch_shapes=[pltpu.VMEM((2, tm, d), BF16)],
        compiler_params=_cparams(("arbitrary",), 56),
        name="mixer_inproj",
    )(x, g, shift, scale, w_qkv, w_u, w_f, q_gain, k_gain)


N_SPLIT = 3


def _bias_selector(n_heads):
    rows = N_SPLIT * LANES
    hbits = n_heads.bit_length() - 1
    r = lax.broadcasted_iota(jnp.int32, (rows, 2 * LANES), 0)
    c = lax.broadcasted_iota(jnp.int32, (rows, 2 * LANES), 1)
    in_q = c < 2 * N_SPLIT * n_heads
    in_k = (c >= LANES) & (c < LANES + 2 * N_SPLIT * n_heads)
    ck = c - LANES
    slot_q = lax.shift_right_logical(c, hbits)
    slot_k = lax.shift_right_logical(jnp.maximum(ck, 0), hbits)
    head_q = c & (n_heads - 1)
    head_k = ck & (n_heads - 1)
    one_row = r == n_heads
    q_piece = in_q & (slot_q < N_SPLIT) & (r == slot_q * LANES + head_q)
    q_one = in_q & (slot_q >= N_SPLIT) & one_row
    k_one = in_k & (slot_k < N_SPLIT) & one_row
    k_piece = in_k & (slot_k >= N_SPLIT) & (r == (slot_k - N_SPLIT) * LANES + head_k)
    sel = jnp.where(q_piece | q_one | k_one, 1.0, 0.0) - jnp.where(k_piece, 1.0, 0.0)
    return sel.astype(BF16)


def _forget_kernel(f_ref, b_ref, qe_ref, ke_ref, pad_scr, *, seq, n_heads):
    z = f_ref[0] + b_ref[...]
    x = jnp.minimum(z, 0.0) - jnp.log1p(jnp.exp(-jnp.abs(z)))
    pad_scr[pl.ds(0, seq), :] = jnp.zeros((seq, LANES), F32)
    d = 1
    while d < seq:
        pad_scr[pl.ds(seq, seq), :] = x
        x = x + pad_scr[pl.ds(seq - d, seq), :]
        d *= 2
    rem = x * LOG2E
    lane = lax.broadcasted_iota(jnp.int32, (seq, LANES), 1)
    pieces = []
    for s in range(N_SPLIT):
        piece = rem.astype(BF16)
        rem = rem - piece.astype(F32)
        if s == 0:
            piece = jnp.where(lane == n_heads, jnp.ones_like(piece), piece)
        pieces.append(piece)
    e = jnp.dot(jnp.concatenate(pieces, axis=1), _bias_selector(n_heads),
                preferred_element_type=F32)
    qe_ref[0] = e[:, :LANES].astype(BF16)
    ke_ref[0] = e[:, LANES:].astype(BF16)


def _forget_call(f_logit, b_pad, n_heads):
    nb, seq, _ = f_logit.shape
    assert n_heads & (n_heads - 1) == 0 and 2 * N_SPLIT * n_heads <= LANES
    out = jax.ShapeDtypeStruct((nb, seq, LANES), BF16)
    spec = pl.BlockSpec((1, seq, LANES), lambda b: (b, 0, 0))
    return pl.pallas_call(
        functools.partial(_forget_kernel, seq=seq, n_heads=n_heads),
        grid=(nb,),
        in_specs=[spec, pl.BlockSpec((1, LANES), lambda b: (0, 0))],
        out_specs=[spec, spec],
        out_shape=[out, out],
        scratch_shapes=[pltpu.VMEM((2 * seq, LANES), F32)],
        compiler_params=_cparams(("parallel",), 40),
        name="forget_cumsum",
    )(f_logit, b_pad)


POOL_HALO = 16
POOL_ROWS = 512


def _pool_kernel(u_ref, w_ref, s_ref, o_ref, pad_scr, *, seq):
    g = pl.program_id(1)
    cg = u_ref.shape[2]
    pad_scr[pl.ds(0, POOL_HALO), :] = jnp.zeros((POOL_HALO, cg), F32)
    pad_scr[pl.ds(POOL_HALO, seq), :] = u_ref[0]
    w = w_ref[0].astype(BF16)
    scale = s_ref[...]
    for gi, win in enumerate(POOL_WINDOWS):
        @pl.when(g == gi)
        def _(win=win):
            for r0 in range(0, seq, POOL_ROWS):
                tok = pad_scr[pl.ds(POOL_HALO + r0, POOL_ROWS), :]
                tot = tok
                for dd in range(1, win):
                    tot = tot + pad_scr[pl.ds(POOL_HALO + r0 - dd, POOL_ROWS), :]
                pos = r0 + lax.broadcasted_iota(jnp.int32, (POOL_ROWS, cg), 0)
                cnt = jnp.minimum(pos + 1, win).astype(F32)
                p = tot / cnt - tok
                y = jnp.dot(p.astype(BF16), w, preferred_element_type=F32) * scale
                o_ref[0, pl.ds(r0, POOL_ROWS), :] = y.astype(BF16)


def _pool_call(u, pool_w, pool_scale):
    nb, seq, d_pool = u.shape
    ng, cg, _ = pool_w.shape
    return pl.pallas_call(
        functools.partial(_pool_kernel, seq=seq),
        grid=(nb, ng),
        in_specs=[pl.BlockSpec((1, seq, cg), lambda b, g: (b, 0, g)),
                  pl.BlockSpec((1, cg, cg), lambda b, g: (g, 0, 0)),
                  pl.BlockSpec((1, cg), lambda b, g: (0, g))],
        out_specs=pl.BlockSpec((1, seq, cg), lambda b, g: (b, 0, g)),
        out_shape=jax.ShapeDtypeStruct((nb, seq, d_pool), BF16),
        scratch_shapes=[pltpu.VMEM((POOL_HALO + seq, cg), F32)],
        compiler_params=_cparams(("parallel", "arbitrary"), 40),
        name="ms_pool",
    )(u, pool_w, pool_scale)


NEG_BIG = -1e30


def _lane_tile_reduce(x, op):
    out = x[:, :LANES]
    for t in range(1, x.shape[1] // LANES):
        out = op(out, x[:, t * LANES:(t + 1) * LANES])
    return out


def _attn_kernel(q_ref, qe_ref, k_ref, ke_ref, v_ref, o_ref, s_scr, p_scr, ke_scr,
                 *, tq, n_heads):
    seq = q_ref.shape[0]
    lane = lax.broadcasted_iota(jnp.int32, (seq, LANES), 1)
    mine = ((lane & (n_heads - 1)) == pl.program_id(1)) & (lane < 2 * N_SPLIT * n_heads)
    ke = ke_ref[0]
    ke_scr[...] = jnp.where(mine, ke, jnp.zeros_like(ke))
    for i in reversed(range(seq // tq)):
        slot = i % 2
        rows = pl.ds(i * tq, tq)
        qx = jnp.concatenate([q_ref[rows, :], qe_ref[0, rows, :]], axis=1)
        n_chunks = i + 1
        m_tile = None
        for c in range(n_chunks):
            cols = pl.ds(c * tq, tq)
            kx = jnp.concatenate([k_ref[cols, :], ke_scr[cols, :]], axis=1)
            s = lax.dot_general(qx, kx, (((1,), (1,)), ((), ())),
                                preferred_element_type=F32)
            if c == i:
                row = lax.broadcasted_iota(jnp.int32, (tq, tq), 0)
                col = lax.broadcasted_iota(jnp.int32, (tq, tq), 1)
                s = jnp.where(row >= col, s, NEG_BIG)
            s_scr[slot, :, cols] = s
            cm = _lane_tile_reduce(s, jnp.maximum)
            m_tile = cm if m_tile is None else jnp.maximum(m_tile, cm)
        m_row = jnp.max(m_tile, axis=-1, keepdims=True)
        l_tile = None
        for c in range(n_chunks):
            cols = pl.ds(c * tq, tq)
            p = jnp.exp2(s_scr[slot, :, cols] - m_row)
            p_scr[slot, :, cols] = p.astype(BF16)
            cl = _lane_tile_reduce(p, jnp.add)
            l_tile = cl if l_tile is None else l_tile + cl
        l_row = jnp.sum(l_tile, axis=-1, keepdims=True)
        kend = n_chunks * tq
        acc = jnp.dot(p_scr[slot, :, pl.ds(0, kend)], v_ref[pl.ds(0, kend), :],
                      preferred_element_type=F32)
        o_ref[rows, :] = (acc / l_row).astype(BF16)


def _attn_call(qkv, qe, ke, *, nb, seq, n_heads, tq=512):
    t = qkv.shape[0]
    d_attn = n_heads * HEAD_DIM
    head_spec = lambda off: pl.BlockSpec((seq, HEAD_DIM), lambda b, h: (b, off + h))
    bias_spec = pl.BlockSpec((1, seq, LANES), lambda b, h: (b, 0, 0))
    return pl.pallas_call(
        functools.partial(_attn_kernel, tq=tq, n_heads=n_heads),
        grid=(nb, n_heads),
        in_specs=[head_spec(0), bias_spec, head_spec(n_heads), bias_spec,
                  head_spec(2 * n_heads)],
        out_specs=head_spec(0),
        out_shape=jax.ShapeDtypeStruct((t, d_attn), BF16),
        scratch_shapes=[pltpu.VMEM((2, tq, seq), F32), pltpu.VMEM((2, tq, seq), BF16),
                        pltpu.VMEM((seq, LANES), BF16)],
        compiler_params=_cparams(("parallel", "parallel"), 56),
        name="fox_attention",
    )(qkv, qe, qkv, ke, qkv)


def _outproj_kernel(x_ref, attn_ref, pool_ref, wa_ref, wp_ref, gate_ref, o_ref, w_scr):
    da = attn_ref.shape[1]

    @pl.when(pl.program_id(0) == 0)
    def _():
        w_scr[:da, :] = wa_ref[...].astype(BF16)
        w_scr[da:, :] = wp_ref[...].astype(BF16)

    y = jnp.dot(attn_ref[...], w_scr[:da, :], preferred_element_type=F32)
    y = y + jnp.dot(pool_ref[...], w_scr[da:, :], preferred_element_type=F32)
    o_ref[...] = x_ref[...] + gate_ref[0] * y


def _outproj_call(x, attn, pool, w_out, gate, *, seq, tm=512):
    t, d = x.shape
    da = attn.shape[1]
    dp = pool.shape[1]
    assert da == dp and w_out.shape == (da + dp, d)
    tiles_per_seq = seq // tm
    w_spec = lambda half: pl.BlockSpec((da, d), lambda i: (half, 0),
                                       pipeline_mode=pl.Buffered(1))
    return pl.pallas_call(
        _outproj_kernel,
        grid=(t // tm,),
        in_specs=[pl.BlockSpec((tm, d), lambda i: (i, 0)),
                  pl.BlockSpec((tm, da), lambda i: (i, 0)),
                  pl.BlockSpec((tm, dp), lambda i: (i, 0)),
                  w_spec(0), w_spec(1),
                  pl.BlockSpec((1, 1, d), lambda i: (i // tiles_per_seq, 0, 0))],
        out_specs=pl.BlockSpec((tm, d), lambda i: (i, 0)),
        out_shape=jax.ShapeDtypeStruct((t, d), F32),
        scratch_shapes=[pltpu.VMEM((da + dp, d), BF16)],
        compiler_params=_cparams(("arbitrary",), 56),
        name="mixer_outproj",
    )(x, attn, pool, w_out, w_out, gate)


def kernel(x, c, w_ada, b_ada, ffn1_norm_g, ffn1_w_in, ffn1_w_out, mix_norm_g, w_in,
           b_forget, q_norm_g, k_norm_g, pool_w, pool_scale, w_out, ffn2_norm_g,
           ffn2_w_in, ffn2_w_out, final_norm_g):
    nb, seq, d = x.shape
    t = nb * seq
    n_heads = b_forget.shape[1]
    d_attn = n_heads * HEAD_DIM
    d_pool = pool_scale.shape[1]
    depth = w_ada.shape[0]
    xf = x.reshape(t, d)

    c_pad = jnp.pad(c, ((0, 8 - nb), (0, 0)))
    for l in range(depth):
        mod = _ada_call(c_pad, w_ada[l], b_ada[l].reshape(1, -1))[:nb]
        sh1, sc1, g1, sh2, sc2, g2, sh3, sc3, g3 = [
            mod[:, i * d:(i + 1) * d].reshape(nb, 1, d) for i in range(9)]

        xf = _ffn_resident(xf, ffn1_norm_g[l].reshape(1, d), sh1, sc1, g1, ffn1_w_in[l],
                           ffn1_w_out[l], None, seq=seq)

        wl = w_in[l]
        f0 = 3 * d_attn
        w_qkv = wl.astype(BF16)
        w_u = w_qkv[:, f0 + n_heads:]
        w_f = jnp.pad(w_qkv[:, f0:f0 + n_heads], ((0, 0), (0, LANES - n_heads)))
        q_gain = (q_norm_g[l] * (HEAD_DIM ** -0.5 * LOG2E)).reshape(1, HEAD_DIM)
        k_gain = k_norm_g[l].reshape(1, HEAD_DIM)
        qkv, u, f_logit = _inproj_call(
            xf, mix_norm_g[l].reshape(1, d), sh2, sc2, w_qkv, w_u, w_f, q_gain, k_gain,
            seq=seq, d_attn=d_attn)
        b_pad = jnp.pad(b_forget[l], (0, LANES - n_heads)).reshape(1, LANES)
        qe, ke = _forget_call(f_logit.reshape(nb, seq, LANES), b_pad, n_heads)
        attn = _attn_call(qkv, qe, ke, nb=nb, seq=seq, n_heads=n_heads)
        pool = _pool_call(u.reshape(nb, seq, d_pool), pool_w[l], pool_scale[l].reshape(1, -1))
        xf = _outproj_call(xf, attn, pool.reshape(t, d_pool), w_out[l], g2, seq=seq)

        last = l == depth - 1
        xf = _ffn_resident(xf, ffn2_norm_g[l].reshape(1, d), sh3, sc3, g3, ffn2_w_in[l],
                           ffn2_w_out[l], final_norm_g.reshape(1, d) if last else None,
                           seq=seq)
    return xf.reshape(nb, seq, d)
```

```python
import functools

import jax
import jax.numpy as jnp
from jax import lax
from jax.experimental import pallas as pl
from jax.experimental.pallas import tpu as pltpu

F32 = jnp.float32
BF16 = jnp.bfloat16

EPS = 1e-6
HEAD_DIM = 128
POOL_WINDOWS = (2, 4, 8, 16)
LANES = 128
MXU_DIM = 256
MIB = 1024 * 1024
LOG2E = 1.4426950408889634


def _cparams(dims, vmem_mib):
    return pltpu.CompilerParams(dimension_semantics=dims,
                                vmem_limit_bytes=vmem_mib * MIB)


def _norm_mod(x, g, shift, scale):
    ms = jnp.mean(x * x, axis=-1, keepdims=True)
    xn = x * lax.rsqrt(ms + EPS) * g
    return xn * (1.0 + scale) + shift


NORM_PIECES = 8


def _norm_pieces(x_ref, g_ref, shift_ref, scale_ref, h_ref):
    rows = x_ref.shape[0] // NORM_PIECES
    zeros = []
    for k in range(NORM_PIECES):
        sl = pl.ds(k * rows, rows)
        h = _norm_mod(x_ref[sl, :], g_ref[...], shift_ref[0], scale_ref[0])
        h_ref[sl, :] = h.astype(BF16)
        zeros.append(jnp.sum(h, keepdims=True) * 0.0)
    return zeros


def _ada_kernel(c_ref, w_ref, b_ref, o_ref):
    c = c_ref[...]
    ca = c * (1.0 / (1.0 + jnp.exp(-c)))
    o_ref[...] = jnp.dot(ca.astype(BF16), w_ref[...].astype(BF16),
                         preferred_element_type=F32) + b_ref[...]


def _ada_call(c_pad, w, b, tn=1024):
    m, d = c_pad.shape
    n = w.shape[1]
    return pl.pallas_call(
        _ada_kernel,
        grid=(n // tn,),
        in_specs=[pl.BlockSpec((m, d), lambda j: (0, 0)),
                  pl.BlockSpec((d, tn), lambda j: (0, j)),
                  pl.BlockSpec((1, tn), lambda j: (0, j))],
        out_specs=pl.BlockSpec((m, tn), lambda j: (0, j)),
        out_shape=jax.ShapeDtypeStruct((m, n), F32),
        compiler_params=_cparams(("arbitrary",), 40),
        name="ada_mod",
    )(c_pad, w, b)


def _ffn_up_kernel(xn_ref, g_ref, shn_ref, scn_ref, a0_ref, a1_ref, b0_ref, b1_ref,
                   hid_ref, w_scr, h0_scr, h1_scr, *, n_load, nblk, half_blocks):
    half = pl.program_id(0)
    s = pl.program_id(1)

    def next_norm():
        return _norm_mod(xn_ref[...], g_ref[...], shn_ref[0], scn_ref[0]).astype(BF16)

    @pl.when(s < n_load)
    def _():
        k0 = half * half_blocks + 2 * s
        for q, ref in enumerate((a0_ref, a1_ref, b0_ref, b1_ref)):
            keep = k0 + (q % 2) < nblk
            w_scr[s, :, q * LANES:(q + 1) * LANES] = jnp.where(
                keep, ref[...], 0.0).astype(BF16)

    @pl.when(s == n_load - 1)
    def _():
        h0_scr[...] = next_norm()

    def project(h_ref, hn_ref):
        zeros = _norm_pieces(xn_ref, g_ref, shn_ref, scn_ref, hn_ref)
        first = n_load - len(zeros)
        for c in range(n_load):
            hw = jnp.dot(h_ref[...], w_scr[c], preferred_element_type=F32)
            a = hw[:, :2 * LANES]
            b = hw[:, 2 * LANES:]
            if c >= first:
                a = a + zeros[c - first]
            hid_ref[:, c * 2 * LANES:(c + 1) * 2 * LANES] = (
                a * (1.0 / (1.0 + jnp.exp(-a))) * b).astype(BF16)

    odd = (s - n_load) % 2

    @pl.when(jnp.logical_and(s >= n_load, odd == 0))
    def _():
        project(h0_scr, h1_scr)

    @pl.when(jnp.logical_and(s >= n_load, odd == 1))
    def _():
        project(h1_scr, h0_scr)


def _ffn_up_call(x, g, shift, scale, w_in, *, seq, nblk, nblk_pad, tm=512):
    t, d = x.shape
    nt = t // tm
    half_blocks = nblk_pad // 2
    n_load = half_blocks // 2
    tiles_per_seq = seq // tm
    nxt = lambda s: jnp.clip(s - n_load + 1, 0, nt - 1)
    blk = lambda c, s, q: jnp.minimum(c * half_blocks + 2 * jnp.minimum(s, n_load - 1) + q,
                                      nblk - 1)
    col = lambda off, q: pl.BlockSpec((d, LANES), lambda c, s: (0, off + blk(c, s, q)))
    mod_spec = pl.BlockSpec((1, 1, d), lambda c, s: (nxt(s) // tiles_per_seq, 0, 0))
    return pl.pallas_call(
        functools.partial(_ffn_up_kernel, n_load=n_load, nblk=nblk,
                          half_blocks=half_blocks),
        grid=(2, n_load + nt),
        in_specs=[pl.BlockSpec((tm, d), lambda c, s: (nxt(s), 0)),
                  pl.BlockSpec((1, d), lambda c, s: (0, 0)),
                  mod_spec, mod_spec,
                  col(0, 0), col(0, 1), col(nblk, 0), col(nblk, 1)],
        out_specs=pl.BlockSpec((tm, half_blocks * LANES),
                               lambda c, s: (jnp.maximum(s - n_load, 0), c)),
        out_shape=jax.ShapeDtypeStruct((t, nblk_pad * LANES), BF16),
        scratch_shapes=[pltpu.VMEM((n_load, d, 4 * LANES), BF16),
                        pltpu.VMEM((tm, d), BF16), pltpu.VMEM((tm, d), BF16)],
        compiler_params=_cparams(("arbitrary", "arbitrary"), 58),
        name="ffn_up",
    )(x, g, shift, scale, w_in, w_in, w_in, w_in)


FFN_DOWN_LOAD_BLOCKS = 2
FFN_DOWN_COLS = 512


def _ffn_down_kernel(hid_ref, x_ref, gate_ref, *rest, n_load, nblk, final_norm):
    rest = list(rest)
    r_refs = [rest.pop(0) for _ in range(FFN_DOWN_LOAD_BLOCKS)]
    fg_ref = rest.pop(0) if final_norm else None
    o_ref, w_scr = rest
    s = pl.program_id(0)
    d = o_ref.shape[1]

    @pl.when(s < n_load)
    def _():
        for q, ref in enumerate(r_refs):
            k = FFN_DOWN_LOAD_BLOCKS * s + q
            rows = pl.ds(pl.multiple_of(k * LANES, LANES), LANES)
            w_scr[rows, :] = jnp.where(k < nblk, ref[...], 0.0).astype(BF16)

    @pl.when(s >= n_load)
    def _():
        ssq = None
        for n in range(d // FFN_DOWN_COLS):
            cols = slice(n * FFN_DOWN_COLS, (n + 1) * FFN_DOWN_COLS)
            acc = jnp.dot(hid_ref[...], w_scr[:, cols], preferred_element_type=F32)
            y = x_ref[:, cols] + 0.5 * gate_ref[0, :, cols] * acc
            if final_norm:
                part = jnp.sum(y * y, axis=-1, keepdims=True)
                ssq = part if ssq is None else ssq + part
            o_ref[:, cols] = y
        if final_norm:
            o_ref[...] = o_ref[...] * lax.rsqrt(ssq * (1.0 / d) + EPS) * fg_ref[...]


def _ffn_down_call(hidden, x, gate, w_out, final_g, *, seq, nblk, nblk_pad, tm=512):
    t, d = x.shape
    nt = t // tm
    lb = FFN_DOWN_LOAD_BLOCKS
    n_load = nblk_pad // lb
    tiles_per_seq = seq // tm
    tile = lambda s: jnp.maximum(s - n_load, 0)
    row = lambda q: pl.BlockSpec(
        (LANES, d), lambda s: (jnp.minimum(lb * jnp.minimum(s, n_load - 1) + q, nblk - 1), 0))
    in_specs = [pl.BlockSpec((tm, nblk_pad * LANES), lambda s: (tile(s), 0)),
                pl.BlockSpec((tm, d), lambda s: (tile(s), 0)),
                pl.BlockSpec((1, 1, d), lambda s: (tile(s) // tiles_per_seq, 0, 0))]
    in_specs += [row(q) for q in range(lb)]
    args = [hidden, x, gate] + [w_out] * lb
    if final_g is not None:
        in_specs.append(pl.BlockSpec((1, d), lambda s: (0, 0)))
        args.append(final_g)
    return pl.pallas_call(
        functools.partial(_ffn_down_kernel, n_load=n_load, nblk=nblk,
                          final_norm=final_g is not None),
        grid=(n_load + nt,),
        in_specs=in_specs,
        out_specs=pl.BlockSpec((tm, d), lambda s: (tile(s), 0)),
        out_shape=jax.ShapeDtypeStruct((t, d), F32),
        scratch_shapes=[pltpu.VMEM((nblk_pad * LANES, d), BF16)],
        compiler_params=_cparams(("arbitrary",), 62),
        name="ffn_down_final" if final_g is not None else "ffn_down",
    )(*args)


FFN_BLOCK_ALIGN = 4


def _ffn_resident(x, g, shift, scale, gate, w_in, w_out, final_g, *, seq):
    nblk = w_out.shape[0] // LANES
    nblk_pad = -(-nblk // FFN_BLOCK_ALIGN) * FFN_BLOCK_ALIGN
    hidden = _ffn_up_call(x, g, shift, scale, w_in, seq=seq, nblk=nblk, nblk_pad=nblk_pad)
    return _ffn_down_call(hidden, x, gate, w_out, final_g, seq=seq, nblk=nblk,
                          nblk_pad=nblk_pad)


def _inproj_kernel(xn_ref, g_ref, shn_ref, scn_ref, w_ref, wu_ref, wf_ref, qg_ref, kg_ref,
                   qkv_ref, u_ref, f_ref, h0_scr, h1_scr, *, d_attn):
    s = pl.program_id(0)

    def next_norm():
        return _norm_mod(xn_ref[...], g_ref[...], shn_ref[0], scn_ref[0]).astype(BF16)

    @pl.when(s == 0)
    def _():
        h0_scr[...] = next_norm()

    def project(h_ref, hn_ref):
        def chunks(src_ref, start, width):
            for c0 in range(start, start + width, MXU_DIM):
                yield c0 - start, jnp.dot(h_ref[...], src_ref[:, c0:c0 + MXU_DIM],
                                          preferred_element_type=F32)

        f_ref[...] = jnp.dot(h_ref[...], wf_ref[...], preferred_element_type=F32)
        for part, gain_ref in enumerate((qg_ref, kg_ref)):
            gain = gain_ref[...]
            for off, acc in chunks(w_ref, part * d_attn, d_attn):
                for hh in range(MXU_DIM // HEAD_DIM):
                    xh = acc[:, hh * HEAD_DIM:(hh + 1) * HEAD_DIM]
                    ms = jnp.mean(xh * xh, axis=-1, keepdims=True)
                    lo = part * d_attn + off + hh * HEAD_DIM
                    qkv_ref[:, lo:lo + HEAD_DIM] = (
                        xh * lax.rsqrt(ms + EPS) * gain).astype(BF16)
        for off, acc in chunks(w_ref, 2 * d_attn, d_attn):
            lo = 2 * d_attn + off
            qkv_ref[:, lo:lo + MXU_DIM] = acc.astype(BF16)
        for off, acc in chunks(wu_ref, 0, wu_ref.shape[1]):
            u_ref[:, off:off + MXU_DIM] = acc
        hn_ref[...] = next_norm()

    odd = (s - 1) % 2

    @pl.when(jnp.logical_and(s >= 1, odd == 0))
    def _():
        project(h0_scr, h1_scr)

    @pl.when(jnp.logical_and(s >= 1, odd == 1))
    def _():
        project(h1_scr, h0_scr)


def _inproj_call(x, g, shift, scale, w_qkv, w_u, w_f, q_gain, k_gain, *, seq, d_attn,
                 tm=512):
    t, d = x.shape
    nt = t // tm
    d_pool = w_u.shape[1]
    assert w_qkv.shape[1] >= 3 * d_attn
    tiles_per_seq = seq // tm
    nxt = lambda s: jnp.minimum(s, nt - 1)
    tile = lambda s: jnp.maximum(s - 1, 0)
    mod_spec = pl.BlockSpec((1, 1, d), lambda s: (nxt(s) // tiles_per_seq, 0, 0))
    resident = lambda a: pl.BlockSpec(a.shape, lambda s: (0, 0),
                                      pipeline_mode=pl.Buffered(1))
    small = lambda a: pl.BlockSpec(a.shape, lambda s: (0, 0))
    return pl.pallas_call(
        functools.partial(_inproj_kernel, d_attn=d_attn),
        grid=(nt + 1,),
        in_specs=[pl.BlockSpec((tm, d), lambda s: (nxt(s), 0)),
                  small(g), mod_spec, mod_spec,
                  resident(w_qkv), resident(w_u), resident(w_f),
                  small(q_gain), small(k_gain)],
        out_specs=[pl.BlockSpec((tm, 3 * d_attn), lambda s: (tile(s), 0)),
                   pl.BlockSpec((tm, d_pool), lambda s: (tile(s), 0)),
                   pl.BlockSpec((tm, LANES), lambda s: (tile(s), 0))],
        out_shape=[jax.ShapeDtypeStruct((t, 3 * d_attn), BF16),
                   jax.ShapeDtypeStruct((t, d_pool), F32),
                   jax.ShapeDtypeStruct((t, LANES), F32)],
        scratch_shapes=[pltpu.VMEM((tm, d), BF16), pltpu.VMEM((tm, d), BF16)],
        compiler_params=_cparams(("arbitrary",), 56),
        name="mixer_inproj",
    )(x, g, shift, scale, w_qkv, w_u, w_f, q_gain, k_gain)


N_SPLIT = 3


def _bias_selector(n_heads):
    rows = N_SPLIT * LANES
    hbits = n_heads.bit_length() - 1
    r = lax.broadcasted_iota(jnp.int32, (rows, 2 * LANES), 0)
    c = lax.broadcasted_iota(jnp.int32, (rows, 2 * LANES), 1)
    in_q = c < 2 * N_SPLIT * n_heads
    in_k = (c >= LANES) & (c < LANES + 2 * N_SPLIT * n_heads)
    ck = c - LANES
    slot_q = lax.shift_right_logical(c, hbits)
    slot_k = lax.shift_right_logical(jnp.maximum(ck, 0), hbits)
    head_q = c & (n_heads - 1)
    head_k = ck & (n_heads - 1)
    one_row = r == n_heads
    q_piece = in_q & (slot_q < N_SPLIT) & (r == slot_q * LANES + head_q)
    q_one = in_q & (slot_q >= N_SPLIT) & one_row
    k_one = in_k & (slot_k < N_SPLIT) & one_row
    k_piece = in_k & (slot_k >= N_SPLIT) & (r == (slot_k - N_SPLIT) * LANES + head_k)
    sel = jnp.where(q_piece | q_one | k_one, 1.0, 0.0) - jnp.where(k_piece, 1.0, 0.0)
    return sel.astype(BF16)


def _forget_kernel(f_ref, b_ref, qe_ref, ke_ref, pad_scr, *, seq, n_heads):
    z = f_ref[0] + b_ref[...]
    x = jnp.minimum(z, 0.0) - jnp.log1p(jnp.exp(-jnp.abs(z)))
    pad_scr[pl.ds(0, seq), :] = jnp.zeros((seq, LANES), F32)
    d = 1
    while d < seq:
        pad_scr[pl.ds(seq, seq), :] = x
        x = x + pad_scr[pl.ds(seq - d, seq), :]
        d *= 2
    rem = x * LOG2E
    lane = lax.broadcasted_iota(jnp.int32, (seq, LANES), 1)
    pieces = []
    for s in range(N_SPLIT):
        piece = rem.astype(BF16)
        rem = rem - piece.astype(F32)
        if s == 0:
            piece = jnp.where(lane == n_heads, jnp.ones_like(piece), piece)
        pieces.append(piece)
    e = jnp.dot(jnp.concatenate(pieces, axis=1), _bias_selector(n_heads),
                preferred_element_type=F32)
    qe_ref[0] = e[:, :LANES].astype(BF16)
    ke_ref[0] = e[:, LANES:].astype(BF16)


def _forget_call(f_logit, b_pad, n_heads):
    nb, seq, _ = f_logit.shape
    assert n_heads & (n_heads - 1) == 0 and 2 * N_SPLIT * n_heads <= LANES
    out = jax.ShapeDtypeStruct((nb, seq, LANES), BF16)
    spec = pl.BlockSpec((1, seq, LANES), lambda b: (b, 0, 0))
    return pl.pallas_call(
        functools.partial(_forget_kernel, seq=seq, n_heads=n_heads),
        grid=(nb,),
        in_specs=[spec, pl.BlockSpec((1, LANES), lambda b: (0, 0))],
        out_specs=[spec, spec],
        out_shape=[out, out],
        scratch_shapes=[pltpu.VMEM((2 * seq, LANES), F32)],
        compiler_params=_cparams(("parallel",), 40),
        name="forget_cumsum",
    )(f_logit, b_pad)


POOL_HALO = 16
POOL_ROWS = 512


def _pool_kernel(u_ref, w_ref, s_ref, o_ref, pad_scr, *, seq):
    g = pl.program_id(1)
    cg = u_ref.shape[2]
    pad_scr[pl.ds(0, POOL_HALO), :] = jnp.zeros((POOL_HALO, cg), F32)
    pad_scr[pl.ds(POOL_HALO, seq), :] = u_ref[0]
    w = w_ref[0].astype(BF16)
    scale = s_ref[...]
    for gi, win in enumerate(POOL_WINDOWS):
        @pl.when(g == gi)
        def _(win=win):
            for r0 in range(0, seq, POOL_ROWS):
                tok = pad_scr[pl.ds(POOL_HALO + r0, POOL_ROWS), :]
                tot = tok
                for dd in range(1, win):
                    tot = tot + pad_scr[pl.ds(POOL_HALO + r0 - dd, POOL_ROWS), :]
                pos = r0 + lax.broadcasted_iota(jnp.int32, (POOL_ROWS, cg), 0)
                cnt = jnp.minimum(pos + 1, win).astype(F32)
                p = tot / cnt - tok
                y = jnp.dot(p.astype(BF16), w, preferred_element_type=F32) * scale
                o_ref[0, pl.ds(r0, POOL_ROWS), :] = y.astype(BF16)


def _pool_call(u, pool_w, pool_scale):
    nb, seq, d_pool = u.shape
    ng, cg, _ = pool_w.shape
    return pl.pallas_call(
        functools.partial(_pool_kernel, seq=seq),
        grid=(nb, ng),
        in_specs=[pl.BlockSpec((1, seq, cg), lambda b, g: (b, 0, g)),
                  pl.BlockSpec((1, cg, cg), lambda b, g: (g, 0, 0)),
                  pl.BlockSpec((1, cg), lambda b, g: (0, g))],
        out_specs=pl.BlockSpec((1, seq, cg), lambda b, g: (b, 0, g)),
        out_shape=jax.ShapeDtypeStruct((nb, seq, d_pool), BF16),
        scratch_shapes=[pltpu.VMEM((POOL_HALO + seq, cg), F32)],
        compiler_params=_cparams(("parallel", "arbitrary"), 40),
        name="ms_pool",
    )(u, pool_w, pool_scale)


NEG_BIG = -1e30


def _lane_tile_reduce(x, op):
    out = x[:, :LANES]
    for t in range(1, x.shape[1] // LANES):
        out = op(out, x[:, t * LANES:(t + 1) * LANES])
    return out


def _attn_kernel(q_ref, qe_ref, k_ref, ke_ref, v_ref, o_ref, s_scr, p_scr, ke_scr,
                 *, tq, n_heads):
    seq = q_ref.shape[0]
    lane = lax.broadcasted_iota(jnp.int32, (seq, LANES), 1)
    mine = ((lane & (n_heads - 1)) == pl.program_id(1)) & (lane < 2 * N_SPLIT * n_heads)
    ke = ke_ref[0]
    ke_scr[...] = jnp.where(mine, ke, jnp.zeros_like(ke))
    for i in reversed(range(seq // tq)):
        slot = i % 2
        rows = pl.ds(i * tq, tq)
        qx = jnp.concatenate([q_ref[rows, :], qe_ref[0, rows, :]], axis=1)
        n_chunks = i + 1
        m_tile = None
        for c in range(n_chunks):
            cols = pl.ds(c * tq, tq)
            kx = jnp.concatenate([k_ref[cols, :], ke_scr[cols, :]], axis=1)
            s = lax.dot_general(qx, kx, (((1,), (1,)), ((), ())),
                                preferred_element_type=F32)
            if c == i:
                row = lax.broadcasted_iota(jnp.int32, (tq, tq), 0)
                col = lax.broadcasted_iota(jnp.int32, (tq, tq), 1)
                s = jnp.where(row >= col, s, NEG_BIG)
            s_scr[slot, :, cols] = s
            cm = _lane_tile_reduce(s, jnp.maximum)
            m_tile = cm if m_tile is None else jnp.maximum(m_tile, cm)
        m_row = jnp.max(m_tile, axis=-1, keepdims=True)
        l_tile = None
        for c in range(n_chunks):
            cols = pl.ds(c * tq, tq)
            p = jnp.exp2(s_scr[slot, :, cols] - m_row)
            p_scr[slot, :, cols] = p.astype(BF16)
            cl = _lane_tile_reduce(p, jnp.add)
            l_tile = cl if l_tile is None else l_tile + cl
        l_row = jnp.sum(l_tile, axis=-1, keepdims=True)
        kend = n_chunks * tq
        acc = jnp.dot(p_scr[slot, :, pl.ds(0, kend)], v_ref[pl.ds(0, kend), :],
                      preferred_element_type=F32)
        o_ref[rows, :] = (acc / l_row).astype(BF16)


def _attn_call(qkv, qe, ke, *, nb, seq, n_heads, tq=512):
    t = qkv.shape[0]
    d_attn = n_heads * HEAD_DIM
    head_spec = lambda off: pl.BlockSpec((seq, HEAD_DIM), lambda b, h: (b, off + h))
    bias_spec = pl.BlockSpec((1, seq, LANES), lambda b, h: (b, 0, 0))
    return pl.pallas_call(
        functools.partial(_attn_kernel, tq=tq, n_heads=n_heads),
        grid=(nb, n_heads),
        in_specs=[head_spec(0), bias_spec, head_spec(n_heads), bias_spec,
                  head_spec(2 * n_heads)],
        out_specs=head_spec(0),
        out_shape=jax.ShapeDtypeStruct((t, d_attn), BF16),
        scratch_shapes=[pltpu.VMEM((2, tq, seq), F32), pltpu.VMEM((2, tq, seq), BF16),
                        pltpu.VMEM((seq, LANES), BF16)],
        compiler_params=_cparams(("parallel", "parallel"), 56),
        name="fox_attention",
    )(qkv, qe, qkv, ke, qkv)


def _outproj_kernel(x_ref, attn_ref, pool_ref, wa_ref, wp_ref, gate_ref, o_ref, w_scr):
    da = attn_ref.shape[1]

    @pl.when(pl.program_id(0) == 0)
    def _():
        w_scr[:da, :] = wa_ref[...].astype(BF16)
        w_scr[da:, :] = wp_ref[...].astype(BF16)

    y = jnp.dot(attn_ref[...], w_scr[:da, :], preferred_element_type=F32)
    y = y + jnp.dot(pool_ref[...], w_scr[da:, :], preferred_element_type=F32)
    o_ref[...] = x_ref[...] + gate_ref[0] * y


def _outproj_call(x, attn, pool, w_out, gate, *, seq, tm=512):
    t, d = x.shape
    da = attn.shape[1]
    dp = pool.shape[1]
    assert da == dp and w_out.shape == (da + dp, d)
    tiles_per_seq = seq // tm
    w_spec = lambda half: pl.BlockSpec((da, d), lambda i: (half, 0),
                                       pipeline_mode=pl.Buffered(1))
    return pl.pallas_call(
        _outproj_kernel,
        grid=(t // tm,),
        in_specs=[pl.BlockSpec((tm, d), lambda i: (i, 0)),
                  pl.BlockSpec((tm, da), lambda i: (i, 0)),
                  pl.BlockSpec((tm, dp), lambda i: (i, 0)),
                  w_spec(0), w_spec(1),
                  pl.BlockSpec((1, 1, d), lambda i: (i // tiles_per_seq, 0, 0))],
        out_specs=pl.BlockSpec((tm, d), lambda i: (i, 0)),
        out_shape=jax.ShapeDtypeStruct((t, d), F32),
        scratch_shapes=[pltpu.VMEM((da + dp, d), BF16)],
        compiler_params=_cparams(("arbitrary",), 56),
        name="mixer_outproj",
    )(x, attn, pool, w_out, w_out, gate)


def kernel(x, c, w_ada, b_ada, ffn1_norm_g, ffn1_w_in, ffn1_w_out, mix_norm_g, w_in,
           b_forget, q_norm_g, k_norm_g, pool_w, pool_scale, w_out, ffn2_norm_g,
           ffn2_w_in, ffn2_w_out, final_norm_g):
    nb, seq, d = x.shape
    t = nb * seq
    n_heads = b_forget.shape[1]
    d_attn = n_heads * HEAD_DIM
    d_pool = pool_scale.shape[1]
    depth = w_ada.shape[0]
    xf = x.reshape(t, d)

    c_pad = jnp.pad(c, ((0, 8 - nb), (0, 0)))
    for l in range(depth):
        mod = _ada_call(c_pad, w_ada[l], b_ada[l].reshape(1, -1))[:nb]
        sh1, sc1, g1, sh2, sc2, g2, sh3, sc3, g3 = [
            mod[:, i * d:(i + 1) * d].reshape(nb, 1, d) for i in range(9)]

        xf = _ffn_resident(xf, ffn1_norm_g[l].reshape(1, d), sh1, sc1, g1, ffn1_w_in[l],
                           ffn1_w_out[l], None, seq=seq)

        wl = w_in[l]
        f0 = 3 * d_attn
        w_qkv = wl.astype(BF16)
        w_u = w_qkv[:, f0 + n_heads:]
        w_f = jnp.pad(w_qkv[:, f0:f0 + n_heads], ((0, 0), (0, LANES - n_heads)))
        q_gain = (q_norm_g[l] * (HEAD_DIM ** -0.5 * LOG2E)).reshape(1, HEAD_DIM)
        k_gain = k_norm_g[l].reshape(1, HEAD_DIM)
        qkv, u, f_logit = _inproj_call(
            xf, mix_norm_g[l].reshape(1, d), sh2, sc2, w_qkv, w_u, w_f, q_gain, k_gain,
            seq=seq, d_attn=d_attn)
        b_pad = jnp.pad(b_forget[l], (0, LANES - n_heads)).reshape(1, LANES)
        qe, ke = _forget_call(f_logit.reshape(nb, seq, LANES), b_pad, n_heads)
        attn = _attn_call(qkv, qe, ke, nb=nb, seq=seq, n_heads=n_heads)
        pool = _pool_call(u.reshape(nb, seq, d_pool), pool_w[l], pool_scale[l].reshape(1, -1))
        xf = _outproj_call(xf, attn, pool.reshape(t, d_pool), w_out[l], g2, seq=seq)

        last = l == depth - 1
        xf = _ffn_resident(xf, ffn2_norm_g[l].reshape(1, d), sh3, sc3, g3, ffn2_w_in[l],
                           ffn2_w_out[l], final_norm_g.reshape(1, d) if last else None,
                           seq=seq)
    return xf.reshape(nb, seq, d)
```

```python
import functools

import jax
import jax.numpy as jnp
from jax import lax
from jax.experimental import pallas as pl
from jax.experimental.pallas import tpu as pltpu

F32 = jnp.float32
BF16 = jnp.bfloat16

EPS = 1e-6
HEAD_DIM = 128
POOL_WINDOWS = (2, 4, 8, 16)
LANES = 128
MXU_DIM = 256
MIB = 1024 * 1024
LOG2E = 1.4426950408889634


def _cparams(dims, vmem_mib):
    return pltpu.CompilerParams(dimension_semantics=dims,
                                vmem_limit_bytes=vmem_mib * MIB)


def _norm_mod(x, g, shift, scale):
    ms = jnp.mean(x * x, axis=-1, keepdims=True)
    return x * lax.rsqrt(ms + EPS) * (g * (1.0 + scale)) + shift


def _ada_kernel(c_ref, w_ref, b_ref, o_ref):
    c = c_ref[...]
    ca = c * (1.0 / (1.0 + jnp.exp(-c)))
    o_ref[...] = jnp.dot(ca.astype(BF16), w_ref[...].astype(BF16),
                         preferred_element_type=F32) + b_ref[...]


def _ada_call(c_pad, w, b, tn=1024):
    m, d = c_pad.shape
    n = w.shape[1]
    return pl.pallas_call(
        _ada_kernel,
        grid=(n // tn,),
        in_specs=[pl.BlockSpec((m, d), lambda j: (0, 0)),
                  pl.BlockSpec((d, tn), lambda j: (0, j)),
                  pl.BlockSpec((1, tn), lambda j: (0, j))],
        out_specs=pl.BlockSpec((m, tn), lambda j: (0, j)),
        out_shape=jax.ShapeDtypeStruct((m, n), F32),
        compiler_params=_cparams(("arbitrary",), 40),
        name="ada_mod",
    )(c_pad, w, b)


def _ffn_up_kernel(xn_ref, g_ref, shn_ref, scn_ref, a0_ref, a1_ref, b0_ref, b1_ref,
                   hid_ref, w_scr, h_scr, *, n_load, nblk, half_blocks):
    half = pl.program_id(0)
    s = pl.program_id(1)

    @pl.when(s < n_load)
    def _():
        k0 = half * half_blocks + 2 * s
        for q, ref in enumerate((a0_ref, a1_ref, b0_ref, b1_ref)):
            keep = k0 + (q % 2) < nblk
            w_scr[s, :, q * LANES:(q + 1) * LANES] = jnp.where(
                keep, ref[...], 0.0).astype(BF16)

    @pl.when(s == n_load - 1)
    def _():
        h_scr[0] = _norm_mod(xn_ref[...], g_ref[...], shn_ref[0], scn_ref[0]).astype(BF16)

    @pl.when(s >= n_load)
    def _():
        cur = (s - n_load) % 2
        for c in range(n_load):
            hw = jnp.dot(h_scr[cur], w_scr[c], preferred_element_type=F32)
            a = hw[:, :2 * LANES]
            b = hw[:, 2 * LANES:]
            hid_ref[:, c * 2 * LANES:(c + 1) * 2 * LANES] = (
                a * (1.0 / (1.0 + jnp.exp(-a))) * b).astype(BF16)
        h_scr[1 - cur] = _norm_mod(xn_ref[...], g_ref[...], shn_ref[0],
                                   scn_ref[0]).astype(BF16)


def _ffn_up_call(x, g, shift, scale, w_in, *, seq, nblk, nblk_pad, tm=512):
    t, d = x.shape
    nt = t // tm
    half_blocks = nblk_pad // 2
    n_load = half_blocks // 2
    tiles_per_seq = seq // tm
    nxt = lambda s: jnp.clip(s - n_load + 1, 0, nt - 1)
    blk = lambda c, s, q: jnp.minimum(c * half_blocks + 2 * jnp.minimum(s, n_load - 1) + q,
                                      nblk - 1)
    col = lambda off, q: pl.BlockSpec((d, LANES), lambda c, s: (0, off + blk(c, s, q)))
    mod_spec = pl.BlockSpec((1, 1, d), lambda c, s: (nxt(s) // tiles_per_seq, 0, 0))
    return pl.pallas_call(
        functools.partial(_ffn_up_kernel, n_load=n_load, nblk=nblk,
                          half_blocks=half_blocks),
        grid=(2, n_load + nt),
        in_specs=[pl.BlockSpec((tm, d), lambda c, s: (nxt(s), 0)),
                  pl.BlockSpec((1, d), lambda c, s: (0, 0)),
                  mod_spec, mod_spec,
                  col(0, 0), col(0, 1), col(nblk, 0), col(nblk, 1)],
        out_specs=pl.BlockSpec((tm, half_blocks * LANES),
                               lambda c, s: (jnp.maximum(s - n_load, 0), c)),
        out_shape=jax.ShapeDtypeStruct((t, nblk_pad * LANES), BF16),
        scratch_shapes=[pltpu.VMEM((n_load, d, 4 * LANES), BF16),
                        pltpu.VMEM((2, tm, d), BF16)],
        compiler_params=_cparams(("arbitrary", "arbitrary"), 58),
        name="ffn_up",
    )(x, g, shift, scale, w_in, w_in, w_in, w_in)


FFN_DOWN_LOAD_BLOCKS = 2
FFN_DOWN_COLS = 512


def _ffn_down_kernel(hid_ref, x_ref, gate_ref, *rest, n_load, nblk, final_norm):
    rest = list(rest)
    r_refs = [rest.pop(0) for _ in range(FFN_DOWN_LOAD_BLOCKS)]
    fg_ref = rest.pop(0) if final_norm else None
    o_ref, w_scr = rest
    s = pl.program_id(0)
    d = o_ref.shape[1]

    @pl.when(s < n_load)
    def _():
        for q, ref in enumerate(r_refs):
            k = FFN_DOWN_LOAD_BLOCKS * s + q
            rows = pl.ds(pl.multiple_of(k * LANES, LANES), LANES)
            w_scr[rows, :] = jnp.where(k < nblk, ref[...], 0.0).astype(BF16)

    @pl.when(s >= n_load)
    def _():
        ssq = None
        for n in range(d // FFN_DOWN_COLS):
            cols = slice(n * FFN_DOWN_COLS, (n + 1) * FFN_DOWN_COLS)
            acc = jnp.dot(hid_ref[...], w_scr[:, cols], preferred_element_type=F32)
            y = x_ref[:, cols] + 0.5 * gate_ref[0, :, cols] * acc
            if final_norm:
                part = jnp.sum(y * y, axis=-1, keepdims=True)
                ssq = part if ssq is None else ssq + part
            o_ref[:, cols] = y
        if final_norm:
            o_ref[...] = o_ref[...] * lax.rsqrt(ssq * (1.0 / d) + EPS) * fg_ref[...]


def _ffn_down_call(hidden, x, gate, w_out, final_g, *, seq, nblk, nblk_pad, tm=512):
    t, d = x.shape
    nt = t // tm
    lb = FFN_DOWN_LOAD_BLOCKS
    n_load = nblk_pad // lb
    tiles_per_seq = seq // tm
    tile = lambda s: jnp.maximum(s - n_load, 0)
    row = lambda q: pl.BlockSpec(
        (LANES, d), lambda s: (jnp.minimum(lb * jnp.minimum(s, n_load - 1) + q, nblk - 1), 0))
    in_specs = [pl.BlockSpec((tm, nblk_pad * LANES), lambda s: (tile(s), 0)),
                pl.BlockSpec((tm, d), lambda s: (tile(s), 0)),
                pl.BlockSpec((1, 1, d), lambda s: (tile(s) // tiles_per_seq, 0, 0))]
    in_specs += [row(q) for q in range(lb)]
    args = [hidden, x, gate] + [w_out] * lb
    if final_g is not None:
        in_specs.append(pl.BlockSpec((1, d), lambda s: (0, 0)))
        args.append(final_g)
    return pl.pallas_call(
        functools.partial(_ffn_down_kernel, n_load=n_load, nblk=nblk,
                          final_norm=final_g is not None),
        grid=(n_load + nt,),
        in_specs=in_specs,
        out_specs=pl.BlockSpec((tm, d), lambda s: (tile(s), 0)),
        out_shape=jax.ShapeDtypeStruct((t, d), F32),
        scratch_shapes=[pltpu.VMEM((nblk_pad * LANES, d), BF16)],
        compiler_params=_cparams(("arbitrary",), 62),
        name="ffn_down_final" if final_g is not None else "ffn_down",
    )(*args)


FFN_BLOCK_ALIGN = 4


def _ffn_resident(x, g, shift, scale, gate, w_in, w_out, final_g, *, seq):
    nblk = w_out.shape[0] // LANES
    nblk_pad = -(-nblk // FFN_BLOCK_ALIGN) * FFN_BLOCK_ALIGN
    hidden = _ffn_up_call(x, g, shift, scale, w_in, seq=seq, nblk=nblk, nblk_pad=nblk_pad)
    return _ffn_down_call(hidden, x, gate, w_out, final_g, seq=seq, nblk=nblk,
                          nblk_pad=nblk_pad)


def _inproj_kernel(xn_ref, g_ref, shn_ref, scn_ref, w_ref, wu_ref, wf_ref, qg_ref, kg_ref,
                   qkv_ref, u_ref, f_ref, h_scr, *, d_attn):
    s = pl.program_id(0)

    @pl.when(s == 0)
    def _():
        h_scr[0] = _norm_mod(xn_ref[...], g_ref[...], shn_ref[0], scn_ref[0]).astype(BF16)

    @pl.when(s >= 1)
    def _():
        cur = (s - 1) % 2

        def project(w_rows):
            return lax.dot_general(h_scr[cur], w_rows, (((1,), (1,)), ((), ())),
                                   preferred_element_type=F32)

        def chunks(src_ref, start, width):
            for c0 in range(start, start + width, MXU_DIM):
                yield c0 - start, project(src_ref[c0:c0 + MXU_DIM, :])

        f_ref[...] = project(wf_ref[...])
        for part, gain_ref in enumerate((qg_ref, kg_ref)):
            gain = gain_ref[...]
            for off, acc in chunks(w_ref, part * d_attn, d_attn):
                for hh in range(MXU_DIM // HEAD_DIM):
                    xh = acc[:, hh * HEAD_DIM:(hh + 1) * HEAD_DIM]
                    ms = jnp.mean(xh * xh, axis=-1, keepdims=True)
                    lo = part * d_attn + off + hh * HEAD_DIM
                    qkv_ref[:, lo:lo + HEAD_DIM] = (
                        xh * lax.rsqrt(ms + EPS) * gain).astype(BF16)
        for off, acc in chunks(w_ref, 2 * d_attn, d_attn):
            lo = 2 * d_attn + off
            qkv_ref[:, lo:lo + MXU_DIM] = acc.astype(BF16)
        for off, acc in chunks(wu_ref, 0, wu_ref.shape[0]):
            u_ref[:, off:off + MXU_DIM] = acc
        h_scr[1 - cur] = _norm_mod(xn_ref[...], g_ref[...], shn_ref[0],
                                   scn_ref[0]).astype(BF16)


def _inproj_call(x, g, shift, scale, w_qkv, w_u, w_f, q_gain, k_gain, *, seq, d_attn,
                 tm=512):
    t, d = x.shape
    nt = t // tm
    d_pool = w_u.shape[0]
    assert w_qkv.shape[0] >= 3 * d_attn
    tiles_per_seq = seq // tm
    nxt = lambda s: jnp.minimum(s, nt - 1)
    tile = lambda s: jnp.maximum(s - 1, 0)
    mod_spec = pl.BlockSpec((1, 1, d), lambda s: (nxt(s) // tiles_per_seq, 0, 0))
    resident = lambda a: pl.BlockSpec(a.shape, lambda s: (0, 0),
                                      pipeline_mode=pl.Buffered(1))
    small = lambda a: pl.BlockSpec(a.shape, lambda s: (0, 0))
    return pl.pallas_call(
        functools.partial(_inproj_kernel, d_attn=d_attn),
        grid=(nt + 1,),
        in_specs=[pl.BlockSpec((tm, d), lambda s: (nxt(s), 0)),
                  small(g), mod_spec, mod_spec,
                  resident(w_qkv), resident(w_u), resident(w_f),
                  small(q_gain), small(k_gain)],
        out_specs=[pl.BlockSpec((tm, 3 * d_attn), lambda s: (tile(s), 0)),
                   pl.BlockSpec((tm, d_pool), lambda s: (tile(s), 0)),
                   pl.BlockSpec((tm, LANES), lambda s: (tile(s), 0))],
        out_shape=[jax.ShapeDtypeStruct((t, 3 * d_attn), BF16),
                   jax.ShapeDtypeStruct((t, d_pool), F32),
                   jax.ShapeDtypeStruct((t, LANES), F32)],
        scratch_shapes=[pltpu.VMEM((2, tm, d), BF16)],
        compiler_params=_cparams(("arbitrary",), 56),
        name="mixer_inproj",
    )(x, g, shift, scale, w_qkv, w_u, w_f, q_gain, k_gain)


N_SPLIT = 3


def _bias_selector(n_heads):
    rows = N_SPLIT * LANES
    hbits = n_heads.bit_length() - 1
    r = lax.broadcasted_iota(jnp.int32, (rows, 2 * LANES), 0)
    c = lax.broadcasted_iota(jnp.int32, (rows, 2 * LANES), 1)
    in_q = c < 2 * N_SPLIT * n_heads
    in_k = (c >= LANES) & (c < LANES + 2 * N_SPLIT * n_heads)
    ck = c - LANES
    slot_q = lax.shift_right_logical(c, hbits)
    slot_k = lax.shift_right_logical(jnp.maximum(ck, 0), hbits)
    head_q = c & (n_heads - 1)
    head_k = ck & (n_heads - 1)
    one_row = r == n_heads
    q_piece = in_q & (slot_q < N_SPLIT) & (r == slot_q * LANES + head_q)
    q_one = in_q & (slot_q >= N_SPLIT) & one_row
    k_one = in_k & (slot_k < N_SPLIT) & one_row
    k_piece = in_k & (slot_k >= N_SPLIT) & (r == (slot_k - N_SPLIT) * LANES + head_k)
    sel = jnp.where(q_piece | q_one | k_one, 1.0, 0.0) - jnp.where(k_piece, 1.0, 0.0)
    return sel.astype(BF16)


def _forget_kernel(f_ref, b_ref, qe_ref, ke_ref, pad_scr, *, seq, n_heads):
    z = f_ref[0] + b_ref[...]
    x = jnp.minimum(z, 0.0) - jnp.log1p(jnp.exp(-jnp.abs(z)))
    pad_scr[pl.ds(0, seq), :] = jnp.zeros((seq, LANES), F32)
    d = 1
    while d < seq:
        pad_scr[pl.ds(seq, seq), :] = x
        x = x + pad_scr[pl.ds(seq - d, seq), :]
        d *= 2
    rem = x * LOG2E
    lane = lax.broadcasted_iota(jnp.int32, (seq, LANES), 1)
    pieces = []
    for s in range(N_SPLIT):
        piece = rem.astype(BF16)
        rem = rem - piece.astype(F32)
        if s == 0:
            piece = jnp.where(lane == n_heads, jnp.ones_like(piece), piece)
        pieces.append(piece)
    e = jnp.dot(jnp.concatenate(pieces, axis=1), _bias_selector(n_heads),
                preferred_element_type=F32)
    qe_ref[0] = e[:, :LANES].astype(BF16)
    ke_ref[0] = e[:, LANES:].astype(BF16)


def _forget_call(f_logit, b_pad, n_heads):
    nb, seq, _ = f_logit.shape
    assert n_heads & (n_heads - 1) == 0 and 2 * N_SPLIT * n_heads <= LANES
    out = jax.ShapeDtypeStruct((nb, seq, LANES), BF16)
    spec = pl.BlockSpec((1, seq, LANES), lambda b: (b, 0, 0))
    return pl.pallas_call(
        functools.partial(_forget_kernel, seq=seq, n_heads=n_heads),
        grid=(nb,),
        in_specs=[spec, pl.BlockSpec((1, LANES), lambda b: (0, 0))],
        out_specs=[spec, spec],
        out_shape=[out, out],
        scratch_shapes=[pltpu.VMEM((2 * seq, LANES), F32)],
        compiler_params=_cparams(("parallel",), 40),
        name="forget_cumsum",
    )(f_logit, b_pad)


POOL_HALO = 16
POOL_ROWS = 512


def _pool_kernel(u_ref, w_ref, s_ref, o_ref, pad_scr, *, seq):
    g = pl.program_id(1)
    cg = u_ref.shape[2]
    pad_scr[pl.ds(0, POOL_HALO), :] = jnp.zeros((POOL_HALO, cg), F32)
    pad_scr[pl.ds(POOL_HALO, seq), :] = u_ref[0]
    w = w_ref[0].astype(BF16)
    scale = s_ref[...]
    for gi, win in enumerate(POOL_WINDOWS):
        @pl.when(g == gi)
        def _(win=win):
            for r0 in range(0, seq, POOL_ROWS):
                tok = pad_scr[pl.ds(POOL_HALO + r0, POOL_ROWS), :]
                tot = tok
                for dd in range(1, win):
                    tot = tot + pad_scr[pl.ds(POOL_HALO + r0 - dd, POOL_ROWS), :]
                pos = r0 + lax.broadcasted_iota(jnp.int32, (POOL_ROWS, cg), 0)
                cnt = jnp.minimum(pos + 1, win).astype(F32)
                p = tot / cnt - tok
                y = jnp.dot(p.astype(BF16), w, preferred_element_type=F32) * scale
                o_ref[0, pl.ds(r0, POOL_ROWS), :] = y.astype(BF16)


def _pool_call(u, pool_w, pool_scale):
    nb, seq, d_pool = u.shape
    ng, cg, _ = pool_w.shape
    return pl.pallas_call(
        functools.partial(_pool_kernel, seq=seq),
        grid=(nb, ng),
        in_specs=[pl.BlockSpec((1, seq, cg), lambda b, g: (b, 0, g)),
                  pl.BlockSpec((1, cg, cg), lambda b, g: (g, 0, 0)),
                  pl.BlockSpec((1, cg), lambda b, g: (0, g))],
        out_specs=pl.BlockSpec((1, seq, cg), lambda b, g: (b, 0, g)),
        out_shape=jax.ShapeDtypeStruct((nb, seq, d_pool), BF16),
        scratch_shapes=[pltpu.VMEM((POOL_HALO + seq, cg), F32)],
        compiler_params=_cparams(("parallel", "arbitrary"), 40),
        name="ms_pool",
    )(u, pool_w, pool_scale)


NEG_BIG = -1e30


def _lane_tile_reduce(x, op):
    out = x[:, :LANES]
    for t in range(1, x.shape[1] // LANES):
        out = op(out, x[:, t * LANES:(t + 1) * LANES])
    return out


def _attn_kernel(q_ref, qe_ref, k_ref, ke_ref, v_ref, o_ref, s_scr, p_scr, ke_scr,
                 *, tq, n_heads):
    seq = q_ref.shape[0]
    lane = lax.broadcasted_iota(jnp.int32, (seq, LANES), 1)
    mine = ((lane & (n_heads - 1)) == pl.program_id(1)) & (lane < 2 * N_SPLIT * n_heads)
    ke = ke_ref[0]
    ke_scr[...] = jnp.where(mine, ke, jnp.zeros_like(ke))
    for i in reversed(range(seq // tq)):
        slot = i % 2
        rows = pl.ds(i * tq, tq)
        qx = jnp.concatenate([q_ref[rows, :], qe_ref[0, rows, :]], axis=1)
        n_chunks = i + 1
        m_tile = None
        for c in range(n_chunks):
            cols = pl.ds(c * tq, tq)
            kx = jnp.concatenate([k_ref[cols, :], ke_scr[cols, :]], axis=1)
            s = lax.dot_general(qx, kx, (((1,), (1,)), ((), ())),
                                preferred_element_type=F32)
            if c == i:
                row = lax.broadcasted_iota(jnp.int32, (tq, tq), 0)
                col = lax.broadcasted_iota(jnp.int32, (tq, tq), 1)
                s = jnp.where(row >= col, s, NEG_BIG)
            s_scr[slot, :, cols] = s
            cm = _lane_tile_reduce(s, jnp.maximum)
            m_tile = cm if m_tile is None else jnp.maximum(m_tile, cm)
        m_row = jnp.max(m_tile, axis=-1, keepdims=True)
        l_tile = None
        for c in range(n_chunks):
            cols = pl.ds(c * tq, tq)
            p = jnp.exp2(s_scr[slot, :, cols] - m_row)
            p_scr[slot, :, cols] = p.astype(BF16)
            cl = _lane_tile_reduce(p, jnp.add)
            l_tile = cl if l_tile is None else l_tile + cl
        l_row = jnp.sum(l_tile, axis=-1, keepdims=True)
        kend = n_chunks * tq
        acc = jnp.dot(p_scr[slot, :, pl.ds(0, kend)], v_ref[pl.ds(0, kend), :],
                      preferred_element_type=F32)
        o_ref[rows, :] = (acc / l_row).astype(BF16)


def _attn_call(qkv, qe, ke, *, nb, seq, n_heads, tq=512):
    t = qkv.shape[0]
    d_attn = n_heads * HEAD_DIM
    head_spec = lambda off: pl.BlockSpec((seq, HEAD_DIM), lambda b, h: (b, off + h))
    bias_spec = pl.BlockSpec((1, seq, LANES), lambda b, h: (b, 0, 0))
    return pl.pallas_call(
        functools.partial(_attn_kernel, tq=tq, n_heads=n_heads),
        grid=(nb, n_heads),
        in_specs=[head_spec(0), bias_spec, head_spec(n_heads), bias_spec,
                  head_spec(2 * n_heads)],
        out_specs=head_spec(0),
        out_shape=jax.ShapeDtypeStruct((t, d_attn), BF16),
        scratch_shapes=[pltpu.VMEM((2, tq, seq), F32), pltpu.VMEM((2, tq, seq), BF16),
                        pltpu.VMEM((seq, LANES), BF16)],
        compiler_params=_cparams(("parallel", "parallel"), 56),
        name="fox_attention",
    )(qkv, qe, qkv, ke, qkv)


def _outproj_kernel(x_ref, attn_ref, pool_ref, wa_ref, wp_ref, gate_ref, o_ref, w_scr):
    da = attn_ref.shape[1]

    @pl.when(pl.program_id(0) == 0)
    def _():
        w_scr[:da, :] = wa_ref[...].astype(BF16)
        w_scr[da:, :] = wp_ref[...].astype(BF16)

    y = jnp.dot(attn_ref[...], w_scr[:da, :], preferred_element_type=F32)
    y = y + jnp.dot(pool_ref[...], w_scr[da:, :], preferred_element_type=F32)
    o_ref[...] = x_ref[...] + gate_ref[0] * y


def _outproj_call(x, attn, pool, w_out, gate, *, seq, tm=512):
    t, d = x.shape
    da = attn.shape[1]
    dp = pool.shape[1]
    assert da == dp and w_out.shape == (da + dp, d)
    tiles_per_seq = seq // tm
    w_spec = lambda half: pl.BlockSpec((da, d), lambda i: (half, 0),
                                       pipeline_mode=pl.Buffered(1))
    return pl.pallas_call(
        _outproj_kernel,
        grid=(t // tm,),
        in_specs=[pl.BlockSpec((tm, d), lambda i: (i, 0)),
                  pl.BlockSpec((tm, da), lambda i: (i, 0)),
                  pl.BlockSpec((tm, dp), lambda i: (i, 0)),
                  w_spec(0), w_spec(1),
                  pl.BlockSpec((1, 1, d), lambda i: (i // tiles_per_seq, 0, 0))],
        out_specs=pl.BlockSpec((tm, d), lambda i: (i, 0)),
        out_shape=jax.ShapeDtypeStruct((t, d), F32),
        scratch_shapes=[pltpu.VMEM((da + dp, d), BF16)],
        compiler_params=_cparams(("arbitrary",), 56),
        name="mixer_outproj",
    )(x, attn, pool, w_out, w_out, gate)


def kernel(x, c, w_ada, b_ada, ffn1_norm_g, ffn1_w_in, ffn1_w_out, mix_norm_g, w_in,
           b_forget, q_norm_g, k_norm_g, pool_w, pool_scale, w_out, ffn2_norm_g,
           ffn2_w_in, ffn2_w_out, final_norm_g):
    nb, seq, d = x.shape
    t = nb * seq
    n_heads = b_forget.shape[1]
    d_attn = n_heads * HEAD_DIM
    d_pool = pool_scale.shape[1]
    depth = w_ada.shape[0]
    xf = x.reshape(t, d)

    c_pad = jnp.pad(c, ((0, 8 - nb), (0, 0)))
    for l in range(depth):
        mod = _ada_call(c_pad, w_ada[l], b_ada[l].reshape(1, -1))[:nb]
        sh1, sc1, g1, sh2, sc2, g2, sh3, sc3, g3 = [
            mod[:, i * d:(i + 1) * d].reshape(nb, 1, d) for i in range(9)]

        xf = _ffn_resident(xf, ffn1_norm_g[l].reshape(1, d), sh1, sc1, g1, ffn1_w_in[l],
                           ffn1_w_out[l], None, seq=seq)

        wl = w_in[l]
        f0 = 3 * d_attn
        w_qkv = wl.T.astype(BF16)
        w_u = w_qkv[f0 + n_heads:]
        w_f = jnp.pad(w_qkv[f0:f0 + n_heads], ((0, LANES - n_heads), (0, 0)))
        q_gain = (q_norm_g[l] * (HEAD_DIM ** -0.5 * LOG2E)).reshape(1, HEAD_DIM)
        k_gain = k_norm_g[l].reshape(1, HEAD_DIM)
        qkv, u, f_logit = _inproj_call(
            xf, mix_norm_g[l].reshape(1, d), sh2, sc2, w_qkv, w_u, w_f, q_gain, k_gain,
            seq=seq, d_attn=d_attn)
        b_pad = jnp.pad(b_forget[l], (0, LANES - n_heads)).reshape(1, LANES)
        qe, ke = _forget_call(f_logit.reshape(nb, seq, LANES), b_pad, n_heads)
        attn = _attn_call(qkv, qe, ke, nb=nb, seq=seq, n_heads=n_heads)
        pool = _pool_call(u.reshape(nb, seq, d_pool), pool_w[l], pool_scale[l].reshape(1, -1))
        xf = _outproj_call(xf, attn, pool.reshape(t, d_pool), w_out[l], g2, seq=seq)

        last = l == depth - 1
        xf = _ffn_resident(xf, ffn2_norm_g[l].reshape(1, d), sh3, sc3, g3, ffn2_w_in[l],
                           ffn2_w_out[l], final_norm_g.reshape(1, d) if last else None,
                           seq=seq)
    return xf.reshape(nb, seq, d)
```

```python
import functools

import jax
import jax.numpy as jnp
from jax import lax
from jax.experimental import pallas as pl
from jax.experimental.pallas import tpu as pltpu

F32 = jnp.float32
BF16 = jnp.bfloat16

EPS = 1e-6
HEAD_DIM = 128
POOL_WINDOWS = (2, 4, 8, 16)
LANES = 128
MXU_DIM = 256
MIB = 1024 * 1024
LOG2E = 1.4426950408889634


def _cparams(dims, vmem_mib):
    return pltpu.CompilerParams(dimension_semantics=dims,
                                vmem_limit_bytes=vmem_mib * MIB)


def _norm_mod(x, g, shift, scale):
    ms = jnp.mean(x * x, axis=-1, keepdims=True)
    return x * lax.rsqrt(ms + EPS) * (g * (1.0 + scale)) + shift


NORM_PIECES = 8
SUBLANES = 8


def _norm_pieces(x_ref, g_ref, shift_ref, scale_ref, h_ref):
    rows = x_ref.shape[0] // NORM_PIECES
    zeros = []
    for k in range(NORM_PIECES):
        sl = pl.ds(k * rows, rows)
        h = _norm_mod(x_ref[sl, :], g_ref[...], shift_ref[0], scale_ref[0])
        h_ref[sl, :] = h.astype(BF16)
        acc = None
        for r in range(0, rows, SUBLANES):
            for c in range(0, h.shape[1], LANES):
                tile = h[r:r + SUBLANES, c:c + LANES]
                acc = tile if acc is None else acc + tile
        zeros.append(acc * 0.0)
    return zeros


def _anchor(a, zero):
    top = 2 * SUBLANES
    z = jnp.concatenate([zero, zero], axis=0)
    a_top = jnp.concatenate([a[:top, :LANES] + z, a[:top, LANES:]], axis=1)
    return jnp.concatenate([a_top, a[top:]], axis=0)


def _ada_kernel(c_ref, w_ref, b_ref, o_ref):
    c = c_ref[...]
    ca = c * (1.0 / (1.0 + jnp.exp(-c)))
    o_ref[...] = jnp.dot(ca.astype(BF16), w_ref[...].astype(BF16),
                         preferred_element_type=F32) + b_ref[...]


def _ada_call(c_pad, w, b, tn=1024):
    m, d = c_pad.shape
    n = w.shape[1]
    return pl.pallas_call(
        _ada_kernel,
        grid=(n // tn,),
        in_specs=[pl.BlockSpec((m, d), lambda j: (0, 0)),
                  pl.BlockSpec((d, tn), lambda j: (0, j)),
                  pl.BlockSpec((1, tn), lambda j: (0, j))],
        out_specs=pl.BlockSpec((m, tn), lambda j: (0, j)),
        out_shape=jax.ShapeDtypeStruct((m, n), F32),
        compiler_params=_cparams(("arbitrary",), 40),
        name="ada_mod",
    )(c_pad, w, b)


def _ffn_up_kernel(xn_ref, g_ref, shn_ref, scn_ref, a0_ref, a1_ref, b0_ref, b1_ref,
                   hid_ref, w_scr, h0_scr, h1_scr, *, n_load, nblk, half_blocks):
    half = pl.program_id(0)
    s = pl.program_id(1)

    @pl.when(s < n_load)
    def _():
        k0 = half * half_blocks + 2 * s
        for q, ref in enumerate((a0_ref, a1_ref, b0_ref, b1_ref)):
            keep = k0 + (q % 2) < nblk
            w_scr[s, :, q * LANES:(q + 1) * LANES] = jnp.where(
                keep, ref[...], 0.0).astype(BF16)

    @pl.when(s == n_load - 1)
    def _():
        h0_scr[...] = _norm_mod(xn_ref[...], g_ref[...], shn_ref[0],
                                scn_ref[0]).astype(BF16)

    def project(h_ref, hn_ref):
        zeros = _norm_pieces(xn_ref, g_ref, shn_ref, scn_ref, hn_ref)
        first = n_load - len(zeros)
        for c in range(n_load):
            hw = jnp.dot(h_ref[...], w_scr[c], preferred_element_type=F32)
            a = hw[:, :2 * LANES]
            b = hw[:, 2 * LANES:]
            if c >= first:
                a = _anchor(a, zeros[c - first])
            hid_ref[:, c * 2 * LANES:(c + 1) * 2 * LANES] = (
                a * (1.0 / (1.0 + jnp.exp(-a))) * b).astype(BF16)

    odd = (s - n_load) % 2

    @pl.when(jnp.logical_and(s >= n_load, odd == 0))
    def _():
        project(h0_scr, h1_scr)

    @pl.when(jnp.logical_and(s >= n_load, odd == 1))
    def _():
        project(h1_scr, h0_scr)


def _ffn_up_call(x, g, shift, scale, w_in, *, seq, nblk, nblk_pad, tm=512):
    t, d = x.shape
    nt = t // tm
    half_blocks = nblk_pad // 2
    n_load = half_blocks // 2
    tiles_per_seq = seq // tm
    nxt = lambda s: jnp.clip(s - n_load + 1, 0, nt - 1)
    blk = lambda c, s, q: jnp.minimum(c * half_blocks + 2 * jnp.minimum(s, n_load - 1) + q,
                                      nblk - 1)
    col = lambda off, q: pl.BlockSpec((d, LANES), lambda c, s: (0, off + blk(c, s, q)))
    mod_spec = pl.BlockSpec((1, 1, d), lambda c, s: (nxt(s) // tiles_per_seq, 0, 0))
    return pl.pallas_call(
        functools.partial(_ffn_up_kernel, n_load=n_load, nblk=nblk,
                          half_blocks=half_blocks),
        grid=(2, n_load + nt),
        in_specs=[pl.BlockSpec((tm, d), lambda c, s: (nxt(s), 0)),
                  pl.BlockSpec((1, d), lambda c, s: (0, 0)),
                  mod_spec, mod_spec,
                  col(0, 0), col(0, 1), col(nblk, 0), col(nblk, 1)],
        out_specs=pl.BlockSpec((tm, half_blocks * LANES),
                               lambda c, s: (jnp.maximum(s - n_load, 0), c)),
        out_shape=jax.ShapeDtypeStruct((t, nblk_pad * LANES), BF16),
        scratch_shapes=[pltpu.VMEM((n_load, d, 4 * LANES), BF16),
                        pltpu.VMEM((tm, d), BF16), pltpu.VMEM((tm, d), BF16)],
        compiler_params=_cparams(("arbitrary", "arbitrary"), 58),
        name="ffn_up",
    )(x, g, shift, scale, w_in, w_in, w_in, w_in)


FFN_DOWN_LOAD_BLOCKS = 2
FFN_DOWN_COLS = 512


def _ffn_down_kernel(hid_ref, x_ref, gate_ref, *rest, n_load, nblk, final_norm):
    rest = list(rest)
    r_refs = [rest.pop(0) for _ in range(FFN_DOWN_LOAD_BLOCKS)]
    fg_ref = rest.pop(0) if final_norm else None
    o_ref, w_scr = rest
    s = pl.program_id(0)
    d = o_ref.shape[1]

    @pl.when(s < n_load)
    def _():
        for q, ref in enumerate(r_refs):
            k = FFN_DOWN_LOAD_BLOCKS * s + q
            rows = pl.ds(pl.multiple_of(k * LANES, LANES), LANES)
            w_scr[rows, :] = jnp.where(k < nblk, ref[...], 0.0).astype(BF16)

    @pl.when(s >= n_load)
    def _():
        ssq = None
        for n in range(d // FFN_DOWN_COLS):
            cols = slice(n * FFN_DOWN_COLS, (n + 1) * FFN_DOWN_COLS)
            acc = jnp.dot(hid_ref[...], w_scr[:, cols], preferred_element_type=F32)
            y = x_ref[:, cols] + 0.5 * gate_ref[0, :, cols] * acc
            if final_norm:
                part = jnp.sum(y * y, axis=-1, keepdims=True)
                ssq = part if ssq is None else ssq + part
            o_ref[:, cols] = y
        if final_norm:
            o_ref[...] = o_ref[...] * lax.rsqrt(ssq * (1.0 / d) + EPS) * fg_ref[...]


def _ffn_down_call(hidden, x, gate, w_out, final_g, *, seq, nblk, nblk_pad, tm=512):
    t, d = x.shape
    nt = t // tm
    lb = FFN_DOWN_LOAD_BLOCKS
    n_load = nblk_pad // lb
    tiles_per_seq = seq // tm
    tile = lambda s: jnp.maximum(s - n_load, 0)
    row = lambda q: pl.BlockSpec(
        (LANES, d), lambda s: (jnp.minimum(lb * jnp.minimum(s, n_load - 1) + q, nblk - 1), 0))
    in_specs = [pl.BlockSpec((tm, nblk_pad * LANES), lambda s: (tile(s), 0)),
                pl.BlockSpec((tm, d), lambda s: (tile(s), 0)),
                pl.BlockSpec((1, 1, d), lambda s: (tile(s) // tiles_per_seq, 0, 0))]
    in_specs += [row(q) for q in range(lb)]
    args = [hidden, x, gate] + [w_out] * lb
    if final_g is not None:
        in_specs.append(pl.BlockSpec((1, d), lambda s: (0, 0)))
        args.append(final_g)
    return pl.pallas_call(
        functools.partial(_ffn_down_kernel, n_load=n_load, nblk=nblk,
                          final_norm=final_g is not None),
        grid=(n_load + nt,),
        in_specs=in_specs,
        out_specs=pl.BlockSpec((tm, d), lambda s: (tile(s), 0)),
        out_shape=jax.ShapeDtypeStruct((t, d), F32),
        scratch_shapes=[pltpu.VMEM((nblk_pad * LANES, d), BF16)],
        compiler_params=_cparams(("arbitrary",), 62),
        name="ffn_down_final" if final_g is not None else "ffn_down",
    )(*args)


FFN_BLOCK_ALIGN = 4


def _ffn_resident(x, g, shift, scale, gate, w_in, w_out, final_g, *, seq):
    nblk = w_out.shape[0] // LANES
    nblk_pad = -(-nblk // FFN_BLOCK_ALIGN) * FFN_BLOCK_ALIGN
    hidden = _ffn_up_call(x, g, shift, scale, w_in, seq=seq, nblk=nblk, nblk_pad=nblk_pad)
    return _ffn_down_call(hidden, x, gate, w_out, final_g, seq=seq, nblk=nblk,
                          nblk_pad=nblk_pad)


def _inproj_kernel(xn_ref, g_ref, shn_ref, scn_ref, w_ref, wu_ref, wf_ref, qg_ref, kg_ref,
                   qkv_ref, u_ref, f_ref, h_scr, *, d_attn):
    s = pl.program_id(0)

    @pl.when(s == 0)
    def _():
        h_scr[0] = _norm_mod(xn_ref[...], g_ref[...], shn_ref[0], scn_ref[0]).astype(BF16)

    @pl.when(s >= 1)
    def _():
        cur = (s - 1) % 2

        def project(w_rows):
            return lax.dot_general(h_scr[cur], w_rows, (((1,), (1,)), ((), ())),
                                   preferred_element_type=F32)

        def chunks(src_ref, start, width):
            for c0 in range(start, start + width, MXU_DIM):
                yield c0 - start, project(src_ref[c0:c0 + MXU_DIM, :])

        f_ref[...] = project(wf_ref[...])
        for part, gain_ref in enumerate((qg_ref, kg_ref)):
            gain = gain_ref[...]
            for off, acc in chunks(w_ref, part * d_attn, d_attn):
                for hh in range(MXU_DIM // HEAD_DIM):
                    xh = acc[:, hh * HEAD_DIM:(hh + 1) * HEAD_DIM]
                    ms = jnp.mean(xh * xh, axis=-1, keepdims=True)
                    lo = part * d_attn + off + hh * HEAD_DIM
                    qkv_ref[:, lo:lo + HEAD_DIM] = (
                        xh * lax.rsqrt(ms + EPS) * gain).astype(BF16)
        for off, acc in chunks(w_ref, 2 * d_attn, d_attn):
            lo = 2 * d_attn + off
            qkv_ref[:, lo:lo + MXU_DIM] = acc.astype(BF16)
        for off, acc in chunks(wu_ref, 0, wu_ref.shape[0]):
            u_ref[:, off:off + MXU_DIM] = acc
        h_scr[1 - cur] = _norm_mod(xn_ref[...], g_ref[...], shn_ref[0],
                                   scn_ref[0]).astype(BF16)


def _inproj_call(x, g, shift, scale, w_qkv, w_u, w_f, q_gain, k_gain, *, seq, d_attn,
                 tm=512):
    t, d = x.shape
    nt = t // tm
    d_pool = w_u.shape[0]
    assert w_qkv.shape[0] >= 3 * d_attn
    tiles_per_seq = seq // tm
    nxt = lambda s: jnp.minimum(s, nt - 1)
    tile = lambda s: jnp.maximum(s - 1, 0)
    mod_spec = pl.BlockSpec((1, 1, d), lambda s: (nxt(s) // tiles_per_seq, 0, 0))
    resident = lambda a: pl.BlockSpec(a.shape, lambda s: (0, 0),
                                      pipeline_mode=pl.Buffered(1))
    small = lambda a: pl.BlockSpec(a.shape, lambda s: (0, 0))
    return pl.pallas_call(
        functools.partial(_inproj_kernel, d_attn=d_attn),
        grid=(nt + 1,),
        in_specs=[pl.BlockSpec((tm, d), lambda s: (nxt(s), 0)),
                  small(g), mod_spec, mod_spec,
                  resident(w_qkv), resident(w_u), resident(w_f),
                  small(q_gain), small(k_gain)],
        out_specs=[pl.BlockSpec((tm, 3 * d_attn), lambda s: (tile(s), 0)),
                   pl.BlockSpec((tm, d_pool), lambda s: (tile(s), 0)),
                   pl.BlockSpec((tm, LANES), lambda s: (tile(s), 0))],
        out_shape=[jax.ShapeDtypeStruct((t, 3 * d_attn), BF16),
                   jax.ShapeDtypeStruct((t, d_pool), F32),
                   jax.ShapeDtypeStruct((t, LANES), F32)],
        scratch_shapes=[pltpu.VMEM((2, tm, d), BF16)],
        compiler_params=_cparams(("arbitrary",), 56),
        name="mixer_inproj",
    )(x, g, shift, scale, w_qkv, w_u, w_f, q_gain, k_gain)


N_SPLIT = 3


def _bias_selector(n_heads):
    rows = N_SPLIT * LANES
    hbits = n_heads.bit_length() - 1
    r = lax.broadcasted_iota(jnp.int32, (rows, 2 * LANES), 0)
    c = lax.broadcasted_iota(jnp.int32, (rows, 2 * LANES), 1)
    in_q = c < 2 * N_SPLIT * n_heads
    in_k = (c >= LANES) & (c < LANES + 2 * N_SPLIT * n_heads)
    ck = c - LANES
    slot_q = lax.shift_right_logical(c, hbits)
    slot_k = lax.shift_right_logical(jnp.maximum(ck, 0), hbits)
    head_q = c & (n_heads - 1)
    head_k = ck & (n_heads - 1)
    one_row = r == n_heads
    q_piece = in_q & (slot_q < N_SPLIT) & (r == slot_q * LANES + head_q)
    q_one = in_q & (slot_q >= N_SPLIT) & one_row
    k_one = in_k & (slot_k < N_SPLIT) & one_row
    k_piece = in_k & (slot_k >= N_SPLIT) & (r == (slot_k - N_SPLIT) * LANES + head_k)
    sel = jnp.where(q_piece | q_one | k_one, 1.0, 0.0) - jnp.where(k_piece, 1.0, 0.0)
    return sel.astype(BF16)


def _forget_kernel(f_ref, b_ref, qe_ref, ke_ref, pad_scr, *, seq, n_heads):
    z = f_ref[0] + b_ref[...]
    x = jnp.minimum(z, 0.0) - jnp.log1p(jnp.exp(-jnp.abs(z)))
    pad_scr[pl.ds(0, seq), :] = jnp.zeros((seq, LANES), F32)
    d = 1
    while d < seq:
        pad_scr[pl.ds(seq, seq), :] = x
        x = x + pad_scr[pl.ds(seq - d, seq), :]
        d *= 2
    rem = x * LOG2E
    lane = lax.broadcasted_iota(jnp.int32, (seq, LANES), 1)
    pieces = []
    for s in range(N_SPLIT):
        piece = rem.astype(BF16)
        rem = rem - piece.astype(F32)
        if s == 0:
            piece = jnp.where(lane == n_heads, jnp.ones_like(piece), piece)
        pieces.append(piece)
    e = jnp.dot(jnp.concatenate(pieces, axis=1), _bias_selector(n_heads),
                preferred_element_type=F32)
    qe_ref[0] = e[:, :LANES].astype(BF16)
    ke_ref[0] = e[:, LANES:].astype(BF16)


def _forget_call(f_logit, b_pad, n_heads):
    nb, seq, _ = f_logit.shape
    assert n_heads & (n_heads - 1) == 0 and 2 * N_SPLIT * n_heads <= LANES
    out = jax.ShapeDtypeStruct((nb, seq, LANES), BF16)
    spec = pl.BlockSpec((1, seq, LANES), lambda b: (b, 0, 0))
    return pl.pallas_call(
        functools.partial(_forget_kernel, seq=seq, n_heads=n_heads),
        grid=(nb,),
        in_specs=[spec, pl.BlockSpec((1, LANES), lambda b: (0, 0))],
        out_specs=[spec, spec],
        out_shape=[out, out],
        scratch_shapes=[pltpu.VMEM((2 * seq, LANES), F32)],
        compiler_params=_cparams(("parallel",), 40),
        name="forget_cumsum",
    )(f_logit, b_pad)


POOL_HALO = 16
POOL_ROWS = 512


def _pool_kernel(u_ref, w_ref, s_ref, o_ref, pad_scr, *, seq):
    g = pl.program_id(1)
    cg = u_ref.shape[2]
    pad_scr[pl.ds(0, POOL_HALO), :] = jnp.zeros((POOL_HALO, cg), F32)
    pad_scr[pl.ds(POOL_HALO, seq), :] = u_ref[0]
    w = w_ref[0].astype(BF16)
    scale = s_ref[...]
    for gi, win in enumerate(POOL_WINDOWS):
        @pl.when(g == gi)
        def _(win=win):
            for r0 in range(0, seq, POOL_ROWS):
                tok = pad_scr[pl.ds(POOL_HALO + r0, POOL_ROWS), :]
                tot = tok
                for dd in range(1, win):
                    tot = tot + pad_scr[pl.ds(POOL_HALO + r0 - dd, POOL_ROWS), :]
                pos = r0 + lax.broadcasted_iota(jnp.int32, (POOL_ROWS, cg), 0)
                cnt = jnp.minimum(pos + 1, win).astype(F32)
                p = tot / cnt - tok
                y = jnp.dot(p.astype(BF16), w, preferred_element_type=F32) * scale
                o_ref[0, pl.ds(r0, POOL_ROWS), :] = y.astype(BF16)


def _pool_call(u, pool_w, pool_scale):
    nb, seq, d_pool = u.shape
    ng, cg, _ = pool_w.shape
    return pl.pallas_call(
        functools.partial(_pool_kernel, seq=seq),
        grid=(nb, ng),
        in_specs=[pl.BlockSpec((1, seq, cg), lambda b, g: (b, 0, g)),
                  pl.BlockSpec((1, cg, cg), lambda b, g: (g, 0, 0)),
                  pl.BlockSpec((1, cg), lambda b, g: (0, g))],
        out_specs=pl.BlockSpec((1, seq, cg), lambda b, g: (b, 0, g)),
        out_shape=jax.ShapeDtypeStruct((nb, seq, d_pool), BF16),
        scratch_shapes=[pltpu.VMEM((POOL_HALO + seq, cg), F32)],
        compiler_params=_cparams(("parallel", "arbitrary"), 40),
        name="ms_pool",
    )(u, pool_w, pool_scale)


NEG_BIG = -1e30


def _lane_tile_reduce(x, op):
    out = x[:, :LANES]
    for t in range(1, x.shape[1] // LANES):
        out = op(out, x[:, t * LANES:(t + 1) * LANES])
    return out


def _attn_kernel(q_ref, qe_ref, k_ref, ke_ref, v_ref, o_ref, s_scr, p_scr, ke_scr,
                 *, tq, n_heads):
    seq = q_ref.shape[0]
    lane = lax.broadcasted_iota(jnp.int32, (seq, LANES), 1)
    mine = ((lane & (n_heads - 1)) == pl.program_id(1)) & (lane < 2 * N_SPLIT * n_heads)
    ke = ke_ref[0]
    ke_scr[...] = jnp.where(mine, ke, jnp.zeros_like(ke))
    for i in reversed(range(seq // tq)):
        slot = i % 2
        rows = pl.ds(i * tq, tq)
        qx = jnp.concatenate([q_ref[rows, :], qe_ref[0, rows, :]], axis=1)
        n_chunks = i + 1
        m_tile = None
        for c in range(n_chunks):
            cols = pl.ds(c * tq, tq)
            kx = jnp.concatenate([k_ref[cols, :], ke_scr[cols, :]], axis=1)
            s = lax.dot_general(qx, kx, (((1,), (1,)), ((), ())),
                                preferred_element_type=F32)
            if c == i:
                row = lax.broadcasted_iota(jnp.int32, (tq, tq), 0)
                col = lax.broadcasted_iota(jnp.int32, (tq, tq), 1)
                s = jnp.where(row >= col, s, NEG_BIG)
            s_scr[slot, :, cols] = s
            cm = _lane_tile_reduce(s, jnp.maximum)
            m_tile = cm if m_tile is None else jnp.maximum(m_tile, cm)
        m_row = jnp.max(m_tile, axis=-1, keepdims=True)
        l_tile = None
        for c in range(n_chunks):
            cols = pl.ds(c * tq, tq)
            p = jnp.exp2(s_scr[slot, :, cols] - m_row)
            p_scr[slot, :, cols] = p.astype(BF16)
            cl = _lane_tile_reduce(p, jnp.add)
            l_tile = cl if l_tile is None else l_tile + cl
        l_row = jnp.sum(l_tile, axis=-1, keepdims=True)
        kend = n_chunks * tq
        acc = jnp.dot(p_scr[slot, :, pl.ds(0, kend)], v_ref[pl.ds(0, kend), :],
                      preferred_element_type=F32)
        o_ref[rows, :] = (acc / l_row).astype(BF16)


def _attn_call(qkv, qe, ke, *, nb, seq, n_heads, tq=512):
    t = qkv.shape[0]
    d_attn = n_heads * HEAD_DIM
    head_spec = lambda off: pl.BlockSpec((seq, HEAD_DIM), lambda b, h: (b, off + h))
    bias_spec = pl.BlockSpec((1, seq, LANES), lambda b, h: (b, 0, 0))
    return pl.pallas_call(
        functools.partial(_attn_kernel, tq=tq, n_heads=n_heads),
        grid=(nb, n_heads),
        in_specs=[head_spec(0), bias_spec, head_spec(n_heads), bias_spec,
                  head_spec(2 * n_heads)],
        out_specs=head_spec(0),
        out_shape=jax.ShapeDtypeStruct((t, d_attn), BF16),
        scratch_shapes=[pltpu.VMEM((2, tq, seq), F32), pltpu.VMEM((2, tq, seq), BF16),
                        pltpu.VMEM((seq, LANES), BF16)],
        compiler_params=_cparams(("parallel", "parallel"), 56),
        name="fox_attention",
    )(qkv, qe, qkv, ke, qkv)


def _outproj_kernel(x_ref, attn_ref, pool_ref, wa_ref, wp_ref, gate_ref, o_ref, w_scr):
    da = attn_ref.shape[1]

    @pl.when(pl.program_id(0) == 0)
    def _():
        w_scr[:da, :] = wa_ref[...].astype(BF16)
        w_scr[da:, :] = wp_ref[...].astype(BF16)

    y = jnp.dot(attn_ref[...], w_scr[:da, :], preferred_element_type=F32)
    y = y + jnp.dot(pool_ref[...], w_scr[da:, :], preferred_element_type=F32)
    o_ref[...] = x_ref[...] + gate_ref[0] * y


def _outproj_call(x, attn, pool, w_out, gate, *, seq, tm=512):
    t, d = x.shape
    da = attn.shape[1]
    dp = pool.shape[1]
    assert da == dp and w_out.shape == (da + dp, d)
    tiles_per_seq = seq // tm
    w_spec = lambda half: pl.BlockSpec((da, d), lambda i: (half, 0),
                                       pipeline_mode=pl.Buffered(1))
    return pl.pallas_call(
        _outproj_kernel,
        grid=(t // tm,),
        in_specs=[pl.BlockSpec((tm, d), lambda i: (i, 0)),
                  pl.BlockSpec((tm, da), lambda i: (i, 0)),
                  pl.BlockSpec((tm, dp), lambda i: (i, 0)),
                  w_spec(0), w_spec(1),
                  pl.BlockSpec((1, 1, d), lambda i: (i // tiles_per_seq, 0, 0))],
        out_specs=pl.BlockSpec((tm, d), lambda i: (i, 0)),
        out_shape=jax.ShapeDtypeStruct((t, d), F32),
        scratch_shapes=[pltpu.VMEM((da + dp, d), BF16)],
        compiler_params=_cparams(("arbitrary",), 56),
        name="mixer_outproj",
    )(x, attn, pool, w_out, w_out, gate)


def kernel(x, c, w_ada, b_ada, ffn1_norm_g, ffn1_w_in, ffn1_w_out, mix_norm_g, w_in,
           b_forget, q_norm_g, k_norm_g, pool_w, pool_scale, w_out, ffn2_norm_g,
           ffn2_w_in, ffn2_w_out, final_norm_g):
    nb, seq, d = x.shape
    t = nb * seq
    n_heads = b_forget.shape[1]
    d_attn = n_heads * HEAD_DIM
    d_pool = pool_scale.shape[1]
    depth = w_ada.shape[0]
    xf = x.reshape(t, d)

    c_pad = jnp.pad(c, ((0, 8 - nb), (0, 0)))
    for l in range(depth):
        mod = _ada_call(c_pad, w_ada[l], b_ada[l].reshape(1, -1))[:nb]
        sh1, sc1, g1, sh2, sc2, g2, sh3, sc3, g3 = [
            mod[:, i * d:(i + 1) * d].reshape(nb, 1, d) for i in range(9)]

        xf = _ffn_resident(xf, ffn1_norm_g[l].reshape(1, d), sh1, sc1, g1, ffn1_w_in[l],
                           ffn1_w_out[l], None, seq=seq)

        wl = w_in[l]
        f0 = 3 * d_attn
        w_qkv = wl.T.astype(BF16)
        w_u = w_qkv[f0 + n_heads:]
        w_f = jnp.pad(w_qkv[f0:f0 + n_heads], ((0, LANES - n_heads), (0, 0)))
        q_gain = (q_norm_g[l] * (HEAD_DIM ** -0.5 * LOG2E)).reshape(1, HEAD_DIM)
        k_gain = k_norm_g[l].reshape(1, HEAD_DIM)
        qkv, u, f_logit = _inproj_call(
            xf, mix_norm_g[l].reshape(1, d), sh2, sc2, w_qkv, w_u, w_f, q_gain, k_gain,
            seq=seq, d_attn=d_attn)
        b_pad = jnp.pad(b_forget[l], (0, LANES - n_heads)).reshape(1, LANES)
        qe, ke = _forget_call(f_logit.reshape(nb, seq, LANES), b_pad, n_heads)
        attn = _attn_call(qkv, qe, ke, nb=nb, seq=seq, n_heads=n_heads)
        pool = _pool_call(u.reshape(nb, seq, d_pool), pool_w[l], pool_scale[l].reshape(1, -1))
        xf = _outproj_call(xf, attn, pool.reshape(t, d_pool), w_out[l], g2, seq=seq)

        last = l == depth - 1
        xf = _ffn_resident(xf, ffn2_norm_g[l].reshape(1, d), sh3, sc3, g3, ffn2_w_in[l],
                           ffn2_w_out[l], final_norm_g.reshape(1, d) if last else None,
                           seq=seq)
    return xf.reshape(nb, seq, d)
```

```python
import functools

import jax
import jax.numpy as jnp
from jax import lax
from jax.experimental import pallas as pl
from jax.experimental.pallas import tpu as pltpu

F32 = jnp.float32
BF16 = jnp.bfloat16

EPS = 1e-6
HEAD_DIM = 128
POOL_WINDOWS = (2, 4, 8, 16)
LANES = 128
MXU_DIM = 256
MIB = 1024 * 1024
LOG2E = 1.4426950408889634


def _cparams(dims, vmem_mib):
    return pltpu.CompilerParams(dimension_semantics=dims,
                                vmem_limit_bytes=vmem_mib * MIB)


def _norm_mod(x, g, shift, scale):
    ms = jnp.mean(x * x, axis=-1, keepdims=True)
    return x * lax.rsqrt(ms + EPS) * (g * (1.0 + scale)) + shift


NORM_PIECES = 8
SUBLANES = 8


def _norm_pieces(x_ref, g_ref, shift_ref, scale_ref, h_ref):
    rows = x_ref.shape[0] // NORM_PIECES
    zeros = []
    for k in range(NORM_PIECES):
        sl = pl.ds(k * rows, rows)
        h = _norm_mod(x_ref[sl, :], g_ref[...], shift_ref[0], scale_ref[0])
        h_ref[sl, :] = h.astype(BF16)
        acc = None
        for r in range(0, rows, SUBLANES):
            for c in range(0, h.shape[1], LANES):
                tile = h[r:r + SUBLANES, c:c + LANES]
                acc = tile if acc is None else acc + tile
        zeros.append(acc * 0.0)
    return zeros


def _anchor(a, zero):
    top = 2 * SUBLANES
    z = jnp.concatenate([zero, zero], axis=0)
    a_top = jnp.concatenate([a[:top, :LANES] + z, a[:top, LANES:]], axis=1)
    return jnp.concatenate([a_top, a[top:]], axis=0)


def _ada_kernel(c_ref, w_ref, b_ref, o_ref):
    c = c_ref[...]
    ca = c * (1.0 / (1.0 + jnp.exp(-c)))
    o_ref[...] = jnp.dot(ca.astype(BF16), w_ref[...].astype(BF16),
                         preferred_element_type=F32) + b_ref[...]


def _ada_call(c_pad, w, b, tn=2048):
    m, d = c_pad.shape
    n = w.shape[1]
    return pl.pallas_call(
        _ada_kernel,
        grid=(n // tn,),
        in_specs=[pl.BlockSpec((m, d), lambda j: (0, 0)),
                  pl.BlockSpec((d, tn), lambda j: (0, j)),
                  pl.BlockSpec((1, tn), lambda j: (0, j))],
        out_specs=pl.BlockSpec((m, tn), lambda j: (0, j)),
        out_shape=jax.ShapeDtypeStruct((m, n), F32),
        compiler_params=_cparams(("arbitrary",), 40),
        name="ada_mod",
    )(c_pad, w, b)


def _ffn_up_kernel(xn_ref, g_ref, shn_ref, scn_ref, a0_ref, a1_ref, b0_ref, b1_ref,
                   hid_ref, w_scr, h0_scr, h1_scr, *, n_load, nblk, half_blocks):
    half = pl.program_id(0)
    s = pl.program_id(1)

    @pl.when(s < n_load)
    def _():
        k0 = half * half_blocks + 2 * s
        for q, ref in enumerate((a0_ref, a1_ref, b0_ref, b1_ref)):
            keep = k0 + (q % 2) < nblk
            w_scr[s, :, q * LANES:(q + 1) * LANES] = jnp.where(
                keep, ref[...], 0.0).astype(BF16)

    @pl.when(s == n_load - 1)
    def _():
        h0_scr[...] = _norm_mod(xn_ref[...], g_ref[...], shn_ref[0],
                                scn_ref[0]).astype(BF16)

    def project(h_ref, hn_ref):
        zeros = _norm_pieces(xn_ref, g_ref, shn_ref, scn_ref, hn_ref)
        first = n_load - len(zeros)
        for c in range(n_load):
            hw = jnp.dot(h_ref[...], w_scr[c], preferred_element_type=F32)
            a = hw[:, :2 * LANES]
            b = hw[:, 2 * LANES:]
            if c >= first:
                a = _anchor(a, zeros[c - first])
            hid_ref[:, c * 2 * LANES:(c + 1) * 2 * LANES] = (
                a * (1.0 / (1.0 + jnp.exp(-a))) * b).astype(BF16)

    odd = (s - n_load) % 2

    @pl.when(jnp.logical_and(s >= n_load, odd == 0))
    def _():
        project(h0_scr, h1_scr)

    @pl.when(jnp.logical_and(s >= n_load, odd == 1))
    def _():
        project(h1_scr, h0_scr)


def _ffn_up_call(x, g, shift, scale, w_in, *, seq, nblk, nblk_pad, tm=512):
    t, d = x.shape
    nt = t // tm
    half_blocks = nblk_pad // 2
    n_load = half_blocks // 2
    tiles_per_seq = seq // tm
    nxt = lambda s: jnp.clip(s - n_load + 1, 0, nt - 1)
    blk = lambda c, s, q: jnp.minimum(c * half_blocks + 2 * jnp.minimum(s, n_load - 1) + q,
                                      nblk - 1)
    col = lambda off, q: pl.BlockSpec((d, LANES), lambda c, s: (0, off + blk(c, s, q)))
    mod_spec = pl.BlockSpec((1, 1, d), lambda c, s: (nxt(s) // tiles_per_seq, 0, 0))
    return pl.pallas_call(
        functools.partial(_ffn_up_kernel, n_load=n_load, nblk=nblk,
                          half_blocks=half_blocks),
        grid=(2, n_load + nt),
        in_specs=[pl.BlockSpec((tm, d), lambda c, s: (nxt(s), 0)),
                  pl.BlockSpec((1, d), lambda c, s: (0, 0)),
                  mod_spec, mod_spec,
                  col(0, 0), col(0, 1), col(nblk, 0), col(nblk, 1)],
        out_specs=pl.BlockSpec((tm, half_blocks * LANES),
                               lambda c, s: (jnp.maximum(s - n_load, 0), c)),
        out_shape=jax.ShapeDtypeStruct((t, nblk_pad * LANES), BF16),
        scratch_shapes=[pltpu.VMEM((n_load, d, 4 * LANES), BF16),
                        pltpu.VMEM((tm, d), BF16), pltpu.VMEM((tm, d), BF16)],
        compiler_params=_cparams(("arbitrary", "arbitrary"), 58),
        name="ffn_up",
    )(x, g, shift, scale, w_in, w_in, w_in, w_in)


FFN_DOWN_LOAD_BLOCKS = 2
FFN_DOWN_COLS = 512


def _ffn_down_kernel(hid_ref, x_ref, gate_ref, *rest, n_load, nblk, final_norm):
    rest = list(rest)
    r_refs = [rest.pop(0) for _ in range(FFN_DOWN_LOAD_BLOCKS)]
    fg_ref = rest.pop(0) if final_norm else None
    o_ref, w_scr = rest
    s = pl.program_id(0)
    d = o_ref.shape[1]

    @pl.when(s < n_load)
    def _():
        for q, ref in enumerate(r_refs):
            k = FFN_DOWN_LOAD_BLOCKS * s + q
            rows = pl.ds(pl.multiple_of(k * LANES, LANES), LANES)
            w_scr[rows, :] = jnp.where(k < nblk, ref[...], 0.0).astype(BF16)

    @pl.when(s >= n_load)
    def _():
        ssq = None
        for n in range(d // FFN_DOWN_COLS):
            cols = slice(n * FFN_DOWN_COLS, (n + 1) * FFN_DOWN_COLS)
            acc = jnp.dot(hid_ref[...], w_scr[:, cols], preferred_element_type=F32)
            y = x_ref[:, cols] + 0.5 * gate_ref[0, :, cols] * acc
            if final_norm:
                part = jnp.sum(y * y, axis=-1, keepdims=True)
                ssq = part if ssq is None else ssq + part
            o_ref[:, cols] = y
        if final_norm:
            o_ref[...] = o_ref[...] * lax.rsqrt(ssq * (1.0 / d) + EPS) * fg_ref[...]


def _ffn_down_call(hidden, x, gate, w_out, final_g, *, seq, nblk, nblk_pad, tm=512):
    t, d = x.shape
    nt = t // tm
    lb = FFN_DOWN_LOAD_BLOCKS
    n_load = nblk_pad // lb
    tiles_per_seq = seq // tm
    tile = lambda s: jnp.maximum(s - n_load, 0)
    row = lambda q: pl.BlockSpec(
        (LANES, d), lambda s: (jnp.minimum(lb * jnp.minimum(s, n_load - 1) + q, nblk - 1), 0))
    in_specs = [pl.BlockSpec((tm, nblk_pad * LANES), lambda s: (tile(s), 0)),
                pl.BlockSpec((tm, d), lambda s: (tile(s), 0)),
                pl.BlockSpec((1, 1, d), lambda s: (tile(s) // tiles_per_seq, 0, 0))]
    in_specs += [row(q) for q in range(lb)]
    args = [hidden, x, gate] + [w_out] * lb
    if final_g is not None:
        in_specs.append(pl.BlockSpec((1, d), lambda s: (0, 0)))
        args.append(final_g)
    return pl.pallas_call(
        functools.partial(_ffn_down_kernel, n_load=n_load, nblk=nblk,
                          final_norm=final_g is not None),
        grid=(n_load + nt,),
        in_specs=in_specs,
        out_specs=pl.BlockSpec((tm, d), lambda s: (tile(s), 0)),
        out_shape=jax.ShapeDtypeStruct((t, d), F32),
        scratch_shapes=[pltpu.VMEM((nblk_pad * LANES, d), BF16)],
        compiler_params=_cparams(("arbitrary",), 62),
        name="ffn_down_final" if final_g is not None else "ffn_down",
    )(*args)


FFN_BLOCK_ALIGN = 4


def _ffn_resident(x, g, shift, scale, gate, w_in, w_out, final_g, *, seq):
    nblk = w_out.shape[0] // LANES
    nblk_pad = -(-nblk // FFN_BLOCK_ALIGN) * FFN_BLOCK_ALIGN
    hidden = _ffn_up_call(x, g, shift, scale, w_in, seq=seq, nblk=nblk, nblk_pad=nblk_pad)
    return _ffn_down_call(hidden, x, gate, w_out, final_g, seq=seq, nblk=nblk,
                          nblk_pad=nblk_pad)


def _inproj_kernel(xn_ref, g_ref, shn_ref, scn_ref, w_ref, wu_ref, wf_ref, qg_ref, kg_ref,
                   qkv_ref, u_ref, f_ref, h0_scr, h1_scr, *, d_attn):
    s = pl.program_id(0)

    @pl.when(s == 0)
    def _():
        h0_scr[...] = _norm_mod(xn_ref[...], g_ref[...], shn_ref[0],
                                scn_ref[0]).astype(BF16)

    def tile_step(h_ref, hn_ref):
        def project(w_rows):
            return lax.dot_general(h_ref[...], w_rows, (((1,), (1,)), ((), ())),
                                   preferred_element_type=F32)

        def chunks(src_ref, start, width):
            for c0 in range(start, start + width, MXU_DIM):
                yield c0 - start, project(src_ref[c0:c0 + MXU_DIM, :])

        f_ref[...] = project(wf_ref[...])
        for part, gain_ref in enumerate((qg_ref, kg_ref)):
            gain = gain_ref[...]
            for off, acc in chunks(w_ref, part * d_attn, d_attn):
                for hh in range(MXU_DIM // HEAD_DIM):
                    xh = acc[:, hh * HEAD_DIM:(hh + 1) * HEAD_DIM]
                    ms = jnp.mean(xh * xh, axis=-1, keepdims=True)
                    lo = part * d_attn + off + hh * HEAD_DIM
                    qkv_ref[:, lo:lo + HEAD_DIM] = (
                        xh * lax.rsqrt(ms + EPS) * gain).astype(BF16)
        zeros = iter(_norm_pieces(xn_ref, g_ref, shn_ref, scn_ref, hn_ref))
        for off, acc in chunks(w_ref, 2 * d_attn, d_attn):
            lo = 2 * d_attn + off
            qkv_ref[:, lo:lo + MXU_DIM] = _anchor(acc, next(zeros)).astype(BF16)
        for off, acc in chunks(wu_ref, 0, wu_ref.shape[0]):
            u_ref[:, off:off + MXU_DIM] = _anchor(acc, next(zeros))

    odd = (s - 1) % 2

    @pl.when(jnp.logical_and(s >= 1, odd == 0))
    def _():
        tile_step(h0_scr, h1_scr)

    @pl.when(jnp.logical_and(s >= 1, odd == 1))
    def _():
        tile_step(h1_scr, h0_scr)


def _inproj_call(x, g, shift, scale, w_qkv, w_u, w_f, q_gain, k_gain, *, seq, d_attn,
                 tm=512):
    t, d = x.shape
    nt = t // tm
    d_pool = w_u.shape[0]
    assert w_qkv.shape[0] >= 3 * d_attn
    tiles_per_seq = seq // tm
    nxt = lambda s: jnp.minimum(s, nt - 1)
    tile = lambda s: jnp.maximum(s - 1, 0)
    mod_spec = pl.BlockSpec((1, 1, d), lambda s: (nxt(s) // tiles_per_seq, 0, 0))
    resident = lambda a: pl.BlockSpec(a.shape, lambda s: (0, 0),
                                      pipeline_mode=pl.Buffered(1))
    small = lambda a: pl.BlockSpec(a.shape, lambda s: (0, 0))
    return pl.pallas_call(
        functools.partial(_inproj_kernel, d_attn=d_attn),
        grid=(nt + 1,),
        in_specs=[pl.BlockSpec((tm, d), lambda s: (nxt(s), 0)),
                  small(g), mod_spec, mod_spec,
                  resident(w_qkv), resident(w_u), resident(w_f),
                  small(q_gain), small(k_gain)],
        out_specs=[pl.BlockSpec((tm, 3 * d_attn), lambda s: (tile(s), 0)),
                   pl.BlockSpec((tm, d_pool), lambda s: (tile(s), 0)),
                   pl.BlockSpec((tm, LANES), lambda s: (tile(s), 0))],
        out_shape=[jax.ShapeDtypeStruct((t, 3 * d_attn), BF16),
                   jax.ShapeDtypeStruct((t, d_pool), F32),
                   jax.ShapeDtypeStruct((t, LANES), F32)],
        scratch_shapes=[pltpu.VMEM((tm, d), BF16), pltpu.VMEM((tm, d), BF16)],
        compiler_params=_cparams(("arbitrary",), 56),
        name="mixer_inproj",
    )(x, g, shift, scale, w_qkv, w_u, w_f, q_gain, k_gain)


N_SPLIT = 3


def _bias_selector(n_heads):
    rows = N_SPLIT * LANES
    hbits = n_heads.bit_length() - 1
    r = lax.broadcasted_iota(jnp.int32, (rows, 2 * LANES), 0)
    c = lax.broadcasted_iota(jnp.int32, (rows, 2 * LANES), 1)
    in_q = c < 2 * N_SPLIT * n_heads
    in_k = (c >= LANES) & (c < LANES + 2 * N_SPLIT * n_heads)
    ck = c - LANES
    slot_q = lax.shift_right_logical(c, hbits)
    slot_k = lax.shift_right_logical(jnp.maximum(ck, 0), hbits)
    head_q = c & (n_heads - 1)
    head_k = ck & (n_heads - 1)
    one_row = r == n_heads
    q_piece = in_q & (slot_q < N_SPLIT) & (r == slot_q * LANES + head_q)
    q_one = in_q & (slot_q >= N_SPLIT) & one_row
    k_one = in_k & (slot_k < N_SPLIT) & one_row
    k_piece = in_k & (slot_k >= N_SPLIT) & (r == (slot_k - N_SPLIT) * LANES + head_k)
    sel = jnp.where(q_piece | q_one | k_one, 1.0, 0.0) - jnp.where(k_piece, 1.0, 0.0)
    return sel.astype(BF16)


def _forget_kernel(f_ref, b_ref, qe_ref, ke_ref, pad_scr, *, seq, n_heads):
    z = f_ref[0] + b_ref[...]
    x = jnp.minimum(z, 0.0) - jnp.log1p(jnp.exp(-jnp.abs(z)))
    pad_scr[pl.ds(0, seq), :] = jnp.zeros((seq, LANES), F32)
    d = 1
    while d < seq:
        pad_scr[pl.ds(seq, seq), :] = x
        x = x + pad_scr[pl.ds(seq - d, seq), :]
        d *= 2
    rem = x * LOG2E
    lane = lax.broadcasted_iota(jnp.int32, (seq, LANES), 1)
    pieces = []
    for s in range(N_SPLIT):
        piece = rem.astype(BF16)
        rem = rem - piece.astype(F32)
        if s == 0:
            piece = jnp.where(lane == n_heads, jnp.ones_like(piece), piece)
        pieces.append(piece)
    e = jnp.dot(jnp.concatenate(pieces, axis=1), _bias_selector(n_heads),
                preferred_element_type=F32)
    qe_ref[0] = e[:, :LANES].astype(BF16)
    ke_ref[0] = e[:, LANES:].astype(BF16)


def _forget_call(f_logit, b_pad, n_heads):
    nb, seq, _ = f_logit.shape
    assert n_heads & (n_heads - 1) == 0 and 2 * N_SPLIT * n_heads <= LANES
    out = jax.ShapeDtypeStruct((nb, seq, LANES), BF16)
    spec = pl.BlockSpec((1, seq, LANES), lambda b: (b, 0, 0))
    return pl.pallas_call(
        functools.partial(_forget_kernel, seq=seq, n_heads=n_heads),
        grid=(nb,),
        in_specs=[spec, pl.BlockSpec((1, LANES), lambda b: (0, 0))],
        out_specs=[spec, spec],
        out_shape=[out, out],
        scratch_shapes=[pltpu.VMEM((2 * seq, LANES), F32)],
        compiler_params=_cparams(("parallel",), 40),
        name="forget_cumsum",
    )(f_logit, b_pad)


POOL_HALO = 16
POOL_ROWS = 512


def _pool_kernel(u_ref, w_ref, s_ref, o_ref, pad_scr, *, seq):
    g = pl.program_id(1)
    cg = u_ref.shape[2]
    pad_scr[pl.ds(0, POOL_HALO), :] = jnp.zeros((POOL_HALO, cg), F32)
    pad_scr[pl.ds(POOL_HALO, seq), :] = u_ref[0]
    w = w_ref[0].astype(BF16)
    scale = s_ref[...]
    for gi, win in enumerate(POOL_WINDOWS):
        @pl.when(g == gi)
        def _(win=win):
            for r0 in range(0, seq, POOL_ROWS):
                tok = pad_scr[pl.ds(POOL_HALO + r0, POOL_ROWS), :]
                tot = tok
                for dd in range(1, win):
                    tot = tot + pad_scr[pl.ds(POOL_HALO + r0 - dd, POOL_ROWS), :]
                pos = r0 + lax.broadcasted_iota(jnp.int32, (POOL_ROWS, cg), 0)
                cnt = jnp.minimum(pos + 1, win).astype(F32)
                p = tot / cnt - tok
                y = jnp.dot(p.astype(BF16), w, preferred_element_type=F32) * scale
                o_ref[0, pl.ds(r0, POOL_ROWS), :] = y.astype(BF16)


def _pool_call(u, pool_w, pool_scale):
    nb, seq, d_pool = u.shape
    ng, cg, _ = pool_w.shape
    return pl.pallas_call(
        functools.partial(_pool_kernel, seq=seq),
        grid=(nb, ng),
        in_specs=[pl.BlockSpec((1, seq, cg), lambda b, g: (b, 0, g)),
                  pl.BlockSpec((1, cg, cg), lambda b, g: (g, 0, 0)),
                  pl.BlockSpec((1, cg), lambda b, g: (0, g))],
        out_specs=pl.BlockSpec((1, seq, cg), lambda b, g: (b, 0, g)),
        out_shape=jax.ShapeDtypeStruct((nb, seq, d_pool), BF16),
        scratch_shapes=[pltpu.VMEM((POOL_HALO + seq, cg), F32)],
        compiler_params=_cparams(("parallel", "arbitrary"), 40),
        name="ms_pool",
    )(u, pool_w, pool_scale)


NEG_BIG = -1e30


def _lane_tile_reduce(x, op):
    out = x[:, :LANES]
    for t in range(1, x.shape[1] // LANES):
        out = op(out, x[:, t * LANES:(t + 1) * LANES])
    return out


def _attn_kernel(q_ref, qe_ref, k_ref, ke_ref, v_ref, o_ref, s_scr, p_scr, ke_scr,
                 *, tq, n_heads):
    seq = q_ref.shape[0]
    lane = lax.broadcasted_iota(jnp.int32, (seq, LANES), 1)
    mine = ((lane & (n_heads - 1)) == pl.program_id(1)) & (lane < 2 * N_SPLIT * n_heads)
    ke = ke_ref[0]
    ke_scr[...] = jnp.where(mine, ke, jnp.zeros_like(ke))
    for i in reversed(range(seq // tq)):
        slot = i % 2
        rows = pl.ds(i * tq, tq)
        qx = jnp.concatenate([q_ref[rows, :], qe_ref[0, rows, :]], axis=1)
        n_chunks = i + 1
        m_tile = None
        for c in range(n_chunks):
            cols = pl.ds(c * tq, tq)
            kx = jnp.concatenate([k_ref[cols, :], ke_scr[cols, :]], axis=1)
            s = lax.dot_general(qx, kx, (((1,), (1,)), ((), ())),
                                preferred_element_type=F32)
            if c == i:
                row = lax.broadcasted_iota(jnp.int32, (tq, tq), 0)
                col = lax.broadcasted_iota(jnp.int32, (tq, tq), 1)
                s = jnp.where(row >= col, s, NEG_BIG)
            s_scr[slot, :, cols] = s
            cm = _lane_tile_reduce(s, jnp.maximum)
            m_tile = cm if m_tile is None else jnp.maximum(m_tile, cm)
        m_row = jnp.max(m_tile, axis=-1, keepdims=True)
        l_tile = None
        for c in range(n_chunks):
            cols = pl.ds(c * tq, tq)
            p = jnp.exp2(s_scr[slot, :, cols] - m_row)
            p_scr[slot, :, cols] = p.astype(BF16)
            cl = _lane_tile_reduce(p, jnp.add)
            l_tile = cl if l_tile is None else l_tile + cl
        l_row = jnp.sum(l_tile, axis=-1, keepdims=True)
        kend = n_chunks * tq
        acc = jnp.dot(p_scr[slot, :, pl.ds(0, kend)], v_ref[pl.ds(0, kend), :],
                      preferred_element_type=F32)
        o_ref[rows, :] = (acc / l_row).astype(BF16)


def _attn_call(qkv, qe, ke, *, nb, seq, n_heads, tq=512):
    t = qkv.shape[0]
    d_attn = n_heads * HEAD_DIM
    head_spec = lambda off: pl.BlockSpec((seq, HEAD_DIM), lambda b, h: (b, off + h))
    bias_spec = pl.BlockSpec((1, seq, LANES), lambda b, h: (b, 0, 0))
    return pl.pallas_call(
        functools.partial(_attn_kernel, tq=tq, n_heads=n_heads),
        grid=(nb, n_heads),
        in_specs=[head_spec(0), bias_spec, head_spec(n_heads), bias_spec,
                  head_spec(2 * n_heads)],
        out_specs=head_spec(0),
        out_shape=jax.ShapeDtypeStruct((t, d_attn), BF16),
        scratch_shapes=[pltpu.VMEM((2, tq, seq), F32), pltpu.VMEM((2, tq, seq), BF16),
                        pltpu.VMEM((seq, LANES), BF16)],
        compiler_params=_cparams(("parallel", "parallel"), 56),
        name="fox_attention",
    )(qkv, qe, qkv, ke, qkv)


def _outproj_kernel(x_ref, attn_ref, pool_ref, wa_ref, wp_ref, gate_ref, o_ref, w_scr):
    da = attn_ref.shape[1]

    @pl.when(pl.program_id(0) == 0)
    def _():
        w_scr[:da, :] = wa_ref[...].astype(BF16)
        w_scr[da:, :] = wp_ref[...].astype(BF16)

    y = jnp.dot(attn_ref[...], w_scr[:da, :], preferred_element_type=F32)
    y = y + jnp.dot(pool_ref[...], w_scr[da:, :], preferred_element_type=F32)
    o_ref[...] = x_ref[...] + gate_ref[0] * y


def _outproj_call(x, attn, pool, w_out, gate, *, seq, tm=512):
    t, d = x.shape
    da = attn.shape[1]
    dp = pool.shape[1]
    assert da == dp and w_out.shape == (da + dp, d)
    tiles_per_seq = seq // tm
    w_spec = lambda half: pl.BlockSpec((da, d), lambda i: (half, 0),
                                       pipeline_mode=pl.Buffered(1))
    return pl.pallas_call(
        _outproj_kernel,
        grid=(t // tm,),
        in_specs=[pl.BlockSpec((tm, d), lambda i: (i, 0)),
                  pl.BlockSpec((tm, da), lambda i: (i, 0)),
                  pl.BlockSpec((tm, dp), lambda i: (i, 0)),
                  w_spec(0), w_spec(1),
                  pl.BlockSpec((1, 1, d), lambda i: (i // tiles_per_seq, 0, 0))],
        out_specs=pl.BlockSpec((tm, d), lambda i: (i, 0)),
        out_shape=jax.ShapeDtypeStruct((t, d), F32),
        scratch_shapes=[pltpu.VMEM((da + dp, d), BF16)],
        compiler_params=_cparams(("arbitrary",), 56),
        name="mixer_outproj",
    )(x, attn, pool, w_out, w_out, gate)


def kernel(x, c, w_ada, b_ada, ffn1_norm_g, ffn1_w_in, ffn1_w_out, mix_norm_g, w_in,
           b_forget, q_norm_g, k_norm_g, pool_w, pool_scale, w_out, ffn2_norm_g,
           ffn2_w_in, ffn2_w_out, final_norm_g):
    nb, seq, d = x.shape
    t = nb * seq
    n_heads = b_forget.shape[1]
    d_attn = n_heads * HEAD_DIM
    d_pool = pool_scale.shape[1]
    depth = w_ada.shape[0]
    xf = x.reshape(t, d)

    c_pad = jnp.pad(c, ((0, 8 - nb), (0, 0)))
    for l in range(depth):
        mod = _ada_call(c_pad, w_ada[l], b_ada[l].reshape(1, -1))[:nb]
        sh1, sc1, g1, sh2, sc2, g2, sh3, sc3, g3 = [
            mod[:, i * d:(i + 1) * d].reshape(nb, 1, d) for i in range(9)]

        xf = _ffn_resident(xf, ffn1_norm_g[l].reshape(1, d), sh1, sc1, g1, ffn1_w_in[l],
                           ffn1_w_out[l], None, seq=seq)

        wl = w_in[l]
        f0 = 3 * d_attn
        w_qkv = wl.T.astype(BF16)
        w_u = w_qkv[f0 + n_heads:]
        w_f = jnp.pad(w_qkv[f0:f0 + n_heads], ((0, LANES - n_heads), (0, 0)))
        q_gain = (q_norm_g[l] * (HEAD_DIM ** -0.5 * LOG2E)).reshape(1, HEAD_DIM)
        k_gain = k_norm_g[l].reshape(1, HEAD_DIM)
        qkv, u, f_logit = _inproj_call(
            xf, mix_norm_g[l].reshape(1, d), sh2, sc2, w_qkv, w_u, w_f, q_gain, k_gain,
            seq=seq, d_attn=d_attn)
        b_pad = jnp.pad(b_forget[l], (0, LANES - n_heads)).reshape(1, LANES)
        qe, ke = _forget_call(f_logit.reshape(nb, seq, LANES), b_pad, n_heads)
        attn = _attn_call(qkv, qe, ke, nb=nb, seq=seq, n_heads=n_heads)
        pool = _pool_call(u.reshape(nb, seq, d_pool), pool_w[l], pool_scale[l].reshape(1, -1))
        xf = _outproj_call(xf, attn, pool.reshape(t, d_pool), w_out[l], g2, seq=seq)

        last = l == depth - 1
        xf = _ffn_resident(xf, ffn2_norm_g[l].reshape(1, d), sh3, sc3, g3, ffn2_w_in[l],
                           ffn2_w_out[l], final_norm_g.reshape(1, d) if last else None,
                           seq=seq)
    return xf.reshape(nb, seq, d)
```

```python
import functools

import jax
import jax.numpy as jnp
from jax import lax
from jax.experimental import pallas as pl
from jax.experimental.pallas import tpu as pltpu

F32 = jnp.float32
BF16 = jnp.bfloat16

EPS = 1e-6
HEAD_DIM = 128
POOL_WINDOWS = (2, 4, 8, 16)
LANES = 128
MXU_DIM = 256
MIB = 1024 * 1024
LOG2E = 1.4426950408889634


def _cparams(dims, vmem_mib):
    return pltpu.CompilerParams(dimension_semantics=dims,
                                vmem_limit_bytes=vmem_mib * MIB)


def _norm_mod(x, g, shift, scale):
    ms = jnp.mean(x * x, axis=-1, keepdims=True)
    return x * lax.rsqrt(ms + EPS) * (g * (1.0 + scale)) + shift


NORM_PIECES = 8
SUBLANES = 8


def _norm_pieces(x_ref, g_ref, shift_ref, scale_ref, h_ref):
    rows = x_ref.shape[0] // NORM_PIECES
    zeros = []
    for k in range(NORM_PIECES):
        sl = pl.ds(k * rows, rows)
        h = _norm_mod(x_ref[sl, :], g_ref[...], shift_ref[0], scale_ref[0])
        h_ref[sl, :] = h.astype(BF16)
        acc = None
        for r in range(0, rows, SUBLANES):
            for c in range(0, h.shape[1], LANES):
                tile = h[r:r + SUBLANES, c:c + LANES]
                acc = tile if acc is None else acc + tile
        zeros.append(acc * 0.0)
    return zeros


def _anchor(a, zero):
    top = 2 * SUBLANES
    z = jnp.concatenate([zero, zero], axis=0)
    a_top = jnp.concatenate([a[:top, :LANES] + z, a[:top, LANES:]], axis=1)
    return jnp.concatenate([a_top, a[top:]], axis=0)


def _ada_kernel(c_ref, w_ref, b_ref, o_ref):
    c = c_ref[...]
    ca = c * (1.0 / (1.0 + jnp.exp(-c)))
    o_ref[...] = jnp.dot(ca.astype(BF16), w_ref[...].astype(BF16),
                         preferred_element_type=F32) + b_ref[...]


def _ada_call(c_pad, w, b, tn=1024):
    m, d = c_pad.shape
    n = w.shape[1]
    return pl.pallas_call(
        _ada_kernel,
        grid=(n // tn,),
        in_specs=[pl.BlockSpec((m, d), lambda j: (0, 0)),
                  pl.BlockSpec((d, tn), lambda j: (0, j)),
                  pl.BlockSpec((1, tn), lambda j: (0, j))],
        out_specs=pl.BlockSpec((m, tn), lambda j: (0, j)),
        out_shape=jax.ShapeDtypeStruct((m, n), F32),
        compiler_params=_cparams(("arbitrary",), 40),
        name="ada_mod",
    )(c_pad, w, b)


def _ffn_up_kernel(xn_ref, g_ref, shn_ref, scn_ref, a0_ref, a1_ref, b0_ref, b1_ref,
                   hid_ref, w_scr, h0_scr, h1_scr, *, n_load, nblk, half_blocks):
    half = pl.program_id(0)
    s = pl.program_id(1)

    @pl.when(s < n_load)
    def _():
        k0 = half * half_blocks + 2 * s
        for q, ref in enumerate((a0_ref, a1_ref, b0_ref, b1_ref)):
            keep = k0 + (q % 2) < nblk
            w_scr[s, :, q * LANES:(q + 1) * LANES] = jnp.where(
                keep, ref[...], 0.0).astype(BF16)

    @pl.when(s == n_load - 1)
    def _():
        h0_scr[...] = _norm_mod(xn_ref[...], g_ref[...], shn_ref[0],
                                scn_ref[0]).astype(BF16)

    def project(h_ref, hn_ref):
        zeros = _norm_pieces(xn_ref, g_ref, shn_ref, scn_ref, hn_ref)
        first = n_load - len(zeros)
        for c in range(n_load):
            hw = jnp.dot(h_ref[...], w_scr[c], preferred_element_type=F32)
            a = hw[:, :2 * LANES]
            b = hw[:, 2 * LANES:]
            if c >= first:
                a = _anchor(a, zeros[c - first])
            hid_ref[:, c * 2 * LANES:(c + 1) * 2 * LANES] = (
                a * (1.0 / (1.0 + jnp.exp(-a))) * b).astype(BF16)

    odd = (s - n_load) % 2

    @pl.when(jnp.logical_and(s >= n_load, odd == 0))
    def _():
        project(h0_scr, h1_scr)

    @pl.when(jnp.logical_and(s >= n_load, odd == 1))
    def _():
        project(h1_scr, h0_scr)


def _ffn_up_call(x, g, shift, scale, w_in, *, seq, nblk, nblk_pad, tm=512):
    t, d = x.shape
    nt = t // tm
    half_blocks = nblk_pad // 2
    n_load = half_blocks // 2
    tiles_per_seq = seq // tm
    nxt = lambda s: jnp.clip(s - n_load + 1, 0, nt - 1)
    blk = lambda c, s, q: jnp.minimum(c * half_blocks + 2 * jnp.minimum(s, n_load - 1) + q,
                                      nblk - 1)
    col = lambda off, q: pl.BlockSpec((d, LANES), lambda c, s: (0, off + blk(c, s, q)))
    mod_spec = pl.BlockSpec((1, 1, d), lambda c, s: (nxt(s) // tiles_per_seq, 0, 0))
    return pl.pallas_call(
        functools.partial(_ffn_up_kernel, n_load=n_load, nblk=nblk,
                          half_blocks=half_blocks),
        grid=(2, n_load + nt),
        in_specs=[pl.BlockSpec((tm, d), lambda c, s: (nxt(s), 0)),
                  pl.BlockSpec((1, d), lambda c, s: (0, 0)),
                  mod_spec, mod_spec,
                  col(0, 0), col(0, 1), col(nblk, 0), col(nblk, 1)],
        out_specs=pl.BlockSpec((tm, half_blocks * LANES),
                               lambda c, s: (jnp.maximum(s - n_load, 0), c)),
        out_shape=jax.ShapeDtypeStruct((t, nblk_pad * LANES), BF16),
        scratch_shapes=[pltpu.VMEM((n_load, d, 4 * LANES), BF16),
                        pltpu.VMEM((tm, d), BF16), pltpu.VMEM((tm, d), BF16)],
        compiler_params=_cparams(("arbitrary", "arbitrary"), 58),
        name="ffn_up",
    )(x, g, shift, scale, w_in, w_in, w_in, w_in)


FFN_DOWN_LOAD_BLOCKS = 4
FFN_DOWN_COLS = 512


def _ffn_down_kernel(hid_ref, x_ref, gate_ref, *rest, n_load, nblk, final_norm):
    rest = list(rest)
    r_refs = [rest.pop(0) for _ in range(FFN_DOWN_LOAD_BLOCKS)]
    fg_ref = rest.pop(0) if final_norm else None
    o_ref, w_scr = rest
    s = pl.program_id(0)
    d = o_ref.shape[1]

    @pl.when(s < n_load)
    def _():
        for q, ref in enumerate(r_refs):
            k = FFN_DOWN_LOAD_BLOCKS * s + q
            rows = pl.ds(pl.multiple_of(k * LANES, LANES), LANES)
            w_scr[rows, :] = jnp.where(k < nblk, ref[...], 0.0).astype(BF16)

    @pl.when(s >= n_load)
    def _():
        ssq = None
        for n in range(d // FFN_DOWN_COLS):
            cols = slice(n * FFN_DOWN_COLS, (n + 1) * FFN_DOWN_COLS)
            acc = jnp.dot(hid_ref[...], w_scr[:, cols], preferred_element_type=F32)
            y = x_ref[:, cols] + 0.5 * gate_ref[0, :, cols] * acc
            if final_norm:
                part = jnp.sum(y * y, axis=-1, keepdims=True)
                ssq = part if ssq is None else ssq + part
            o_ref[:, cols] = y
        if final_norm:
            o_ref[...] = o_ref[...] * lax.rsqrt(ssq * (1.0 / d) + EPS) * fg_ref[...]


def _ffn_down_call(hidden, x, gate, w_out, final_g, *, seq, nblk, nblk_pad, tm=512):
    t, d = x.shape
    nt = t // tm
    lb = FFN_DOWN_LOAD_BLOCKS
    n_load = nblk_pad // lb
    tiles_per_seq = seq // tm
    tile = lambda s: jnp.maximum(s - n_load, 0)
    row = lambda q: pl.BlockSpec(
        (LANES, d), lambda s: (jnp.minimum(lb * jnp.minimum(s, n_load - 1) + q, nblk - 1), 0))
    in_specs = [pl.BlockSpec((tm, nblk_pad * LANES), lambda s: (tile(s), 0)),
                pl.BlockSpec((tm, d), lambda s: (tile(s), 0)),
                pl.BlockSpec((1, 1, d), lambda s: (tile(s) // tiles_per_seq, 0, 0))]
    in_specs += [row(q) for q in range(lb)]
    args = [hidden, x, gate] + [w_out] * lb
    if final_g is not None:
        in_specs.append(pl.BlockSpec((1, d), lambda s: (0, 0)))
        args.append(final_g)
    return pl.pallas_call(
        functools.partial(_ffn_down_kernel, n_load=n_load, nblk=nblk,
                          final_norm=final_g is not None),
        grid=(n_load + nt,),
        in_specs=in_specs,
        out_specs=pl.BlockSpec((tm, d), lambda s: (tile(s), 0)),
        out_shape=jax.ShapeDtypeStruct((t, d), F32),
        scratch_shapes=[pltpu.VMEM((nblk_pad * LANES, d), BF16)],
        compiler_params=_cparams(("arbitrary",), 62),
        name="ffn_down_final" if final_g is not None else "ffn_down",
    )(*args)


FFN_BLOCK_ALIGN = 4


def _ffn_resident(x, g, shift, scale, gate, w_in, w_out, final_g, *, seq):
    nblk = w_out.shape[0] // LANES
    nblk_pad = -(-nblk // FFN_BLOCK_ALIGN) * FFN_BLOCK_ALIGN
    hidden = _ffn_up_call(x, g, shift, scale, w_in, seq=seq, nblk=nblk, nblk_pad=nblk_pad)
    return _ffn_down_call(hidden, x, gate, w_out, final_g, seq=seq, nblk=nblk,
                          nblk_pad=nblk_pad)


def _inproj_kernel(xn_ref, g_ref, shn_ref, scn_ref, w_ref, wu_ref, wf_ref, qg_ref, kg_ref,
                   qkv_ref, u_ref, f_ref, h_scr, *, d_attn):
    s = pl.program_id(0)

    @pl.when(s == 0)
    def _():
        h_scr[0] = _norm_mod(xn_ref[...], g_ref[...], shn_ref[0], scn_ref[0]).astype(BF16)

    @pl.when(s >= 1)
    def _():
        cur = (s - 1) % 2

        def project(w_rows):
            return lax.dot_general(h_scr[cur], w_rows, (((1,), (1,)), ((), ())),
                                   preferred_element_type=F32)

        def chunks(src_ref, start, width):
            for c0 in range(start, start + width, MXU_DIM):
                yield c0 - start, project(src_ref[c0:c0 + MXU_DIM, :])

        f_ref[...] = project(wf_ref[...])
        for part, gain_ref in enumerate((qg_ref, kg_ref)):
            gain = gain_ref[...]
            for off, acc in chunks(w_ref, part * d_attn, d_attn):
                for hh in range(MXU_DIM // HEAD_DIM):
                    xh = acc[:, hh * HEAD_DIM:(hh + 1) * HEAD_DIM]
                    ms = jnp.mean(xh * xh, axis=-1, keepdims=True)
                    lo = part * d_attn + off + hh * HEAD_DIM
                    qkv_ref[:, lo:lo + HEAD_DIM] = (
                        xh * lax.rsqrt(ms + EPS) * gain).astype(BF16)
        for off, acc in chunks(w_ref, 2 * d_attn, d_attn):
            lo = 2 * d_attn + off
            qkv_ref[:, lo:lo + MXU_DIM] = acc.astype(BF16)
        for off, acc in chunks(wu_ref, 0, wu_ref.shape[0]):
            u_ref[:, off:off + MXU_DIM] = acc
        h_scr[1 - cur] = _norm_mod(xn_ref[...], g_ref[...], shn_ref[0],
                                   scn_ref[0]).astype(BF16)


def _inproj_call(x, g, shift, scale, w_qkv, w_u, w_f, q_gain, k_gain, *, seq, d_attn,
                 tm=512):
    t, d = x.shape
    nt = t // tm
    d_pool = w_u.shape[0]
    assert w_qkv.shape[0] >= 3 * d_attn
    tiles_per_seq = seq // tm
    nxt = lambda s: jnp.minimum(s, nt - 1)
    tile = lambda s: jnp.maximum(s - 1, 0)
    mod_spec = pl.BlockSpec((1, 1, d), lambda s: (nxt(s) // tiles_per_seq, 0, 0))
    resident = lambda a: pl.BlockSpec(a.shape, lambda s: (0, 0),
                                      pipeline_mode=pl.Buffered(1))
    small = lambda a: pl.BlockSpec(a.shape, lambda s: (0, 0))
    return pl.pallas_call(
        functools.partial(_inproj_kernel, d_attn=d_attn),
        grid=(nt + 1,),
        in_specs=[pl.BlockSpec((tm, d), lambda s: (nxt(s), 0)),
                  small(g), mod_spec, mod_spec,
                  resident(w_qkv), resident(w_u), resident(w_f),
                  small(q_gain), small(k_gain)],
        out_specs=[pl.BlockSpec((tm, 3 * d_attn), lambda s: (tile(s), 0)),
                   pl.BlockSpec((tm, d_pool), lambda s: (tile(s), 0)),
                   pl.BlockSpec((tm, LANES), lambda s: (tile(s), 0))],
        out_shape=[jax.ShapeDtypeStruct((t, 3 * d_attn), BF16),
                   jax.ShapeDtypeStruct((t, d_pool), F32),
                   jax.ShapeDtypeStruct((t, LANES), F32)],
        scratch_shapes=[pltpu.VMEM((2, tm, d), BF16)],
        compiler_params=_cparams(("arbitrary",), 56),
        name="mixer_inproj",
    )(x, g, shift, scale, w_qkv, w_u, w_f, q_gain, k_gain)


N_SPLIT = 3


def _bias_selector(n_heads):
    rows = N_SPLIT * LANES
    hbits = n_heads.bit_length() - 1
    r = lax.broadcasted_iota(jnp.int32, (rows, 2 * LANES), 0)
    c = lax.broadcasted_iota(jnp.int32, (rows, 2 * LANES), 1)
    in_q = c < 2 * N_SPLIT * n_heads
    in_k = (c >= LANES) & (c < LANES + 2 * N_SPLIT * n_heads)
    ck = c - LANES
    slot_q = lax.shift_right_logical(c, hbits)
    slot_k = lax.shift_right_logical(jnp.maximum(ck, 0), hbits)
    head_q = c & (n_heads - 1)
    head_k = ck & (n_heads - 1)
    one_row = r == n_heads
    q_piece = in_q & (slot_q < N_SPLIT) & (r == slot_q * LANES + head_q)
    q_one = in_q & (slot_q >= N_SPLIT) & one_row
    k_one = in_k & (slot_k < N_SPLIT) & one_row
    k_piece = in_k & (slot_k >= N_SPLIT) & (r == (slot_k - N_SPLIT) * LANES + head_k)
    sel = jnp.where(q_piece | q_one | k_one, 1.0, 0.0) - jnp.where(k_piece, 1.0, 0.0)
    return sel.astype(BF16)


def _forget_kernel(f_ref, b_ref, qe_ref, ke_ref, pad_scr, *, seq, n_heads):
    z = f_ref[0] + b_ref[...]
    x = jnp.minimum(z, 0.0) - jnp.log1p(jnp.exp(-jnp.abs(z)))
    pad_scr[pl.ds(0, seq), :] = jnp.zeros((seq, LANES), F32)
    d = 1
    while d < seq:
        pad_scr[pl.ds(seq, seq), :] = x
        x = x + pad_scr[pl.ds(seq - d, seq), :]
        d *= 2
    rem = x * LOG2E
    lane = lax.broadcasted_iota(jnp.int32, (seq, LANES), 1)
    pieces = []
    for s in range(N_SPLIT):
        piece = rem.astype(BF16)
        rem = rem - piece.astype(F32)
        if s == 0:
            piece = jnp.where(lane == n_heads, jnp.ones_like(piece), piece)
        pieces.append(piece)
    e = jnp.dot(jnp.concatenate(pieces, axis=1), _bias_selector(n_heads),
                preferred_element_type=F32)
    qe_ref[0] = e[:, :LANES].astype(BF16)
    ke_ref[0] = e[:, LANES:].astype(BF16)


def _forget_call(f_logit, b_pad, n_heads):
    nb, seq, _ = f_logit.shape
    assert n_heads & (n_heads - 1) == 0 and 2 * N_SPLIT * n_heads <= LANES
    out = jax.ShapeDtypeStruct((nb, seq, LANES), BF16)
    spec = pl.BlockSpec((1, seq, LANES), lambda b: (b, 0, 0))
    return pl.pallas_call(
        functools.partial(_forget_kernel, seq=seq, n_heads=n_heads),
        grid=(nb,),
        in_specs=[spec, pl.BlockSpec((1, LANES), lambda b: (0, 0))],
        out_specs=[spec, spec],
        out_shape=[out, out],
        scratch_shapes=[pltpu.VMEM((2 * seq, LANES), F32)],
        compiler_params=_cparams(("parallel",), 40),
        name="forget_cumsum",
    )(f_logit, b_pad)


POOL_HALO = 16
POOL_ROWS = 512


def _pool_kernel(u_ref, w_ref, s_ref, o_ref, pad_scr, *, seq):
    g = pl.program_id(1)
    cg = u_ref.shape[2]
    pad_scr[pl.ds(0, POOL_HALO), :] = jnp.zeros((POOL_HALO, cg), F32)
    pad_scr[pl.ds(POOL_HALO, seq), :] = u_ref[0]
    w = w_ref[0].astype(BF16)
    scale = s_ref[...]
    for gi, win in enumerate(POOL_WINDOWS):
        @pl.when(g == gi)
        def _(win=win):
            for r0 in range(0, seq, POOL_ROWS):
                tok = pad_scr[pl.ds(POOL_HALO + r0, POOL_ROWS), :]
                tot = tok
                for dd in range(1, win):
                    tot = tot + pad_scr[pl.ds(POOL_HALO + r0 - dd, POOL_ROWS), :]
                pos = r0 + lax.broadcasted_iota(jnp.int32, (POOL_ROWS, cg), 0)
                cnt = jnp.minimum(pos + 1, win).astype(F32)
                p = tot / cnt - tok
                y = jnp.dot(p.astype(BF16), w, preferred_element_type=F32) * scale
                o_ref[0, pl.ds(r0, POOL_ROWS), :] = y.astype(BF16)


def _pool_call(u, pool_w, pool_scale):
    nb, seq, d_pool = u.shape
    ng, cg, _ = pool_w.shape
    return pl.pallas_call(
        functools.partial(_pool_kernel, seq=seq),
        grid=(nb, ng),
        in_specs=[pl.BlockSpec((1, seq, cg), lambda b, g: (b, 0, g)),
                  pl.BlockSpec((1, cg, cg), lambda b, g: (g, 0, 0)),
                  pl.BlockSpec((1, cg), lambda b, g: (0, g))],
        out_specs=pl.BlockSpec((1, seq, cg), lambda b, g: (b, 0, g)),
        out_shape=jax.ShapeDtypeStruct((nb, seq, d_pool), BF16),
        scratch_shapes=[pltpu.VMEM((POOL_HALO + seq, cg), F32)],
        compiler_params=_cparams(("parallel", "arbitrary"), 40),
        name="ms_pool",
    )(u, pool_w, pool_scale)


NEG_BIG = -1e30


def _lane_tile_reduce(x, op):
    out = x[:, :LANES]
    for t in range(1, x.shape[1] // LANES):
        out = op(out, x[:, t * LANES:(t + 1) * LANES])
    return out


def _attn_kernel(q_ref, qe_ref, k_ref, ke_ref, v_ref, o_ref, s_scr, p_scr, ke_scr,
                 *, tq, n_heads):
    seq = q_ref.shape[0]
    lane = lax.broadcasted_iota(jnp.int32, (seq, LANES), 1)
    mine = ((lane & (n_heads - 1)) == pl.program_id(1)) & (lane < 2 * N_SPLIT * n_heads)
    ke = ke_ref[0]
    ke_scr[...] = jnp.where(mine, ke, jnp.zeros_like(ke))
    for i in reversed(range(seq // tq)):
        slot = i % 2
        rows = pl.ds(i * tq, tq)
        qx = jnp.concatenate([q_ref[rows, :], qe_ref[0, rows, :]], axis=1)
        n_chunks = i + 1
        m_tile = None
        for c in range(n_chunks):
            cols = pl.ds(c * tq, tq)
            kx = jnp.concatenate([k_ref[cols, :], ke_scr[cols, :]], axis=1)
            s = lax.dot_general(qx, kx, (((1,), (1,)), ((), ())),
                                preferred_element_type=F32)
            if c == i:
                row = lax.broadcasted_iota(jnp.int32, (tq, tq), 0)
                col = lax.broadcasted_iota(jnp.int32, (tq, tq), 1)
                s = jnp.where(row >= col, s, NEG_BIG)
            s_scr[slot, :, cols] = s
            cm = _lane_tile_reduce(s, jnp.maximum)
            m_tile = cm if m_tile is None else jnp.maximum(m_tile, cm)
        m_row = jnp.max(m_tile, axis=-1, keepdims=True)
        l_tile = None
        for c in range(n_chunks):
            cols = pl.ds(c * tq, tq)
            p = jnp.exp2(s_scr[slot, :, cols] - m_row)
            p_scr[slot, :, cols] = p.astype(BF16)
            cl = _lane_tile_reduce(p, jnp.add)
            l_tile = cl if l_tile is None else l_tile + cl
        l_row = jnp.sum(l_tile, axis=-1, keepdims=True)
        kend = n_chunks * tq
        acc = jnp.dot(p_scr[slot, :, pl.ds(0, kend)], v_ref[pl.ds(0, kend), :],
                      preferred_element_type=F32)
        o_ref[rows, :] = (acc / l_row).astype(BF16)


def _attn_call(qkv, qe, ke, *, nb, seq, n_heads, tq=512):
    t = qkv.shape[0]
    d_attn = n_heads * HEAD_DIM
    head_spec = lambda off: pl.BlockSpec((seq, HEAD_DIM), lambda b, h: (b, off + h))
    bias_spec = pl.BlockSpec((1, seq, LANES), lambda b, h: (b, 0, 0))
    return pl.pallas_call(
        functools.partial(_attn_kernel, tq=tq, n_heads=n_heads),
        grid=(nb, n_heads),
        in_specs=[head_spec(0), bias_spec, head_spec(n_heads), bias_spec,
                  head_spec(2 * n_heads)],
        out_specs=head_spec(0),
        out_shape=jax.ShapeDtypeStruct((t, d_attn), BF16),
        scratch_shapes=[pltpu.VMEM((2, tq, seq), F32), pltpu.VMEM((2, tq, seq), BF16),
                        pltpu.VMEM((seq, LANES), BF16)],
        compiler_params=_cparams(("parallel", "parallel"), 56),
        name="fox_attention",
    )(qkv, qe, qkv, ke, qkv)


def _outproj_kernel(x_ref, attn_ref, pool_ref, wa_ref, wp_ref, gate_ref, o_ref, w_scr):
    da = attn_ref.shape[1]

    @pl.when(pl.program_id(0) == 0)
    def _():
        w_scr[:da, :] = wa_ref[...].astype(BF16)
        w_scr[da:, :] = wp_ref[...].astype(BF16)

    y = jnp.dot(attn_ref[...], w_scr[:da, :], preferred_element_type=F32)
    y = y + jnp.dot(pool_ref[...], w_scr[da:, :], preferred_element_type=F32)
    o_ref[...] = x_ref[...] + gate_ref[0] * y


def _outproj_call(x, attn, pool, w_out, gate, *, seq, tm=512):
    t, d = x.shape
    da = attn.shape[1]
    dp = pool.shape[1]
    assert da == dp and w_out.shape == (da + dp, d)
    tiles_per_seq = seq // tm
    w_spec = lambda half: pl.BlockSpec((da, d), lambda i: (half, 0),
                                       pipeline_mode=pl.Buffered(1))
    return pl.pallas_call(
        _outproj_kernel,
        grid=(t // tm,),
        in_specs=[pl.BlockSpec((tm, d), lambda i: (i, 0)),
                  pl.BlockSpec((tm, da), lambda i: (i, 0)),
                  pl.BlockSpec((tm, dp), lambda i: (i, 0)),
                  w_spec(0), w_spec(1),
                  pl.BlockSpec((1, 1, d), lambda i: (i // tiles_per_seq, 0, 0))],
        out_specs=pl.BlockSpec((tm, d), lambda i: (i, 0)),
        out_shape=jax.ShapeDtypeStruct((t, d), F32),
        scratch_shapes=[pltpu.VMEM((da + dp, d), BF16)],
        compiler_params=_cparams(("arbitrary",), 56),
        name="mixer_outproj",
    )(x, attn, pool, w_out, w_out, gate)


def kernel(x, c, w_ada, b_ada, ffn1_norm_g, ffn1_w_in, ffn1_w_out, mix_norm_g, w_in,
           b_forget, q_norm_g, k_norm_g, pool_w, pool_scale, w_out, ffn2_norm_g,
           ffn2_w_in, ffn2_w_out, final_norm_g):
    nb, seq, d = x.shape
    t = nb * seq
    n_heads = b_forget.shape[1]
    d_attn = n_heads * HEAD_DIM
    d_pool = pool_scale.shape[1]
    depth = w_ada.shape[0]
    xf = x.reshape(t, d)

    c_pad = jnp.pad(c, ((0, 8 - nb), (0, 0)))
    for l in range(depth):
        mod = _ada_call(c_pad, w_ada[l], b_ada[l].reshape(1, -1))[:nb]
        sh1, sc1, g1, sh2, sc2, g2, sh3, sc3, g3 = [
            mod[:, i * d:(i + 1) * d].reshape(nb, 1, d) for i in range(9)]

        xf = _ffn_resident(xf, ffn1_norm_g[l].reshape(1, d), sh1, sc1, g1, ffn1_w_in[l],
                           ffn1_w_out[l], None, seq=seq)

        wl = w_in[l]
        f0 = 3 * d_attn
        w_qkv = wl.T.astype(BF16)
        w_u = w_qkv[f0 + n_heads:]
        w_f = jnp.pad(w_qkv[f0:f0 + n_heads], ((0, LANES - n_heads), (0, 0)))
        q_gain = (q_norm_g[l] * (HEAD_DIM ** -0.5 * LOG2E)).reshape(1, HEAD_DIM)
        k_gain = k_norm_g[l].reshape(1, HEAD_DIM)
        qkv, u, f_logit = _inproj_call(
            xf, mix_norm_g[l].reshape(1, d), sh2, sc2, w_qkv, w_u, w_f, q_gain, k_gain,
            seq=seq, d_attn=d_attn)
        b_pad = jnp.pad(b_forget[l], (0, LANES - n_heads)).reshape(1, LANES)
        qe, ke = _forget_call(f_logit.reshape(nb, seq, LANES), b_pad, n_heads)
        attn = _attn_call(qkv, qe, ke, nb=nb, seq=seq, n_heads=n_heads)
        pool = _pool_call(u.reshape(nb, seq, d_pool), pool_w[l], pool_scale[l].reshape(1, -1))
        xf = _outproj_call(xf, attn, pool.reshape(t, d_pool), w_out[l], g2, seq=seq)

        last = l == depth - 1
        xf = _ffn_resident(xf, ffn2_norm_g[l].reshape(1, d), sh3, sc3, g3, ffn2_w_in[l],
                           ffn2_w_out[l], final_norm_g.reshape(1, d) if last else None,
                           seq=seq)
    return xf.reshape(nb, seq, d)
```

```python
import functools

import jax
import jax.numpy as jnp
from jax import lax
from jax.experimental import pallas as pl
from jax.experimental.pallas import tpu as pltpu

F32 = jnp.float32
BF16 = jnp.bfloat16

EPS = 1e-6
HEAD_DIM = 128
POOL_WINDOWS = (2, 4, 8, 16)
LANES = 128
MXU_DIM = 256
MIB = 1024 * 1024
LOG2E = 1.4426950408889634


def _cparams(dims, vmem_mib):
    return pltpu.CompilerParams(dimension_semantics=dims,
                                vmem_limit_bytes=vmem_mib * MIB)


def _norm_mod(x, g, shift, scale):
    ms = jnp.mean(x * x, axis=-1, keepdims=True)
    return x * lax.rsqrt(ms + EPS) * (g * (1.0 + scale)) + shift


NORM_PIECES = 8
SUBLANES = 8


def _norm_pieces(x_ref, g_ref, shift_ref, scale_ref, h_ref):
    rows = x_ref.shape[0] // NORM_PIECES
    zeros = []
    for k in range(NORM_PIECES):
        sl = pl.ds(k * rows, rows)
        h = _norm_mod(x_ref[sl, :], g_ref[...], shift_ref[0], scale_ref[0])
        h_ref[sl, :] = h.astype(BF16)
        acc = None
        for r in range(0, rows, SUBLANES):
            for c in range(0, h.shape[1], LANES):
                tile = h[r:r + SUBLANES, c:c + LANES]
                acc = tile if acc is None else acc + tile
        zeros.append(acc * 0.0)
    return zeros


def _anchor(a, zero):
    top = 2 * SUBLANES
    z = jnp.concatenate([zero, zero], axis=0)
    a_top = jnp.concatenate([a[:top, :LANES] + z, a[:top, LANES:]], axis=1)
    return jnp.concatenate([a_top, a[top:]], axis=0)


def _ada_kernel(c_ref, w_ref, b_ref, o_ref):
    c = c_ref[...]
    ca = c * (1.0 / (1.0 + jnp.exp(-c)))
    o_ref[...] = jnp.dot(ca.astype(BF16), w_ref[...].astype(BF16),
                         preferred_element_type=F32) + b_ref[...]


def _ada_call(c_pad, w, b, tn=1024):
    m, d = c_pad.shape
    n = w.shape[1]
    return pl.pallas_call(
        _ada_kernel,
        grid=(n // tn,),
        in_specs=[pl.BlockSpec((m, d), lambda j: (0, 0)),
                  pl.BlockSpec((d, tn), lambda j: (0, j)),
                  pl.BlockSpec((1, tn), lambda j: (0, j))],
        out_specs=pl.BlockSpec((m, tn), lambda j: (0, j)),
        out_shape=jax.ShapeDtypeStruct((m, n), F32),
        compiler_params=_cparams(("arbitrary",), 40),
        name="ada_mod",
    )(c_pad, w, b)


def _ffn_up_kernel(xn_ref, g_ref, shn_ref, scn_ref, a0_ref, a1_ref, b0_ref, b1_ref,
                   hid_ref, w_scr, h0_scr, h1_scr, *, n_load, nblk, half_blocks):
    half = pl.program_id(0)
    s = pl.program_id(1)

    @pl.when(s < n_load)
    def _():
        k0 = half * half_blocks + 2 * s
        for q, ref in enumerate((a0_ref, a1_ref, b0_ref, b1_ref)):
            keep = k0 + (q % 2) < nblk
            w_scr[s, :, q * LANES:(q + 1) * LANES] = jnp.where(
                keep, ref[...], 0.0).astype(BF16)

    @pl.when(s == n_load - 1)
    def _():
        h0_scr[...] = _norm_mod(xn_ref[...], g_ref[...], shn_ref[0],
                                scn_ref[0]).astype(BF16)

    def project(h_ref, hn_ref):
        zeros = _norm_pieces(xn_ref, g_ref, shn_ref, scn_ref, hn_ref)
        first = n_load - len(zeros)
        for c in range(n_load):
            hw = jnp.dot(h_ref[...], w_scr[c], preferred_element_type=F32)
            a = hw[:, :2 * LANES]
            b = hw[:, 2 * LANES:]
            if c >= first:
                a = _anchor(a, zeros[c - first])
            hid_ref[:, c * 2 * LANES:(c + 1) * 2 * LANES] = (
                a * (1.0 / (1.0 + jnp.exp(-a))) * b).astype(BF16)

    odd = (s - n_load) % 2

    @pl.when(jnp.logical_and(s >= n_load, odd == 0))
    def _():
        project(h0_scr, h1_scr)

    @pl.when(jnp.logical_and(s >= n_load, odd == 1))
    def _():
        project(h1_scr, h0_scr)


def _ffn_up_call(x, g, shift, scale, w_in, *, seq, nblk, nblk_pad, tm=512):
    t, d = x.shape
    nt = t // tm
    half_blocks = nblk_pad // 2
    n_load = half_blocks // 2
    tiles_per_seq = seq // tm
    nxt = lambda s: jnp.clip(s - n_load + 1, 0, nt - 1)
    blk = lambda c, s, q: jnp.minimum(c * half_blocks + 2 * jnp.minimum(s, n_load - 1) + q,
                                      nblk - 1)
    col = lambda off, q: pl.BlockSpec((d, LANES), lambda c, s: (0, off + blk(c, s, q)))
    mod_spec = pl.BlockSpec((1, 1, d), lambda c, s: (nxt(s) // tiles_per_seq, 0, 0))
    return pl.pallas_call(
        functools.partial(_ffn_up_kernel, n_load=n_load, nblk=nblk,
                          half_blocks=half_blocks),
        grid=(2, n_load + nt),
        in_specs=[pl.BlockSpec((tm, d), lambda c, s: (nxt(s), 0)),
                  pl.BlockSpec((1, d), lambda c, s: (0, 0)),
                  mod_spec, mod_spec,
                  col(0, 0), col(0, 1), col(nblk, 0), col(nblk, 1)],
        out_specs=pl.BlockSpec((tm, half_blocks * LANES),
                               lambda c, s: (jnp.maximum(s - n_load, 0), c)),
        out_shape=jax.ShapeDtypeStruct((t, nblk_pad * LANES), BF16),
        scratch_shapes=[pltpu.VMEM((n_load, d, 4 * LANES), BF16),
                        pltpu.VMEM((tm, d), BF16), pltpu.VMEM((tm, d), BF16)],
        compiler_params=_cparams(("arbitrary", "arbitrary"), 58),
        name="ffn_up",
    )(x, g, shift, scale, w_in, w_in, w_in, w_in)


FFN_DOWN_LOAD_BLOCKS = 4
FFN_DOWN_COLS = 512


def _ffn_down_kernel(hid_ref, x_ref, gate_ref, *rest, n_load, nblk, final_norm):
    rest = list(rest)
    r_refs = [rest.pop(0) for _ in range(FFN_DOWN_LOAD_BLOCKS)]
    fg_ref = rest.pop(0) if final_norm else None
    o_ref, w_scr = rest
    s = pl.program_id(0)
    d = o_ref.shape[1]

    @pl.when(s < n_load)
    def _():
        for q, ref in enumerate(r_refs):
            k = FFN_DOWN_LOAD_BLOCKS * s + q
            rows = pl.ds(pl.multiple_of(k * LANES, LANES), LANES)
            w_scr[rows, :] = jnp.where(k < nblk, ref[...], 0.0).astype(BF16)

    @pl.when(s >= n_load)
    def _():
        ssq = None
        for n in range(d // FFN_DOWN_COLS):
            cols = slice(n * FFN_DOWN_COLS, (n + 1) * FFN_DOWN_COLS)
            acc = jnp.dot(hid_ref[...], w_scr[:, cols], preferred_element_type=F32)
            y = x_ref[:, cols] + 0.5 * gate_ref[0, :, cols] * acc
            if final_norm:
                part = jnp.sum(y * y, axis=-1, keepdims=True)
                ssq = part if ssq is None else ssq + part
            o_ref[:, cols] = y
        if final_norm:
            o_ref[...] = o_ref[...] * lax.rsqrt(ssq * (1.0 / d) + EPS) * fg_ref[...]


def _ffn_down_call(hidden, x, gate, w_out, final_g, *, seq, nblk, nblk_pad, tm=512):
    t, d = x.shape
    nt = t // tm
    lb = FFN_DOWN_LOAD_BLOCKS
    n_load = nblk_pad // lb
    tiles_per_seq = seq // tm
    tile = lambda s: jnp.maximum(s - n_load, 0)
    row = lambda q: pl.BlockSpec(
        (LANES, d), lambda s: (jnp.minimum(lb * jnp.minimum(s, n_load - 1) + q, nblk - 1), 0))
    in_specs = [pl.BlockSpec((tm, nblk_pad * LANES), lambda s: (tile(s), 0)),
                pl.BlockSpec((tm, d), lambda s: (tile(s), 0)),
                pl.BlockSpec((1, 1, d), lambda s: (tile(s) // tiles_per_seq, 0, 0))]
    in_specs += [row(q) for q in range(lb)]
    args = [hidden, x, gate] + [w_out] * lb
    if final_g is not None:
        in_specs.append(pl.BlockSpec((1, d), lambda s: (0, 0)))
        args.append(final_g)
    return pl.pallas_call(
        functools.partial(_ffn_down_kernel, n_load=n_load, nblk=nblk,
                          final_norm=final_g is not None),
        grid=(n_load + nt,),
        in_specs=in_specs,
        out_specs=pl.BlockSpec((tm, d), lambda s: (tile(s), 0)),
        out_shape=jax.ShapeDtypeStruct((t, d), F32),
        scratch_shapes=[pltpu.VMEM((nblk_pad * LANES, d), BF16)],
        compiler_params=_cparams(("arbitrary",), 62),
        name="ffn_down_final" if final_g is not None else "ffn_down",
    )(*args)


FFN_BLOCK_ALIGN = 4


def _ffn_resident(x, g, shift, scale, gate, w_in, w_out, final_g, *, seq):
    nblk = w_out.shape[0] // LANES
    nblk_pad = -(-nblk // FFN_BLOCK_ALIGN) * FFN_BLOCK_ALIGN
    hidden = _ffn_up_call(x, g, shift, scale, w_in, seq=seq, nblk=nblk, nblk_pad=nblk_pad)
    return _ffn_down_call(hidden, x, gate, w_out, final_g, seq=seq, nblk=nblk,
                          nblk_pad=nblk_pad)


def _inproj_kernel(xn_ref, g_ref, shn_ref, scn_ref, w_ref, wu_ref, wf_ref, qg_ref, kg_ref,
                   qkv_ref, u_ref, f_ref, h_scr, *, d_attn):
    s = pl.program_id(0)

    @pl.when(s == 0)
    def _():
        h_scr[0] = _norm_mod(xn_ref[...], g_ref[...], shn_ref[0], scn_ref[0]).astype(BF16)

    @pl.when(s >= 1)
    def _():
        cur = (s - 1) % 2

        def project(w_rows):
            return lax.dot_general(h_scr[cur], w_rows, (((1,), (1,)), ((), ())),
                                   preferred_element_type=F32)

        def chunks(src_ref, start, width):
            for c0 in range(start, start + width, MXU_DIM):
                yield c0 - start, project(src_ref[c0:c0 + MXU_DIM, :])

        f_ref[...] = project(wf_ref[...])
        for part, gain_ref in enumerate((qg_ref, kg_ref)):
            gain = gain_ref[...]
            for off, acc in chunks(w_ref, part * d_attn, d_attn):
                for hh in range(MXU_DIM // HEAD_DIM):
                    xh = acc[:, hh * HEAD_DIM:(hh + 1) * HEAD_DIM]
                    ms = jnp.mean(xh * xh, axis=-1, keepdims=True)
                    lo = part * d_attn + off + hh * HEAD_DIM
                    qkv_ref[:, lo:lo + HEAD_DIM] = (
                        xh * lax.rsqrt(ms + EPS) * gain).astype(BF16)
        for off, acc in chunks(w_ref, 2 * d_attn, d_attn):
            lo = 2 * d_attn + off
            qkv_ref[:, lo:lo + MXU_DIM] = acc.astype(BF16)
        for off, acc in chunks(wu_ref, 0, wu_ref.shape[0]):
            u_ref[:, off:off + MXU_DIM] = acc
        h_scr[1 - cur] = _norm_mod(xn_ref[...], g_ref[...], shn_ref[0],
                                   scn_ref[0]).astype(BF16)


def _inproj_call(x, g, shift, scale, w_qkv, w_u, w_f, q_gain, k_gain, *, seq, d_attn,
                 tm=512):
    t, d = x.shape
    nt = t // tm
    d_pool = w_u.shape[0]
    assert w_qkv.shape[0] >= 3 * d_attn
    tiles_per_seq = seq // tm
    nxt = lambda s: jnp.minimum(s, nt - 1)
    tile = lambda s: jnp.maximum(s - 1, 0)
    mod_spec = pl.BlockSpec((1, 1, d), lambda s: (nxt(s) // tiles_per_seq, 0, 0))
    resident = lambda a: pl.BlockSpec(a.shape, lambda s: (0, 0),
                                      pipeline_mode=pl.Buffered(1))
    small = lambda a: pl.BlockSpec(a.shape, lambda s: (0, 0))
    return pl.pallas_call(
        functools.partial(_inproj_kernel, d_attn=d_attn),
        grid=(nt + 1,),
        in_specs=[pl.BlockSpec((tm, d), lambda s: (nxt(s), 0)),
                  small(g), mod_spec, mod_spec,
                  resident(w_qkv), resident(w_u), resident(w_f),
                  small(q_gain), small(k_gain)],
        out_specs=[pl.BlockSpec((tm, 3 * d_attn), lambda s: (tile(s), 0)),
                   pl.BlockSpec((tm, d_pool), lambda s: (tile(s), 0)),
                   pl.BlockSpec((tm, LANES), lambda s: (tile(s), 0))],
        out_shape=[jax.ShapeDtypeStruct((t, 3 * d_attn), BF16),
                   jax.ShapeDtypeStruct((t, d_pool), F32),
                   jax.ShapeDtypeStruct((t, LANES), F32)],
        scratch_shapes=[pltpu.VMEM((2, tm, d), BF16)],
        compiler_params=_cparams(("arbitrary",), 56),
        name="mixer_inproj",
    )(x, g, shift, scale, w_qkv, w_u, w_f, q_gain, k_gain)


N_SPLIT = 3


def _bias_selector(n_heads):
    rows = N_SPLIT * LANES
    hbits = n_heads.bit_length() - 1
    r = lax.broadcasted_iota(jnp.int32, (rows, 2 * LANES), 0)
    c = lax.broadcasted_iota(jnp.int32, (rows, 2 * LANES), 1)
    in_q = c < 2 * N_SPLIT * n_heads
    in_k = (c >= LANES) & (c < LANES + 2 * N_SPLIT * n_heads)
    ck = c - LANES
    slot_q = lax.shift_right_logical(c, hbits)
    slot_k = lax.shift_right_logical(jnp.maximum(ck, 0), hbits)
    head_q = c & (n_heads - 1)
    head_k = ck & (n_heads - 1)
    one_row = r == n_heads
    q_piece = in_q & (slot_q < N_SPLIT) & (r == slot_q * LANES + head_q)
    q_one = in_q & (slot_q >= N_SPLIT) & one_row
    k_one = in_k & (slot_k < N_SPLIT) & one_row
    k_piece = in_k & (slot_k >= N_SPLIT) & (r == (slot_k - N_SPLIT) * LANES + head_k)
    sel = jnp.where(q_piece | q_one | k_one, 1.0, 0.0) - jnp.where(k_piece, 1.0, 0.0)
    return sel.astype(BF16)


def _forget_kernel(f_ref, b_ref, qe_ref, ke_ref, pad_scr, *, seq, n_heads):
    z = f_ref[0] + b_ref[...]
    x = jnp.minimum(z, 0.0) - jnp.log1p(jnp.exp(-jnp.abs(z)))
    pad_scr[pl.ds(0, seq), :] = jnp.zeros((seq, LANES), F32)
    d = 1
    while d < seq:
        pad_scr[pl.ds(seq, seq), :] = x
        x = x + pad_scr[pl.ds(seq - d, seq), :]
        d *= 2
    rem = x * LOG2E
    lane = lax.broadcasted_iota(jnp.int32, (seq, LANES), 1)
    pieces = []
    for s in range(N_SPLIT):
        piece = rem.astype(BF16)
        rem = rem - piece.astype(F32)
        if s == 0:
            piece = jnp.where(lane == n_heads, jnp.ones_like(piece), piece)
        pieces.append(piece)
    e = jnp.dot(jnp.concatenate(pieces, axis=1), _bias_selector(n_heads),
                preferred_element_type=F32)
    qe_ref[0] = e[:, :LANES].astype(BF16)
    ke_ref[0] = e[:, LANES:].astype(BF16)


def _forget_call(f_logit, b_pad, n_heads):
    nb, seq, _ = f_logit.shape
    assert n_heads & (n_heads - 1) == 0 and 2 * N_SPLIT * n_heads <= LANES
    out = jax.ShapeDtypeStruct((nb, seq, LANES), BF16)
    spec = pl.BlockSpec((1, seq, LANES), lambda b: (b, 0, 0))
    return pl.pallas_call(
        functools.partial(_forget_kernel, seq=seq, n_heads=n_heads),
        grid=(nb,),
        in_specs=[spec, pl.BlockSpec((1, LANES), lambda b: (0, 0))],
        out_specs=[spec, spec],
        out_shape=[out, out],
        scratch_shapes=[pltpu.VMEM((2 * seq, LANES), F32)],
        compiler_params=_cparams(("parallel",), 40),
        name="forget_cumsum",
    )(f_logit, b_pad)


POOL_HALO = 16
POOL_ROWS = 512


def _pool_kernel(u_ref, w_ref, s_ref, o_ref, pad_a, pad_b, *, seq):
    g = pl.program_id(1)
    cg = u_ref.shape[2]
    for pad in (pad_a, pad_b):
        pad[pl.ds(0, POOL_HALO), :] = jnp.zeros((POOL_HALO, cg), F32)
    pad_a[pl.ds(POOL_HALO, seq), :] = u_ref[0]
    w = w_ref[0].astype(BF16)
    scale = s_ref[...]

    def level(src, r0, m):
        return (src[pl.ds(POOL_HALO + r0, POOL_ROWS), :]
                + src[pl.ds(POOL_HALO + r0 - m, POOL_ROWS), :])

    for gi, win in enumerate(POOL_WINDOWS):
        assert win & (win - 1) == 0 and win // 2 <= POOL_HALO

        @pl.when(g == gi)
        def _(win=win):
            src, dst = pad_a, pad_b
            m = 1
            while 2 * m < win:
                for r0 in range(0, seq, POOL_ROWS):
                    dst[pl.ds(POOL_HALO + r0, POOL_ROWS), :] = level(src, r0, m)
                src, dst = dst, src
                m *= 2
            for r0 in range(0, seq, POOL_ROWS):
                tok = u_ref[0, pl.ds(r0, POOL_ROWS), :]
                tot = level(src, r0, m)
                pos = r0 + lax.broadcasted_iota(jnp.int32, (POOL_ROWS, cg), 0)
                cnt = jnp.minimum(pos + 1, win).astype(F32)
                p = tot / cnt - tok
                y = jnp.dot(p.astype(BF16), w, preferred_element_type=F32) * scale
                o_ref[0, pl.ds(r0, POOL_ROWS), :] = y.astype(BF16)


def _pool_call(u, pool_w, pool_scale):
    nb, seq, d_pool = u.shape
    ng, cg, _ = pool_w.shape
    return pl.pallas_call(
        functools.partial(_pool_kernel, seq=seq),
        grid=(nb, ng),
        in_specs=[pl.BlockSpec((1, seq, cg), lambda b, g: (b, 0, g)),
                  pl.BlockSpec((1, cg, cg), lambda b, g: (g, 0, 0)),
                  pl.BlockSpec((1, cg), lambda b, g: (0, g))],
        out_specs=pl.BlockSpec((1, seq, cg), lambda b, g: (b, 0, g)),
        out_shape=jax.ShapeDtypeStruct((nb, seq, d_pool), BF16),
        scratch_shapes=[pltpu.VMEM((POOL_HALO + seq, cg), F32),
                        pltpu.VMEM((POOL_HALO + seq, cg), F32)],
        compiler_params=_cparams(("parallel", "arbitrary"), 40),
        name="ms_pool",
    )(u, pool_w, pool_scale)


NEG_BIG = -1e30


def _lane_tile_reduce(x, op):
    out = x[:, :LANES]
    for t in range(1, x.shape[1] // LANES):
        out = op(out, x[:, t * LANES:(t + 1) * LANES])
    return out


def _attn_kernel(q_ref, qe_ref, k_ref, ke_ref, v_ref, o_ref, s_scr, p_scr, ke_scr,
                 *, tq, n_heads):
    seq = q_ref.shape[0]
    lane = lax.broadcasted_iota(jnp.int32, (seq, LANES), 1)
    mine = ((lane & (n_heads - 1)) == pl.program_id(1)) & (lane < 2 * N_SPLIT * n_heads)
    ke = ke_ref[0]
    ke_scr[...] = jnp.where(mine, ke, jnp.zeros_like(ke))
    for i in reversed(range(seq // tq)):
        slot = i % 2
        rows = pl.ds(i * tq, tq)
        qx = jnp.concatenate([q_ref[rows, :], qe_ref[0, rows, :]], axis=1)
        n_chunks = i + 1
        m_tile = None
        for c in range(n_chunks):
            cols = pl.ds(c * tq, tq)
            kx = jnp.concatenate([k_ref[cols, :], ke_scr[cols, :]], axis=1)
            s = lax.dot_general(qx, kx, (((1,), (1,)), ((), ())),
                                preferred_element_type=F32)
            if c == i:
                row = lax.broadcasted_iota(jnp.int32, (tq, tq), 0)
                col = lax.broadcasted_iota(jnp.int32, (tq, tq), 1)
                s = jnp.where(row >= col, s, NEG_BIG)
            s_scr[slot, :, cols] = s
            cm = _lane_tile_reduce(s, jnp.maximum)
            m_tile = cm if m_tile is None else jnp.maximum(m_tile, cm)
        m_row = jnp.max(m_tile, axis=-1, keepdims=True)
        l_tile = None
        for c in range(n_chunks):
            cols = pl.ds(c * tq, tq)
            p = jnp.exp2(s_scr[slot, :, cols] - m_row)
            p_scr[slot, :, cols] = p.astype(BF16)
            cl = _lane_tile_reduce(p, jnp.add)
            l_tile = cl if l_tile is None else l_tile + cl
        l_row = jnp.sum(l_tile, axis=-1, keepdims=True)
        kend = n_chunks * tq
        acc = jnp.dot(p_scr[slot, :, pl.ds(0, kend)], v_ref[pl.ds(0, kend), :],
                      preferred_element_type=F32)
        o_ref[rows, :] = (acc / l_row).astype(BF16)


def _attn_call(qkv, qe, ke, *, nb, seq, n_heads, tq=512):
    t = qkv.shape[0]
    d_attn = n_heads * HEAD_DIM
    head_spec = lambda off: pl.BlockSpec((seq, HEAD_DIM), lambda b, h: (b, off + h))
    bias_spec = pl.BlockSpec((1, seq, LANES), lambda b, h: (b, 0, 0))
    return pl.pallas_call(
        functools.partial(_attn_kernel, tq=tq, n_heads=n_heads),
        grid=(nb, n_heads),
        in_specs=[head_spec(0), bias_spec, head_spec(n_heads), bias_spec,
                  head_spec(2 * n_heads)],
        out_specs=head_spec(0),
        out_shape=jax.ShapeDtypeStruct((t, d_attn), BF16),
        scratch_shapes=[pltpu.VMEM((2, tq, seq), F32), pltpu.VMEM((2, tq, seq), BF16),
                        pltpu.VMEM((seq, LANES), BF16)],
        compiler_params=_cparams(("parallel", "parallel"), 56),
        name="fox_attention",
    )(qkv, qe, qkv, ke, qkv)


def _outproj_kernel(x_ref, attn_ref, pool_ref, wa_ref, wp_ref, gate_ref, o_ref, w_scr):
    da = attn_ref.shape[1]

    @pl.when(pl.program_id(0) == 0)
    def _():
        w_scr[:da, :] = wa_ref[...].astype(BF16)
        w_scr[da:, :] = wp_ref[...].astype(BF16)

    y = jnp.dot(attn_ref[...], w_scr[:da, :], preferred_element_type=F32)
    y = y + jnp.dot(pool_ref[...], w_scr[da:, :], preferred_element_type=F32)
    o_ref[...] = x_ref[...] + gate_ref[0] * y


def _outproj_call(x, attn, pool, w_out, gate, *, seq, tm=512):
    t, d = x.shape
    da = attn.shape[1]
    dp = pool.shape[1]
    assert da == dp and w_out.shape == (da + dp, d)
    tiles_per_seq = seq // tm
    w_spec = lambda half: pl.BlockSpec((da, d), lambda i: (half, 0),
                                       pipeline_mode=pl.Buffered(1))
    return pl.pallas_call(
        _outproj_kernel,
        grid=(t // tm,),
        in_specs=[pl.BlockSpec((tm, d), lambda i: (i, 0)),
                  pl.BlockSpec((tm, da), lambda i: (i, 0)),
                  pl.BlockSpec((tm, dp), lambda i: (i, 0)),
                  w_spec(0), w_spec(1),
                  pl.BlockSpec((1, 1, d), lambda i: (i // tiles_per_seq, 0, 0))],
        out_specs=pl.BlockSpec((tm, d), lambda i: (i, 0)),
        out_shape=jax.ShapeDtypeStruct((t, d), F32),
        scratch_shapes=[pltpu.VMEM((da + dp, d), BF16)],
        compiler_params=_cparams(("arbitrary",), 56),
        name="mixer_outproj",
    )(x, attn, pool, w_out, w_out, gate)


def kernel(x, c, w_ada, b_ada, ffn1_norm_g, ffn1_w_in, ffn1_w_out, mix_norm_g, w_in,
           b_forget, q_norm_g, k_norm_g, pool_w, pool_scale, w_out, ffn2_norm_g,
           ffn2_w_in, ffn2_w_out, final_norm_g):
    nb, seq, d = x.shape
    t = nb * seq
    n_heads = b_forget.shape[1]
    d_attn = n_heads * HEAD_DIM
    d_pool = pool_scale.shape[1]
    depth = w_ada.shape[0]
    xf = x.reshape(t, d)

    c_pad = jnp.pad(c, ((0, 8 - nb), (0, 0)))
    for l in range(depth):
        mod = _ada_call(c_pad, w_ada[l], b_ada[l].reshape(1, -1))[:nb]
        sh1, sc1, g1, sh2, sc2, g2, sh3, sc3, g3 = [
            mod[:, i * d:(i + 1) * d].reshape(nb, 1, d) for i in range(9)]

        xf = _ffn_resident(xf, ffn1_norm_g[l].reshape(1, d), sh1, sc1, g1, ffn1_w_in[l],
                           ffn1_w_out[l], None, seq=seq)

        wl = w_in[l]
        f0 = 3 * d_attn
        w_qkv = wl.T.astype(BF16)
        w_u = w_qkv[f0 + n_heads:]
        w_f = jnp.pad(w_qkv[f0:f0 + n_heads], ((0, LANES - n_heads), (0, 0)))
        q_gain = (q_norm_g[l] * (HEAD_DIM ** -0.5 * LOG2E)).reshape(1, HEAD_DIM)
        k_gain = k_norm_g[l].reshape(1, HEAD_DIM)
        qkv, u, f_logit = _inproj_call(
            xf, mix_norm_g[l].reshape(1, d), sh2, sc2, w_qkv, w_u, w_f, q_gain, k_gain,
            seq=seq, d_attn=d_attn)
        b_pad = jnp.pad(b_forget[l], (0, LANES - n_heads)).reshape(1, LANES)
        qe, ke = _forget_call(f_logit.reshape(nb, seq, LANES), b_pad, n_heads)
        attn = _attn_call(qkv, qe, ke, nb=nb, seq=seq, n_heads=n_heads)
        pool = _pool_call(u.reshape(nb, seq, d_pool), pool_w[l], pool_scale[l].reshape(1, -1))
        xf = _outproj_call(xf, attn, pool.reshape(t, d_pool), w_out[l], g2, seq=seq)

        last = l == depth - 1
        xf = _ffn_resident(xf, ffn2_norm_g[l].reshape(1, d), sh3, sc3, g3, ffn2_w_in[l],
                           ffn2_w_out[l], final_norm_g.reshape(1, d) if last else None,
                           seq=seq)
    return xf.reshape(nb, seq, d)
```

```python
import functools

import jax
import jax.numpy as jnp
from jax import lax
from jax.experimental import pallas as pl
from jax.experimental.pallas import tpu as pltpu

F32 = jnp.float32
BF16 = jnp.bfloat16

EPS = 1e-6
HEAD_DIM = 128
POOL_WINDOWS = (2, 4, 8, 16)
LANES = 128
MXU_DIM = 256
MIB = 1024 * 1024
LOG2E = 1.4426950408889634


def _cparams(dims, vmem_mib):
    return pltpu.CompilerParams(dimension_semantics=dims,
                                vmem_limit_bytes=vmem_mib * MIB)


def _norm_mod(x, g, shift, scale):
    ms = jnp.mean(x * x, axis=-1, keepdims=True)
    return x * lax.rsqrt(ms + EPS) * (g * (1.0 + scale)) + shift


NORM_PIECES = 8
SUBLANES = 8


def _norm_pieces(x_ref, g_ref, shift_ref, scale_ref, h_ref):
    rows = x_ref.shape[0] // NORM_PIECES
    zeros = []
    for k in range(NORM_PIECES):
        sl = pl.ds(k * rows, rows)
        h = _norm_mod(x_ref[sl, :], g_ref[...], shift_ref[0], scale_ref[0])
        h_ref[sl, :] = h.astype(BF16)
        acc = None
        for r in range(0, rows, SUBLANES):
            for c in range(0, h.shape[1], LANES):
                tile = h[r:r + SUBLANES, c:c + LANES]
                acc = tile if acc is None else acc + tile
        zeros.append(acc * 0.0)
    return zeros


def _anchor(a, zero):
    top = 2 * SUBLANES
    z = jnp.concatenate([zero, zero], axis=0)
    a_top = jnp.concatenate([a[:top, :LANES] + z, a[:top, LANES:]], axis=1)
    return jnp.concatenate([a_top, a[top:]], axis=0)


def _ada_kernel(c_ref, w_ref, b_ref, o_ref):
    c = c_ref[...]
    ca = c * (1.0 / (1.0 + jnp.exp(-c)))
    o_ref[...] = jnp.dot(ca.astype(BF16), w_ref[...].astype(BF16),
                         preferred_element_type=F32) + b_ref[...]


def _ada_call(c_pad, w, b, tn=1024):
    m, d = c_pad.shape
    n = w.shape[1]
    return pl.pallas_call(
        _ada_kernel,
        grid=(n // tn,),
        in_specs=[pl.BlockSpec((m, d), lambda j: (0, 0)),
                  pl.BlockSpec((d, tn), lambda j: (0, j)),
                  pl.BlockSpec((1, tn), lambda j: (0, j))],
        out_specs=pl.BlockSpec((m, tn), lambda j: (0, j)),
        out_shape=jax.ShapeDtypeStruct((m, n), F32),
        compiler_params=_cparams(("arbitrary",), 40),
        name="ada_mod",
    )(c_pad, w, b)


def _ffn_up_kernel(xn_ref, g_ref, shn_ref, scn_ref, a_ref, b0_ref, b1_ref,
                   hid_ref, w_scr, h0_scr, h1_scr, *, n_load, nblk, half_blocks):
    half = pl.program_id(0)
    s = pl.program_id(1)

    @pl.when(s < n_load)
    def _():
        k0 = half * half_blocks + 2 * s
        a = a_ref[...]
        parts = (a[:, :LANES], a[:, LANES:], b0_ref[...], b1_ref[...])
        for q, part in enumerate(parts):
            keep = k0 + (q % 2) < nblk
            w_scr[s, :, q * LANES:(q + 1) * LANES] = jnp.where(
                keep, part, 0.0).astype(BF16)

    @pl.when(s == n_load - 1)
    def _():
        h0_scr[...] = _norm_mod(xn_ref[...], g_ref[...], shn_ref[0],
                                scn_ref[0]).astype(BF16)

    def project(h_ref, hn_ref):
        zeros = _norm_pieces(xn_ref, g_ref, shn_ref, scn_ref, hn_ref)
        first = n_load - len(zeros)
        for c in range(n_load):
            hw = jnp.dot(h_ref[...], w_scr[c], preferred_element_type=F32)
            a = hw[:, :2 * LANES]
            b = hw[:, 2 * LANES:]
            if c >= first:
                a = _anchor(a, zeros[c - first])
            hid_ref[:, c * 2 * LANES:(c + 1) * 2 * LANES] = (
                a * (1.0 / (1.0 + jnp.exp(-a))) * b).astype(BF16)

    odd = (s - n_load) % 2

    @pl.when(jnp.logical_and(s >= n_load, odd == 0))
    def _():
        project(h0_scr, h1_scr)

    @pl.when(jnp.logical_and(s >= n_load, odd == 1))
    def _():
        project(h1_scr, h0_scr)


def _ffn_up_call(x, g, shift, scale, w_in, *, seq, nblk, nblk_pad, tm=512):
    t, d = x.shape
    nt = t // tm
    half_blocks = nblk_pad // 2
    n_load = half_blocks // 2
    tiles_per_seq = seq // tm
    nxt = lambda s: jnp.clip(s - n_load + 1, 0, nt - 1)
    blk = lambda c, s, q: jnp.minimum(c * half_blocks + 2 * jnp.minimum(s, n_load - 1) + q,
                                      nblk - 1)
    col = lambda off, q: pl.BlockSpec((d, LANES), lambda c, s: (0, off + blk(c, s, q)))
    pair = lambda c, s: c * n_load + jnp.minimum(s, n_load - 1)
    a_spec = pl.BlockSpec((d, 2 * LANES), lambda c, s: (0, pair(c, s)))
    mod_spec = pl.BlockSpec((1, 1, d), lambda c, s: (nxt(s) // tiles_per_seq, 0, 0))
    return pl.pallas_call(
        functools.partial(_ffn_up_kernel, n_load=n_load, nblk=nblk,
                          half_blocks=half_blocks),
        grid=(2, n_load + nt),
        in_specs=[pl.BlockSpec((tm, d), lambda c, s: (nxt(s), 0)),
                  pl.BlockSpec((1, d), lambda c, s: (0, 0)),
                  mod_spec, mod_spec,
                  a_spec, col(nblk, 0), col(nblk, 1)],
        out_specs=pl.BlockSpec((tm, half_blocks * LANES),
                               lambda c, s: (jnp.maximum(s - n_load, 0), c)),
        out_shape=jax.ShapeDtypeStruct((t, nblk_pad * LANES), BF16),
        scratch_shapes=[pltpu.VMEM((n_load, d, 4 * LANES), BF16),
                        pltpu.VMEM((tm, d), BF16), pltpu.VMEM((tm, d), BF16)],
        compiler_params=_cparams(("arbitrary", "arbitrary"), 58),
        name="ffn_up",
    )(x, g, shift, scale, w_in, w_in, w_in)


FFN_DOWN_LOAD_BLOCKS = 4
FFN_DOWN_COLS = 512


def _ffn_down_kernel(hid_ref, x_ref, gate_ref, *rest, n_load, nblk, final_norm):
    rest = list(rest)
    r_refs = [rest.pop(0) for _ in range(FFN_DOWN_LOAD_BLOCKS)]
    fg_ref = rest.pop(0) if final_norm else None
    o_ref, w_scr = rest
    s = pl.program_id(0)
    d = o_ref.shape[1]

    @pl.when(s < n_load)
    def _():
        for q, ref in enumerate(r_refs):
            k = FFN_DOWN_LOAD_BLOCKS * s + q
            rows = pl.ds(pl.multiple_of(k * LANES, LANES), LANES)
            w_scr[rows, :] = jnp.where(k < nblk, ref[...], 0.0).astype(BF16)

    @pl.when(s >= n_load)
    def _():
        ssq = None
        for n in range(d // FFN_DOWN_COLS):
            cols = slice(n * FFN_DOWN_COLS, (n + 1) * FFN_DOWN_COLS)
            acc = jnp.dot(hid_ref[...], w_scr[:, cols], preferred_element_type=F32)
            y = x_ref[:, cols] + 0.5 * gate_ref[0, :, cols] * acc
            if final_norm:
                part = jnp.sum(y * y, axis=-1, keepdims=True)
                ssq = part if ssq is None else ssq + part
            o_ref[:, cols] = y
        if final_norm:
            o_ref[...] = o_ref[...] * lax.rsqrt(ssq * (1.0 / d) + EPS) * fg_ref[...]


def _ffn_down_call(hidden, x, gate, w_out, final_g, *, seq, nblk, nblk_pad, tm=512):
    t, d = x.shape
    nt = t // tm
    lb = FFN_DOWN_LOAD_BLOCKS
    n_load = nblk_pad // lb
    tiles_per_seq = seq // tm
    tile = lambda s: jnp.maximum(s - n_load, 0)
    row = lambda q: pl.BlockSpec(
        (LANES, d), lambda s: (jnp.minimum(lb * jnp.minimum(s, n_load - 1) + q, nblk - 1), 0))
    in_specs = [pl.BlockSpec((tm, nblk_pad * LANES), lambda s: (tile(s), 0)),
                pl.BlockSpec((tm, d), lambda s: (tile(s), 0)),
                pl.BlockSpec((1, 1, d), lambda s: (tile(s) // tiles_per_seq, 0, 0))]
    in_specs += [row(q) for q in range(lb)]
    args = [hidden, x, gate] + [w_out] * lb
    if final_g is not None:
        in_specs.append(pl.BlockSpec((1, d), lambda s: (0, 0)))
        args.append(final_g)
    return pl.pallas_call(
        functools.partial(_ffn_down_kernel, n_load=n_load, nblk=nblk,
                          final_norm=final_g is not None),
        grid=(n_load + nt,),
        in_specs=in_specs,
        out_specs=pl.BlockSpec((tm, d), lambda s: (tile(s), 0)),
        out_shape=jax.ShapeDtypeStruct((t, d), F32),
        scratch_shapes=[pltpu.VMEM((nblk_pad * LANES, d), BF16)],
        compiler_params=_cparams(("arbitrary",), 62),
        name="ffn_down_final" if final_g is not None else "ffn_down",
    )(*args)


FFN_BLOCK_ALIGN = 4


def _ffn_resident(x, g, shift, scale, gate, w_in, w_out, final_g, *, seq):
    nblk = w_out.shape[0] // LANES
    nblk_pad = -(-nblk // FFN_BLOCK_ALIGN) * FFN_BLOCK_ALIGN
    hidden = _ffn_up_call(x, g, shift, scale, w_in, seq=seq, nblk=nblk, nblk_pad=nblk_pad)
    return _ffn_down_call(hidden, x, gate, w_out, final_g, seq=seq, nblk=nblk,
                          nblk_pad=nblk_pad)


def _inproj_kernel(xn_ref, g_ref, shn_ref, scn_ref, w_ref, wu_ref, wf_ref, qg_ref, kg_ref,
                   qkv_ref, u_ref, f_ref, h_scr, *, d_attn):
    s = pl.program_id(0)

    @pl.when(s == 0)
    def _():
        h_scr[0] = _norm_mod(xn_ref[...], g_ref[...], shn_ref[0], scn_ref[0]).astype(BF16)

    @pl.when(s >= 1)
    def _():
        cur = (s - 1) % 2

        def project(w_rows):
            return lax.dot_general(h_scr[cur], w_rows, (((1,), (1,)), ((), ())),
                                   preferred_element_type=F32)

        def chunks(src_ref, start, width):
            for c0 in range(start, start + width, MXU_DIM):
                yield c0 - start, project(src_ref[c0:c0 + MXU_DIM, :])

        f_ref[...] = project(wf_ref[...])
        for part, gain_ref in enumerate((qg_ref, kg_ref)):
            gain = gain_ref[...]
            for off, acc in chunks(w_ref, part * d_attn, d_attn):
                for hh in range(MXU_DIM // HEAD_DIM):
                    xh = acc[:, hh * HEAD_DIM:(hh + 1) * HEAD_DIM]
                    ms = jnp.mean(xh * xh, axis=-1, keepdims=True)
                    lo = part * d_attn + off + hh * HEAD_DIM
                    qkv_ref[:, lo:lo + HEAD_DIM] = (
                        xh * lax.rsqrt(ms + EPS) * gain).astype(BF16)
        for off, acc in chunks(w_ref, 2 * d_attn, d_attn):
            lo = 2 * d_attn + off
            qkv_ref[:, lo:lo + MXU_DIM] = acc.astype(BF16)
        for off, acc in chunks(wu_ref, 0, wu_ref.shape[0]):
            u_ref[:, off:off + MXU_DIM] = acc
        h_scr[1 - cur] = _norm_mod(xn_ref[...], g_ref[...], shn_ref[0],
                                   scn_ref[0]).astype(BF16)


def _inproj_call(x, g, shift, scale, w_qkv, w_u, w_f, q_gain, k_gain, *, seq, d_attn,
                 tm=512):
    t, d = x.shape
    nt = t // tm
    d_pool = w_u.shape[0]
    assert w_qkv.shape[0] >= 3 * d_attn
    tiles_per_seq = seq // tm
    nxt = lambda s: jnp.minimum(s, nt - 1)
    tile = lambda s: jnp.maximum(s - 1, 0)
    mod_spec = pl.BlockSpec((1, 1, d), lambda s: (nxt(s) // tiles_per_seq, 0, 0))
    resident = lambda a: pl.BlockSpec(a.shape, lambda s: (0, 0),
                                      pipeline_mode=pl.Buffered(1))
    small = lambda a: pl.BlockSpec(a.shape, lambda s: (0, 0))
    return pl.pallas_call(
        functools.partial(_inproj_kernel, d_attn=d_attn),
        grid=(nt + 1,),
        in_specs=[pl.BlockSpec((tm, d), lambda s: (nxt(s), 0)),
                  small(g), mod_spec, mod_spec,
                  resident(w_qkv), resident(w_u), resident(w_f),
                  small(q_gain), small(k_gain)],
        out_specs=[pl.BlockSpec((tm, 3 * d_attn), lambda s: (tile(s), 0)),
                   pl.BlockSpec((tm, d_pool), lambda s: (tile(s), 0)),
                   pl.BlockSpec((tm, LANES), lambda s: (tile(s), 0))],
        out_shape=[jax.ShapeDtypeStruct((t, 3 * d_attn), BF16),
                   jax.ShapeDtypeStruct((t, d_pool), F32),
                   jax.ShapeDtypeStruct((t, LANES), F32)],
        scratch_shapes=[pltpu.VMEM((2, tm, d), BF16)],
        compiler_params=_cparams(("arbitrary",), 56),
        name="mixer_inproj",
    )(x, g, shift, scale, w_qkv, w_u, w_f, q_gain, k_gain)


N_SPLIT = 3


def _bias_selector(n_heads):
    rows = N_SPLIT * LANES
    hbits = n_heads.bit_length() - 1
    r = lax.broadcasted_iota(jnp.int32, (rows, 2 * LANES), 0)
    c = lax.broadcasted_iota(jnp.int32, (rows, 2 * LANES), 1)
    in_q = c < 2 * N_SPLIT * n_heads
    in_k = (c >= LANES) & (c < LANES + 2 * N_SPLIT * n_heads)
    ck = c - LANES
    slot_q = lax.shift_right_logical(c, hbits)
    slot_k = lax.shift_right_logical(jnp.maximum(ck, 0), hbits)
    head_q = c & (n_heads - 1)
    head_k = ck & (n_heads - 1)
    one_row = r == n_heads
    q_piece = in_q & (slot_q < N_SPLIT) & (r == slot_q * LANES + head_q)
    q_one = in_q & (slot_q >= N_SPLIT) & one_row
    k_one = in_k & (slot_k < N_SPLIT) & one_row
    k_piece = in_k & (slot_k >= N_SPLIT) & (r == (slot_k - N_SPLIT) * LANES + head_k)
    sel = jnp.where(q_piece | q_one | k_one, 1.0, 0.0) - jnp.where(k_piece, 1.0, 0.0)
    return sel.astype(BF16)


def _forget_kernel(f_ref, b_ref, qe_ref, ke_ref, pad_scr, *, seq, n_heads):
    z = f_ref[0] + b_ref[...]
    x = jnp.minimum(z, 0.0) - jnp.log1p(jnp.exp(-jnp.abs(z)))
    pad_scr[pl.ds(0, seq), :] = jnp.zeros((seq, LANES), F32)
    d = 1
    while d < seq:
        pad_scr[pl.ds(seq, seq), :] = x
        x = x + pad_scr[pl.ds(seq - d, seq), :]
        d *= 2
    rem = x * LOG2E
    lane = lax.broadcasted_iota(jnp.int32, (seq, LANES), 1)
    pieces = []
    for s in range(N_SPLIT):
        piece = rem.astype(BF16)
        rem = rem - piece.astype(F32)
        if s == 0:
            piece = jnp.where(lane == n_heads, jnp.ones_like(piece), piece)
        pieces.append(piece)
    e = jnp.dot(jnp.concatenate(pieces, axis=1), _bias_selector(n_heads),
                preferred_element_type=F32)
    qe_ref[0] = e[:, :LANES].astype(BF16)
    ke_ref[0] = e[:, LANES:].astype(BF16)


def _forget_call(f_logit, b_pad, n_heads):
    nb, seq, _ = f_logit.shape
    assert n_heads & (n_heads - 1) == 0 and 2 * N_SPLIT * n_heads <= LANES
    out = jax.ShapeDtypeStruct((nb, seq, LANES), BF16)
    spec = pl.BlockSpec((1, seq, LANES), lambda b: (b, 0, 0))
    return pl.pallas_call(
        functools.partial(_forget_kernel, seq=seq, n_heads=n_heads),
        grid=(nb,),
        in_specs=[spec, pl.BlockSpec((1, LANES), lambda b: (0, 0))],
        out_specs=[spec, spec],
        out_shape=[out, out],
        scratch_shapes=[pltpu.VMEM((2 * seq, LANES), F32)],
        compiler_params=_cparams(("parallel",), 40),
        name="forget_cumsum",
    )(f_logit, b_pad)


POOL_HALO = 16
POOL_ROWS = 512


def _pool_kernel(u_ref, w_ref, s_ref, o_ref, pad_a, pad_b, *, seq):
    g = pl.program_id(1)
    cg = u_ref.shape[2]
    for pad in (pad_a, pad_b):
        pad[pl.ds(0, POOL_HALO), :] = jnp.zeros((POOL_HALO, cg), F32)
    pad_a[pl.ds(POOL_HALO, seq), :] = u_ref[0]
    w = w_ref[0].astype(BF16)
    scale = s_ref[...]

    def level(src, r0, m):
        return (src[pl.ds(POOL_HALO + r0, POOL_ROWS), :]
                + src[pl.ds(POOL_HALO + r0 - m, POOL_ROWS), :])

    for gi, win in enumerate(POOL_WINDOWS):
        assert win & (win - 1) == 0 and win // 2 <= POOL_HALO

        @pl.when(g == gi)
        def _(win=win):
            src, dst = pad_a, pad_b
            m = 1
            while 2 * m < win:
                for r0 in range(0, seq, POOL_ROWS):
                    dst[pl.ds(POOL_HALO + r0, POOL_ROWS), :] = level(src, r0, m)
                src, dst = dst, src
                m *= 2
            for r0 in range(0, seq, POOL_ROWS):
                tok = u_ref[0, pl.ds(r0, POOL_ROWS), :]
                tot = level(src, r0, m)
                pos = r0 + lax.broadcasted_iota(jnp.int32, (POOL_ROWS, cg), 0)
                cnt = jnp.minimum(pos + 1, win).astype(F32)
                p = tot / cnt - tok
                y = jnp.dot(p.astype(BF16), w, preferred_element_type=F32) * scale
                o_ref[0, pl.ds(r0, POOL_ROWS), :] = y.astype(BF16)


def _pool_call(u, pool_w, pool_scale):
    nb, seq, d_pool = u.shape
    ng, cg, _ = pool_w.shape
    return pl.pallas_call(
        functools.partial(_pool_kernel, seq=seq),
        grid=(nb, ng),
        in_specs=[pl.BlockSpec((1, seq, cg), lambda b, g: (b, 0, g)),
                  pl.BlockSpec((1, cg, cg), lambda b, g: (g, 0, 0)),
                  pl.BlockSpec((1, cg), lambda b, g: (0, g))],
        out_specs=pl.BlockSpec((1, seq, cg), lambda b, g: (b, 0, g)),
        out_shape=jax.ShapeDtypeStruct((nb, seq, d_pool), BF16),
        scratch_shapes=[pltpu.VMEM((POOL_HALO + seq, cg), F32),
                        pltpu.VMEM((POOL_HALO + seq, cg), F32)],
        compiler_params=_cparams(("parallel", "arbitrary"), 40),
        name="ms_pool",
    )(u, pool_w, pool_scale)


NEG_BIG = -1e30


def _lane_tile_reduce(x, op):
    out = x[:, :LANES]
    for t in range(1, x.shape[1] // LANES):
        out = op(out, x[:, t * LANES:(t + 1) * LANES])
    return out


def _attn_kernel(q_ref, qe_ref, k_ref, ke_ref, v_ref, o_ref, s_scr, p_scr, ke_scr,
                 *, tq, n_heads):
    seq = q_ref.shape[0]
    lane = lax.broadcasted_iota(jnp.int32, (seq, LANES), 1)
    mine = ((lane & (n_heads - 1)) == pl.program_id(1)) & (lane < 2 * N_SPLIT * n_heads)
    ke = ke_ref[0]
    ke_scr[...] = jnp.where(mine, ke, jnp.zeros_like(ke))
    for i in reversed(range(seq // tq)):
        slot = i % 2
        rows = pl.ds(i * tq, tq)
        qx = jnp.concatenate([q_ref[rows, :], qe_ref[0, rows, :]], axis=1)
        n_chunks = i + 1
        m_tile = None
        for c in range(n_chunks):
            cols = pl.ds(c * tq, tq)
            kx = jnp.concatenate([k_ref[cols, :], ke_scr[cols, :]], axis=1)
            s = lax.dot_general(qx, kx, (((1,), (1,)), ((), ())),
                                preferred_element_type=F32)
            if c == i:
                row = lax.broadcasted_iota(jnp.int32, (tq, tq), 0)
                col = lax.broadcasted_iota(jnp.int32, (tq, tq), 1)
                s = jnp.where(row >= col, s, NEG_BIG)
            s_scr[slot, :, cols] = s
            cm = _lane_tile_reduce(s, jnp.maximum)
            m_tile = cm if m_tile is None else jnp.maximum(m_tile, cm)
        m_row = jnp.max(m_tile, axis=-1, keepdims=True)
        l_tile = None
        for c in range(n_chunks):
            cols = pl.ds(c * tq, tq)
            p = jnp.exp2(s_scr[slot, :, cols] - m_row)
            p_scr[slot, :, cols] = p.astype(BF16)
            cl = _lane_tile_reduce(p, jnp.add)
            l_tile = cl if l_tile is None else l_tile + cl
        l_row = jnp.sum(l_tile, axis=-1, keepdims=True)
        kend = n_chunks * tq
        acc = jnp.dot(p_scr[slot, :, pl.ds(0, kend)], v_ref[pl.ds(0, kend), :],
                      preferred_element_type=F32)
        o_ref[rows, :] = (acc / l_row).astype(BF16)


def _attn_call(qkv, qe, ke, *, nb, seq, n_heads, tq=512):
    t = qkv.shape[0]
    d_attn = n_heads * HEAD_DIM
    head_spec = lambda off: pl.BlockSpec((seq, HEAD_DIM), lambda b, h: (b, off + h))
    bias_spec = pl.BlockSpec((1, seq, LANES), lambda b, h: (b, 0, 0))
    return pl.pallas_call(
        functools.partial(_attn_kernel, tq=tq, n_heads=n_heads),
        grid=(nb, n_heads),
        in_specs=[head_spec(0), bias_spec, head_spec(n_heads), bias_spec,
                  head_spec(2 * n_heads)],
        out_specs=head_spec(0),
        out_shape=jax.ShapeDtypeStruct((t, d_attn), BF16),
        scratch_shapes=[pltpu.VMEM((2, tq, seq), F32), pltpu.VMEM((2, tq, seq), BF16),
                        pltpu.VMEM((seq, LANES), BF16)],
        compiler_params=_cparams(("parallel", "parallel"), 56),
        name="fox_attention",
    )(qkv, qe, qkv, ke, qkv)


def _outproj_kernel(x_ref, attn_ref, pool_ref, wa_ref, wp_ref, gate_ref, o_ref, w_scr):
    da = attn_ref.shape[1]

    @pl.when(pl.program_id(0) == 0)
    def _():
        w_scr[:da, :] = wa_ref[...].astype(BF16)
        w_scr[da:, :] = wp_ref[...].astype(BF16)

    y = jnp.dot(attn_ref[...], w_scr[:da, :], preferred_element_type=F32)
    y = y + jnp.dot(pool_ref[...], w_scr[da:, :], preferred_element_type=F32)
    o_ref[...] = x_ref[...] + gate_ref[0] * y


def _outproj_call(x, attn, pool, w_out, gate, *, seq, tm=512):
    t, d = x.shape
    da = attn.shape[1]
    dp = pool.shape[1]
    assert da == dp and w_out.shape == (da + dp, d)
    tiles_per_seq = seq // tm
    w_spec = lambda half: pl.BlockSpec((da, d), lambda i: (half, 0),
                                       pipeline_mode=pl.Buffered(1))
    return pl.pallas_call(
        _outproj_kernel,
        grid=(t // tm,),
        in_specs=[pl.BlockSpec((tm, d), lambda i: (i, 0)),
                  pl.BlockSpec((tm, da), lambda i: (i, 0)),
                  pl.BlockSpec((tm, dp), lambda i: (i, 0)),
                  w_spec(0), w_spec(1),
                  pl.BlockSpec((1, 1, d), lambda i: (i // tiles_per_seq, 0, 0))],
        out_specs=pl.BlockSpec((tm, d), lambda i: (i, 0)),
        out_shape=jax.ShapeDtypeStruct((t, d), F32),
        scratch_shapes=[pltpu.VMEM((da + dp, d), BF16)],
        compiler_params=_cparams(("arbitrary",), 56),
        name="mixer_outproj",
    )(x, attn, pool, w_out, w_out, gate)


def kernel(x, c, w_ada, b_ada, ffn1_norm_g, ffn1_w_in, ffn1_w_out, mix_norm_g, w_in,
           b_forget, q_norm_g, k_norm_g, pool_w, pool_scale, w_out, ffn2_norm_g,
           ffn2_w_in, ffn2_w_out, final_norm_g):
    nb, seq, d = x.shape
    t = nb * seq
    n_heads = b_forget.shape[1]
    d_attn = n_heads * HEAD_DIM
    d_pool = pool_scale.shape[1]
    depth = w_ada.shape[0]
    xf = x.reshape(t, d)

    c_pad = jnp.pad(c, ((0, 8 - nb), (0, 0)))
    for l in range(depth):
        mod = _ada_call(c_pad, w_ada[l], b_ada[l].reshape(1, -1))[:nb]
        sh1, sc1, g1, sh2, sc2, g2, sh3, sc3, g3 = [
            mod[:, i * d:(i + 1) * d].reshape(nb, 1, d) for i in range(9)]

        xf = _ffn_resident(xf, ffn1_norm_g[l].reshape(1, d), sh1, sc1, g1, ffn1_w_in[l],
                           ffn1_w_out[l], None, seq=seq)

        wl = w_in[l]
        f0 = 3 * d_attn
        w_qkv = wl.T.astype(BF16)
        w_u = w_qkv[f0 + n_heads:]
        w_f = jnp.pad(w_qkv[f0:f0 + n_heads], ((0, LANES - n_heads), (0, 0)))
        q_gain = (q_norm_g[l] * (HEAD_DIM ** -0.5 * LOG2E)).reshape(1, HEAD_DIM)
        k_gain = k_norm_g[l].reshape(1, HEAD_DIM)
        qkv, u, f_logit = _inproj_call(
            xf, mix_norm_g[l].reshape(1, d), sh2, sc2, w_qkv, w_u, w_f, q_gain, k_gain,
            seq=seq, d_attn=d_attn)
        b_pad = jnp.pad(b_forget[l], (0, LANES - n_heads)).reshape(1, LANES)
        qe, ke = _forget_call(f_logit.reshape(nb, seq, LANES), b_pad, n_heads)
        attn = _attn_call(qkv, qe, ke, nb=nb, seq=seq, n_heads=n_heads)
        pool = _pool_call(u.reshape(nb, seq, d_pool), pool_w[l], pool_scale[l].reshape(1, -1))
        xf = _outproj_call(xf, attn, pool.reshape(t, d_pool), w_out[l], g2, seq=seq)

        last = l == depth - 1
        xf = _ffn_resident(xf, ffn2_norm_g[l].reshape(1, d), sh3, sc3, g3, ffn2_w_in[l],
                           ffn2_w_out[l], final_norm_g.reshape(1, d) if last else None,
                           seq=seq)
    return xf.reshape(nb, seq, d)
```
